```python
import math
import jax
import jax.numpy as jnp
from jax import lax
import numpy as np

D_MODEL = 2048
BATCH = 2
SEQ = 8192
DEPTH = 4

N_MIXERS = 3
HEAD_DIM = 128
ROPE_THETA = 10000.0
NORM_EPS = 1e-6
D_FF = 4 * D_MODEL
BAND_BLOCK = 128

NSA_HEADS = D_MODEL // HEAD_DIM
NSA_KV_GROUPS = 4
NSA_REP = NSA_HEADS // NSA_KV_GROUPS
NSA_KV_WIDTH = NSA_KV_GROUPS * HEAD_DIM
CMP_BLOCK = 32
CMP_STRIDE = 16
CMP_HIDDEN = 4 * HEAD_DIM
SEL_BLOCK = 64
SEL_TOPK = 16
NSA_WINDOW = 512
NSA_Q_CHUNK = 64
FORCED_BONUS = 1e9
NSA_IN_WIDTH = NSA_HEADS * HEAD_DIM + 6 * NSA_KV_WIDTH + 3 * NSA_HEADS

DIL_HEADS = D_MODEL // HEAD_DIM
DIL_PATTERNS = ((128, 1), (512, 4), (2048, 16))
DIL_IN_WIDTH = len(DIL_PATTERNS) * 3 * DIL_HEADS * HEAD_DIM

D_RNN = 2688
RNN_BLOCKS = 16
RNN_BLOCK_DIM = D_RNN // RNN_BLOCKS
CONV_WIDTH = 4
LRU_C = 8.0

kernel_name = 'hybrid_nsa_dilated_rglru_adaln'

F32 = jnp.float32


def rmsnorm(x, gain):
    x32 = x.astype(F32)
    return x32 * lax.rsqrt(jnp.mean(x32 * x32, axis=-1, keepdims=True) + NORM_EPS) * gain.astype(F32)


def modulate(x, gain, shift, scale):
    return (rmsnorm(x, gain) * (1.0 + scale.astype(F32)) + shift.astype(F32)).astype(x.dtype)


def rope_tables(seq):
    inv_freq = ROPE_THETA ** (-jnp.arange(0, HEAD_DIM, 2, dtype=F32) / HEAD_DIM)
    ang = jnp.arange(seq, dtype=F32)[:, None] * inv_freq[None, :]
    return jnp.cos(ang), jnp.sin(ang)


def rope(t, cos, sin):
    t1, t2 = jnp.split(t, 2, axis=-1)
    return jnp.concatenate([t1 * cos - t2 * sin, t1 * sin + t2 * cos], axis=-1)


def banded_attention(q, k, v, window, block):
    q, k, v = q.astype(F32), k.astype(F32), v.astype(F32)
    N, G, R, L, dh = q.shape
    blk = min(block, L)
    nb = -(-L // blk)
    lp = nb * blk
    nw = -(-window // blk)
    q = jnp.pad(q, ((0, 0), (0, 0), (0, 0), (0, lp - L), (0, 0)))
    kv_pad = ((0, 0), (0, 0), (nw * blk, lp - L), (0, 0))
    kb = jnp.pad(k, kv_pad).reshape(N, G, nb + nw, blk, dh)
    vb = jnp.pad(v, kv_pad).reshape(N, G, nb + nw, blk, dh)
    kw = jnp.concatenate([kb[:, :, s:s + nb] for s in range(nw + 1)], axis=3)
    vw = jnp.concatenate([vb[:, :, s:s + nb] for s in range(nw + 1)], axis=3)
    qb = q.reshape(N, G, R, nb, blk, dh)
    s = jnp.einsum('ngrbqd,ngbkd->ngrbqk', qb, kw) / math.sqrt(dh)
    qpos = jnp.arange(nb)[:, None] * blk + jnp.arange(blk)[None, :]
    kpos = jnp.arange(nb)[:, None] * blk - nw * blk + jnp.arange((nw + 1) * blk)[None, :]
    diff = qpos[:, :, None] - kpos[:, None, :]
    mask = (diff >= 0) & (diff <= window) & (kpos[:, None, :] >= 0)
    s = jnp.where(mask, s, -jnp.inf)
    m = jnp.max(s, axis=-1, keepdims=True)
    p = jnp.exp(s - m)
    l = jnp.sum(p, axis=-1, keepdims=True)
    out = jnp.einsum('ngrbqk,ngbkd->ngrbqd', p, vw) / l
    lse = (m + jnp.log(l))[..., 0]
    out = out.reshape(N, G, R, lp, dh)[:, :, :, :L]
    lse = lse.reshape(N, G, R, lp)[:, :, :, :L]
    return out, lse


def nsa_mixer(h, w_in, cmp_pe, cmp_w1, cmp_w2, w_out, cos, sin):
    B, S, _ = h.shape
    G, R, dh = NSA_KV_GROUPS, NSA_REP, HEAD_DIM
    scale = 1.0 / math.sqrt(dh)
    hq = NSA_HEADS * dh
    proj = (h @ w_in).astype(F32)
    q = proj[..., :hq].reshape(B, S, G, R, dh).transpose(0, 2, 3, 1, 4)
    kv = proj[..., hq:hq + 6 * NSA_KV_WIDTH].reshape(B, S, 6, G, dh).transpose(2, 0, 3, 1, 4)
    k_cmp, v_cmp, k_sel, v_sel, k_win, v_win = kv[0], kv[1], kv[2], kv[3], kv[4], kv[5]
    gates = jax.nn.sigmoid(proj[..., hq + 6 * NSA_KV_WIDTH:]).reshape(B, S, G, R, 3)
    gates = gates.transpose(4, 0, 2, 3, 1)[..., None]
    q_rot = rope(q, cos, sin)
    k_sel = rope(k_sel, cos, sin)
    k_win = rope(k_win, cos, sin)

    ratio = CMP_BLOCK // CMP_STRIDE
    n_cmp = S // CMP_STRIDE - ratio + 1

    def compress(t, pe, w1, w2):
        pieces = t.reshape(B, G, S // CMP_STRIDE, CMP_STRIDE, dh)
        blocks = jnp.concatenate([pieces[:, :, j:j + n_cmp] for j in range(ratio)], axis=3) + pe.astype(F32)
        flat = blocks.reshape(B, G, n_cmp, CMP_BLOCK * dh)
        return jax.nn.gelu(flat @ w1.astype(F32)) @ w2.astype(F32)

    k_c = compress(k_cmp, cmp_pe[0], cmp_w1[0], cmp_w2[0])
    v_c = compress(v_cmp, cmp_pe[1], cmp_w1[1], cmp_w2[1])
    cmp_start = jnp.arange(n_cmp) * CMP_STRIDE
    cmp_end = cmp_start + CMP_BLOCK - 1

    n_sb = S // SEL_BLOCK
    n_top = min(SEL_TOPK, n_sb)
    sel_ids = jnp.arange(n_sb)
    sel_start = sel_ids * SEL_BLOCK
    overlap = ((cmp_start[:, None] < sel_start[None, :] + SEL_BLOCK)
               & (cmp_start[:, None] + CMP_BLOCK > sel_start[None, :])).astype(F32)
    k_sb = k_sel.reshape(B, G, n_sb, SEL_BLOCK, dh)
    v_sb = v_sel.reshape(B, G, n_sb, SEL_BLOCK, dh)
    gather_blocks = jax.vmap(jax.vmap(lambda blocks, ix: blocks[ix]))

    def chunk(ci):
        start = ci * NSA_Q_CHUNK
        qc = lax.dynamic_slice_in_dim(q, start, NSA_Q_CHUNK, axis=3)
        qrc = lax.dynamic_slice_in_dim(q_rot, start, NSA_Q_CHUNK, axis=3)
        t = start + jnp.arange(NSA_Q_CHUNK)
        s = jnp.einsum('bgrqd,bgnd->bgrqn', qc, k_c) * scale
        s = jnp.where(cmp_end[None, :] <= t[:, None], s, -jnp.inf)
        m = jnp.max(s, axis=-1, keepdims=True)
        m = jnp.where(jnp.isfinite(m), m, 0.0)
        p = jnp.exp(s - m)
        l = jnp.sum(p, axis=-1, keepdims=True)
        p = p / jnp.where(l > 0, l, 1.0)
        o_c = jnp.einsum('bgrqn,bgnd->bgrqd', p, v_c)
        imp = jnp.einsum('bgrqn,nj->bgqj', p, overlap)
        cur = t // SEL_BLOCK
        avail = sel_ids[None, :] <= cur[:, None]
        forced = (sel_ids[None, :] == 0) | (sel_ids[None, :] == cur[:, None]) | (sel_ids[None, :] == cur[:, None] - 1)
        score = jnp.where(avail, imp + FORCED_BONUS * forced, -jnp.inf)
        _, idx = lax.top_k(score, n_top)
        ks = gather_blocks(k_sb, idx)
        vs = gather_blocks(v_sb, idx)
        ss = jnp.einsum('bgrqd,bgqnkd->bgrqnk', qrc, ks) * scale
        kpos = idx[..., None] * SEL_BLOCK + jnp.arange(SEL_BLOCK)
        ok = kpos <= t[:, None, None]
        ss = jnp.where(ok[:, :, None], ss, -jnp.inf)
        ps = jax.nn.softmax(ss.reshape(ss.shape[:4] + (-1,)), axis=-1).reshape(ss.shape)
        o_s = jnp.einsum('bgrqnk,bgqnkd->bgrqd', ps, vs)
        return o_c, o_s

    o_cmp, o_sel = lax.map(chunk, jnp.arange(S // NSA_Q_CHUNK))
    o_cmp = jnp.moveaxis(o_cmp, 0, 3).reshape(B, G, R, S, dh)
    o_sel = jnp.moveaxis(o_sel, 0, 3).reshape(B, G, R, S, dh)
    o_win, _ = banded_attention(q_rot, k_win, v_win, NSA_WINDOW - 1, BAND_BLOCK)
    o = gates[0] * o_cmp + gates[1] * o_sel + gates[2] * o_win
    o = o.transpose(0, 3, 1, 2, 4).reshape(B, S, hq).astype(h.dtype)
    return o @ w_out


def dilated_mixer(h, w_in, w_out, cos, sin):
    B, S, _ = h.shape
    H, dh = DIL_HEADS, HEAD_DIM
    proj = (h @ w_in).astype(F32).reshape(B, S, len(DIL_PATTERNS), 3, H, dh).transpose(2, 3, 0, 4, 1, 5)
    outs, lses = [], []
    for g, (window, dil) in enumerate(DIL_PATTERNS):
        L = S // dil

        def to_residue(t):
            return t.reshape(B, H, L, dil, dh).transpose(0, 1, 3, 2, 4).reshape(B, H * dil, L, dh)

        q = to_residue(rope(proj[g, 0], cos, sin))[:, :, None]
        k = to_residue(rope(proj[g, 1], cos, sin))
        v = to_residue(proj[g, 2])
        o, lse = banded_attention(q, k, v, window // dil, BAND_BLOCK)
        outs.append(o.reshape(B, H, dil, L, dh).transpose(0, 1, 3, 2, 4).reshape(B, H, S, dh))
        lses.append(lse.reshape(B, H, dil, L).transpose(0, 1, 3, 2).reshape(B, H, S))
    wts = jax.nn.softmax(jnp.stack(lses), axis=0)[..., None]
    o = jnp.sum(wts * jnp.stack(outs), axis=0)
    o = o.transpose(0, 2, 1, 3).reshape(B, S, H * dh).astype(h.dtype)
    return o @ w_out


def rglru_mixer(h, w_in, conv_w, conv_b, w_gate, b_gate, lru_lambda, w_out):
    B, S, _ = h.shape
    proj = (h @ w_in).astype(F32)
    y = jax.nn.gelu(proj[..., :D_RNN])
    xr = proj[..., D_RNN:]
    xp = jnp.pad(xr, ((0, 0), (CONV_WIDTH - 1, 0), (0, 0)))
    x = conv_b.astype(F32) + xp[:, 0:S] * conv_w[0].astype(F32)
    for j in range(1, CONV_WIDTH):
        x = x + xp[:, j:j + S] * conv_w[j].astype(F32)
    xb = x.reshape(B, S, RNN_BLOCKS, RNN_BLOCK_DIM)
    gl = jnp.einsum('bsnc,gncd->gbsnd', xb, w_gate.astype(F32)).reshape(2, B, S, D_RNN)
    gl = gl + b_gate.astype(F32)[:, None, None, :]
    r = jax.nn.sigmoid(gl[0])
    i = jax.nn.sigmoid(gl[1])
    log_a = -LRU_C * r * jax.nn.softplus(-lru_lambda.astype(F32))
    a = jnp.exp(log_a)
    b = jnp.sqrt(-jnp.expm1(2.0 * log_a)) * (i * x)

    def combine(left, right):
        a1, b1 = left
        a2, b2 = right
        return a1 * a2, a2 * b1 + b2

    hs = lax.associative_scan(combine, (a, b), axis=1)[1]
    return (hs * y).astype(h.dtype) @ w_out


def sqrelu_mlp(h, w1, w2):
    return jnp.square(jax.nn.relu(h @ w1)) @ w2


def setup_inputs(seed: int = 0) -> dict:
    key = jax.random.key(seed)
    keys = iter(jax.random.split(key, 128))

    def normal(shape, scale):
        return jax.random.normal(next(keys), shape, F32) * scale

    D = D_MODEL
    inp = {'x': normal((BATCH, SEQ, D), 1.0), 'c': normal((BATCH, D), 1.0)}
    for li in range(DEPTH):
        p = 'l%d_' % li
        inp[p + 'w_ada'] = normal((D, 6 * D), 0.5 * D ** -0.5)
        inp[p + 'b_ada'] = normal((6 * D,), 0.02)
        inp[p + 'norm1'] = 1.0 + normal((D,), 0.02)
        kind = li % N_MIXERS
        if kind == 0:
            inp[p + 'w_in'] = normal((D, NSA_IN_WIDTH), D ** -0.5)
            inp[p + 'cmp_pe'] = normal((2, CMP_BLOCK, HEAD_DIM), 0.1)
            inp[p + 'cmp_w1'] = normal((2, CMP_BLOCK * HEAD_DIM, CMP_HIDDEN), (CMP_BLOCK * HEAD_DIM) ** -0.5)
            inp[p + 'cmp_w2'] = normal((2, CMP_HIDDEN, HEAD_DIM), CMP_HIDDEN ** -0.5)
            inp[p + 'w_out'] = normal((NSA_HEADS * HEAD_DIM, D), (NSA_HEADS * HEAD_DIM) ** -0.5)
        elif kind == 1:
            inp[p + 'w_in'] = normal((D, DIL_IN_WIDTH), D ** -0.5)
            inp[p + 'w_out'] = normal((DIL_HEADS * HEAD_DIM, D), (DIL_HEADS * HEAD_DIM) ** -0.5)
        else:
            inp[p + 'w_in'] = normal((D, 2 * D_RNN), D ** -0.5)
            inp[p + 'conv_w'] = normal((CONV_WIDTH, D_RNN), CONV_WIDTH ** -0.5)
            inp[p + 'conv_b'] = normal((D_RNN,), 0.02)
            inp[p + 'w_gate'] = normal((2, RNN_BLOCKS, RNN_BLOCK_DIM, RNN_BLOCK_DIM), RNN_BLOCK_DIM ** -0.5)
            inp[p + 'b_gate'] = normal((2, D_RNN), 0.02)
            u = jax.random.uniform(next(keys), (D_RNN,), F32, minval=0.9, maxval=0.999)
            a0 = u ** (1.0 / LRU_C)
            inp[p + 'lambda'] = jnp.log(a0) - jnp.log1p(-a0)
            inp[p + 'w_out'] = normal((D_RNN, D), D_RNN ** -0.5)
        inp[p + 'norm2'] = 1.0 + normal((D,), 0.02)
        inp[p + 'w_ff1'] = normal((D, D_FF), D ** -0.5)
        inp[p + 'w_ff2'] = normal((D_FF, D), D_FF ** -0.5)
    inp['norm_f'] = 1.0 + normal((D,), 0.02)
    return inp


def reference(x, c,
              l0_w_ada, l0_b_ada, l0_norm1, l0_w_in, l0_cmp_pe, l0_cmp_w1, l0_cmp_w2, l0_w_out, l0_norm2, l0_w_ff1, l0_w_ff2,
              l1_w_ada, l1_b_ada, l1_norm1, l1_w_in, l1_w_out, l1_norm2, l1_w_ff1, l1_w_ff2,
              l2_w_ada, l2_b_ada, l2_norm1, l2_w_in, l2_conv_w, l2_conv_b, l2_w_gate, l2_b_gate, l2_lambda, l2_w_out, l2_norm2, l2_w_ff1, l2_w_ff2,
              l3_w_ada, l3_b_ada, l3_norm1, l3_w_in, l3_cmp_pe, l3_cmp_w1, l3_cmp_w2, l3_w_out, l3_norm2, l3_w_ff1, l3_w_ff2,
              norm_f):
    S = x.shape[1]
    cos, sin = rope_tables(S)
    layers = (
        (l0_w_ada, l0_b_ada, l0_norm1, l0_norm2, l0_w_ff1, l0_w_ff2, (l0_w_in, l0_cmp_pe, l0_cmp_w1, l0_cmp_w2, l0_w_out)),
        (l1_w_ada, l1_b_ada, l1_norm1, l1_norm2, l1_w_ff1, l1_w_ff2, (l1_w_in, l1_w_out)),
        (l2_w_ada, l2_b_ada, l2_norm1, l2_norm2, l2_w_ff1, l2_w_ff2,
         (l2_w_in, l2_conv_w, l2_conv_b, l2_w_gate, l2_b_gate, l2_lambda, l2_w_out)),
        (l3_w_ada, l3_b_ada, l3_norm1, l3_norm2, l3_w_ff1, l3_w_ff2, (l3_w_in, l3_cmp_pe, l3_cmp_w1, l3_cmp_w2, l3_w_out)),
    )
    cond = jax.nn.silu(c)
    for li in range(DEPTH):
        w_ada, b_ada, n1, n2, ff1, ff2, mix = layers[li]
        mod = (cond @ w_ada + b_ada)[:, None, :]
        sh1, sc1, g1, sh2, sc2, g2 = jnp.split(mod, 6, axis=-1)
        hn = modulate(x, n1, sh1, sc1)
        kind = li % N_MIXERS
        if kind == 0:
            y = nsa_mixer(hn, *mix, cos, sin)
        elif kind == 1:
            y = dilated_mixer(hn, *mix, cos, sin)
        else:
            y = rglru_mixer(hn, *mix)
        x = x + g1 * y
        x = x + g2 * sqrelu_mlp(modulate(x, n2, sh2, sc2), ff1, ff2)
    return rmsnorm(x, norm_f).astype(x.dtype)
```

```python
import functools
import math

import jax
import jax.numpy as jnp
import numpy as np
from jax import lax
from jax.experimental import pallas as pl
from jax.experimental.pallas import tpu as pltpu

F32 = jnp.float32
BF16 = jnp.bfloat16

D_MODEL = 2048
DEPTH = 4
N_MIXERS = 3
HEAD_DIM = 128
ROPE_THETA = 10000.0
NORM_EPS = 1e-6

NSA_HEADS = D_MODEL // HEAD_DIM
NSA_KV_GROUPS = 4
NSA_REP = NSA_HEADS // NSA_KV_GROUPS
NSA_KV_WIDTH = NSA_KV_GROUPS * HEAD_DIM
CMP_BLOCK = 32
CMP_STRIDE = 16
CMP_HIDDEN = 4 * HEAD_DIM
SEL_BLOCK = 64
SEL_TOPK = 16
NSA_WINDOW = 512
FORCED_BONUS = 1e9

DIL_HEADS = D_MODEL // HEAD_DIM
DIL_PATTERNS = ((128, 1), (512, 4), (2048, 16))

D_RNN = 2688
RNN_BLOCKS = 16
RNN_BLOCK_DIM = D_RNN // RNN_BLOCKS
CONV_WIDTH = 4
LRU_C = 8.0

LANE = 128
ATT_SCALE = 1.0 / math.sqrt(HEAD_DIM)
NEG_BIG = -1e30
SEL_OFF = -float(2 ** 30)
VMEM_LIMIT = 56 * 1024 * 1024

NT_DIMS = (((1,), (1,)), ((), ()))


def _cparams(sem):
    return pltpu.CompilerParams(dimension_semantics=sem, vmem_limit_bytes=VMEM_LIMIT)


def _adaln_kernel(c_ref, w_ref, b_ref, o_ref):
    w = w_ref[...]
    for b in range(c_ref.shape[0]):
        c = c_ref[b]
        cond = c * jax.nn.sigmoid(c)
        o_ref[b:b + 1, :] = jnp.sum(w * cond, axis=0, keepdims=True) + b_ref[...]


def adaln(c, w_ada, b_ada, tn=1024):
    B, D = c.shape
    N = w_ada.shape[1]
    return pl.pallas_call(
        _adaln_kernel,
        grid=(N // tn,),
        in_specs=[pl.BlockSpec((B, D, 1), lambda j: (0, 0, 0)),
                  pl.BlockSpec((D, tn), lambda j: (0, j)),
                  pl.BlockSpec((1, tn), lambda j: (0, j))],
        out_specs=pl.BlockSpec((B, tn), lambda j: (0, j)),
        out_shape=jax.ShapeDtypeStruct((B, N), F32),
        compiler_params=_cparams(("parallel",)),
        name="adaln",
    )(c.reshape(B, D, 1), w_ada, b_ada.reshape(1, N))


def _modulate_kernel(x_ref, gain_ref, sh_ref, sc_ref, o_ref):
    x = x_ref[...]
    ms = jnp.mean(x * x, axis=-1, keepdims=True)
    y = x * lax.rsqrt(ms + NORM_EPS) * gain_ref[...]
    o_ref[...] = (y * (1.0 + sc_ref[...]) + sh_ref[...]).astype(o_ref.dtype)


def modulate(x, gain, mod, shift_blk, scale_blk, ts=512):
    B, S, D = x.shape
    return pl.pallas_call(
        _modulate_kernel,
        grid=(B, S // ts),
        in_specs=[pl.BlockSpec((None, ts, D), lambda b, i: (b, i, 0)),
                  pl.BlockSpec((1, D), lambda b, i: (0, 0)),
                  pl.BlockSpec((None, 1, D), lambda b, i: (b, 0, shift_blk)),
                  pl.BlockSpec((None, 1, D), lambda b, i: (b, 0, scale_blk))],
        out_specs=pl.BlockSpec((None, ts, D), lambda b, i: (b, i, 0)),
        out_shape=jax.ShapeDtypeStruct((B, S, D), BF16),
        compiler_params=_cparams(("parallel", "parallel")),
        name="modulate",
    )(x, gain.reshape(1, D), mod, mod)


def _rmsnorm_kernel(x_ref, gain_ref, o_ref):
    x = x_ref[...]
    ms = jnp.mean(x * x, axis=-1, keepdims=True)
    o_ref[...] = x * lax.rsqrt(ms + NORM_EPS) * gain_ref[...]


def rmsnorm(x, gain, ts=512):
    B, S, D = x.shape
    return pl.pallas_call(
        _rmsnorm_kernel,
        grid=(B, S // ts),
        in_specs=[pl.BlockSpec((None, ts, D), lambda b, i: (b, i, 0)),
                  pl.BlockSpec((1, D), lambda b, i: (0, 0))],
        out_specs=pl.BlockSpec((None, ts, D), lambda b, i: (b, i, 0)),
        out_shape=jax.ShapeDtypeStruct((B, S, D), F32),
        compiler_params=_cparams(("parallel", "parallel")),
        name="rmsnorm",
    )(x, gain.reshape(1, D))


def _mm_kernel(*refs, nk, epilogue):
    if epilogue == "residual":
        a_ref, w_ref, res_ref, gate_ref, o_ref = refs[:5]
        rest = refs[5:]
    else:
        a_ref, w_ref, o_ref = refs[:3]
        rest = refs[3:]

    def finish(acc):
        if epilogue == "relu2":
            r = jnp.maximum(acc, 0.0)
            acc = r * r
        elif epilogue == "residual":
            acc = res_ref[...] + gate_ref[...] * acc
        o_ref[...] = acc.astype(o_ref.dtype)

    part = jnp.dot(a_ref[...], w_ref[...], preferred_element_type=F32)
    if nk == 1:
        finish(part)
        return
    acc_ref = rest[0]
    k = pl.program_id(2)

    @pl.when(k == 0)
    def _():
        acc_ref[...] = part

    @pl.when(jnp.logical_and(k > 0, k < nk - 1))
    def _():
        acc_ref[...] += part

    @pl.when(k == nk - 1)
    def _():
        finish(acc_ref[...] + part)


def matmul(a, w, *, tm, tn, tk, out_dtype=F32, epilogue="none", res=None, mod=None,
           gate_blk=0, rows_per_batch=None):
    M, K = a.shape
    N = w.shape[1]
    nk = K // tk
    assert M % tm == 0 and N % tn == 0 and K % tk == 0
    in_specs = [pl.BlockSpec((tm, tk), lambda i, j, k: (i, k)),
                pl.BlockSpec((tk, tn), lambda i, j, k: (k, j))]
    args = [a, w]
    if epilogue == "residual":
        assert rows_per_batch % tm == 0
        gpb = D_MODEL // tn
        in_specs += [pl.BlockSpec((tm, tn), lambda i, j, k: (i, j)),
                     pl.BlockSpec((None, 1, tn),
                                  lambda i, j, k: (i * tm // rows_per_batch, 0, gate_blk * gpb + j))]
        args += [res, mod]
    scratch = [pltpu.VMEM((tm, tn), F32)] if nk > 1 else []
    return pl.pallas_call(
        functools.partial(_mm_kernel, nk=nk, epilogue=epilogue),
        grid=(M // tm, N // tn, nk),
        in_specs=in_specs,
        out_specs=pl.BlockSpec((tm, tn), lambda i, j, k: (i, j)),
        out_shape=jax.ShapeDtypeStruct((M, N), out_dtype),
        scratch_shapes=scratch,
        compiler_params=_cparams(("parallel", "parallel", "arbitrary")),
        name="mm_" + epilogue,
    )(*args)


def rope_tables(S):
    inv_freq = ROPE_THETA ** (-jnp.arange(0, HEAD_DIM, 2, dtype=F32) / HEAD_DIM)
    ang = jnp.arange(S, dtype=F32)[:, None] * inv_freq[None, :]
    cos, sin = jnp.cos(ang), jnp.sin(ang)
    return jnp.concatenate([cos, cos], axis=-1), jnp.concatenate([-sin, sin], axis=-1)


def _rope_cast_kernel(flags_ref, x_ref, c_ref, s_ref, o_ref, slab_ref, *, dil):
    j = pl.program_id(2)
    nslab = x_ref.shape[-1] // LANE
    rows = x_ref.shape[0] // dil

    def residue(r):
        return pl.ds(r, rows, stride=dil) if dil > 1 else slice(None)

    def slabs():
        for h in range(nslab):
            sl = slice(h * LANE, (h + 1) * LANE)
            if dil > 1:
                slab_ref[...] = x_ref[:, sl]
                yield sl, slab_ref
            else:
                yield sl, x_ref.at[:, sl]

    @pl.when(flags_ref[j] == 0)
    def _():
        for sl, src in slabs():
            for r in range(dil):
                o_ref[r, :, sl] = src[residue(r), :].astype(o_ref.dtype)

    @pl.when(flags_ref[j] != 0)
    def _():
        for sl, src in slabs():
            for r in range(dil):
                t = src[residue(r), :]
                o_ref[r, :, sl] = (t * c_ref[residue(r), :] + pltpu.roll(t, HEAD_DIM // 2, 1)
                                   * s_ref[residue(r), :]).astype(o_ref.dtype)


def rope_cast(x, cos2, sin2, *, width, first_blk, flags, dil=1, ts=512):
    B, S, _ = x.shape
    nblk = len(flags)
    grid_spec = pltpu.PrefetchScalarGridSpec(
        num_scalar_prefetch=1,
        grid=(B, S // ts, nblk),
        in_specs=[pl.BlockSpec((None, ts, width), lambda b, i, j, f: (b, i, first_blk + j)),
                  pl.BlockSpec((ts, LANE), lambda b, i, j, f: (i, 0)),
                  pl.BlockSpec((ts, LANE), lambda b, i, j, f: (i, 0))],
        out_specs=pl.BlockSpec((None, dil, ts // dil, width), lambda b, i, j, f: (b, 0, i, j)),
        scratch_shapes=[pltpu.VMEM((ts, LANE), F32)],
    )
    return pl.pallas_call(
        functools.partial(_rope_cast_kernel, dil=dil),
        grid_spec=grid_spec,
        out_shape=jax.ShapeDtypeStruct((B, dil, S // dil, nblk * width), BF16),
        compiler_params=_cparams(("parallel", "parallel", "arbitrary")),
        name="rope_cast",
    )(jnp.asarray(flags, jnp.int32), x, cos2, sin2)


def _banded_kernel(q_ref, k_ref, v_ref, o_ref, *lse_refs, R, nwin, window, tq):
    i = pl.program_id(3)
    nkeys = (nwin + 1) * LANE
    rows = R * LANE
    rel = (lax.broadcasted_iota(jnp.int32, (rows, nkeys), 0) & (LANE - 1)) \
        - lax.broadcasted_iota(jnp.int32, (rows, nkeys), 1)
    for j in range(tq // LANE):
        q0 = i * tq + j * LANE
        kstart = pl.multiple_of(jnp.maximum(q0 - nwin * LANE, 0), LANE)
        qj = q_ref[j * LANE:(j + 1) * LANE, :]
        if R > 1:
            q = jnp.concatenate([qj[:, r * LANE:(r + 1) * LANE] for r in range(R)], axis=0)
        else:
            q = qj
        k = k_ref[pl.ds(kstart, nkeys), :]
        v = v_ref[pl.ds(kstart, nkeys), :]
        s = lax.dot_general(q, k, NT_DIMS, preferred_element_type=F32) * ATT_SCALE
        diff = rel + (q0 - kstart)
        valid = lax.bitcast_convert_type(diff, jnp.uint32) <= jnp.uint32(window)
        s = jnp.where(valid, s, NEG_BIG)
        m = jnp.max(s, axis=-1, keepdims=True)
        p = jnp.exp(s - m)
        l = jnp.sum(p, axis=-1, keepdims=True)
        o = jnp.dot(p.astype(BF16), v, preferred_element_type=F32) / l
        for r in range(R):
            o_ref[j * LANE:(j + 1) * LANE, r * LANE:(r + 1) * LANE] = o[r * LANE:(r + 1) * LANE]
        if lse_refs:
            lse_refs[0][j * LANE:(j + 1) * LANE, :] = jnp.broadcast_to(m + jnp.log(l), (LANE, LANE))


def banded_attention(q_arr, kv_arr, *, n_kv, R, q_blk, k_blk, v_blk, nwin, window,
                     want_lse):
    B, dil, L, _ = q_arr.shape
    tq = min(512, L)
    assert L % tq == 0 and L >= (nwin + 1) * LANE and q_blk % R == 0
    width = n_kv * R * LANE
    out_shape = [jax.ShapeDtypeStruct((B, dil, L, width), F32)]
    out_specs = [pl.BlockSpec((None, None, tq, R * LANE), lambda b, r, h, i: (b, r, i, h))]
    if want_lse:
        out_shape.append(out_shape[0])
        out_specs.append(out_specs[0])
    return pl.pallas_call(
        functools.partial(_banded_kernel, R=R, nwin=nwin, window=window, tq=tq),
        grid=(B, dil, n_kv, L // tq),
        in_specs=[pl.BlockSpec((None, None, tq, R * LANE),
                               lambda b, r, h, i: (b, r, i, q_blk // R + h)),
                  pl.BlockSpec((None, None, L, LANE), lambda b, r, h, i: (b, r, 0, k_blk + h)),
                  pl.BlockSpec((None, None, L, LANE), lambda b, r, h, i: (b, r, 0, v_blk + h))],
        out_specs=out_specs,
        out_shape=out_shape,
        compiler_params=_cparams(("parallel", "parallel", "parallel", "arbitrary")),
        name="banded_attn",
    )(q_arr, kv_arr, kv_arr)


def _compress_kernel(p_ref, pe_ref, w1_ref, w2_ref, o_ref):
    half = CMP_STRIDE * HEAD_DIM
    pieces = p_ref[...]
    top = (pieces + pe_ref[0:1, :]).astype(BF16)
    bot = (pieces + pe_ref[1:2, :]).astype(BF16)
    a = jnp.dot(top, w1_ref[0:half, :], preferred_element_type=F32)
    b = jnp.dot(bot, w1_ref[half:2 * half, :], preferred_element_type=F32)
    n = a.shape[0]
    hid = a + pltpu.roll(b, n - 1, 0)
    o_ref[...] = jnp.dot(jax.nn.gelu(hid).astype(BF16), w2_ref[...], preferred_element_type=F32)


def compress(pieces, pe, w1, w2):
    _, B, G, NP, W = pieces.shape
    return pl.pallas_call(
        _compress_kernel,
        grid=(2, B, G),
        in_specs=[pl.BlockSpec((None, None, None, NP, W), lambda t, b, g: (t, b, g, 0, 0)),
                  pl.BlockSpec((None, 2, W), lambda t, b, g: (t, 0, 0)),
                  pl.BlockSpec((None, 2 * W, CMP_HIDDEN), lambda t, b, g: (t, 0, 0)),
                  pl.BlockSpec((None, CMP_HIDDEN, HEAD_DIM), lambda t, b, g: (t, 0, 0))],
        out_specs=pl.BlockSpec((None, None, None, NP, HEAD_DIM), lambda t, b, g: (t, b, g, 0, 0)),
        out_shape=jax.ShapeDtypeStruct((2, B, G, NP, HEAD_DIM), F32),
        compiler_params=_cparams(("parallel", "parallel", "parallel")),
        name="nsa_compress",
    )(pieces, pe, w1, w2)


def _cmp_attn_kernel(q_ref, kc_ref, vc_ref, ov_ref, o_ref, imp_ref, *, tq):
    i = pl.program_id(2)
    R = NSA_REP
    qj = q_ref[...]
    q = jnp.concatenate([qj[:, r * LANE:(r + 1) * LANE] for r in range(R)], axis=0).astype(BF16)
    kc = kc_ref[...].astype(BF16)
    ncmp = kc.shape[0]
    s = lax.dot_general(q, kc, NT_DIMS, preferred_element_type=F32) * ATT_SCALE
    t = i * tq + (lax.broadcasted_iota(jnp.int32, s.shape, 0) & (tq - 1))
    cmp_end = lax.broadcasted_iota(jnp.int32, s.shape, 1) * CMP_STRIDE + (CMP_BLOCK - 1)
    valid = cmp_end <= t
    s = jnp.where(valid, s, NEG_BIG)
    m = jnp.max(s, axis=-1, keepdims=True)
    p = jnp.where(valid, jnp.exp(s - m), 0.0)
    l = jnp.sum(p, axis=-1, keepdims=True)
    p = p / jnp.where(l > 0, l, 1.0)
    o = jnp.dot(p.astype(BF16), vc_ref[...].astype(BF16), preferred_element_type=F32)
    for r in range(R):
        o_ref[:, r * LANE:(r + 1) * LANE] = o[r * tq:(r + 1) * tq]
    psum = p[0:tq]
    for r in range(1, R):
        psum = psum + p[r * tq:(r + 1) * tq]
    p_hi = psum.astype(BF16)
    p_lo = (psum - p_hi.astype(F32)).astype(BF16)
    ov = ov_ref[...]
    imp_ref[...] = (lax.dot_general(ov, p_hi, NT_DIMS, preferred_element_type=F32)
                    + lax.dot_general(ov, p_lo, NT_DIMS, preferred_element_type=F32))


def cmp_attention(proj, kvc, overlap_t, tq=128):
    B, S, _ = proj.shape
    G = NSA_KV_GROUPS
    NP = kvc.shape[3]
    n_sel = overlap_t.shape[0]
    qw = NSA_REP * LANE
    return pl.pallas_call(
        functools.partial(_cmp_attn_kernel, tq=tq),
        grid=(B, G, S // tq),
        in_specs=[pl.BlockSpec((None, tq, qw), lambda b, g, i: (b, i, g)),
                  pl.BlockSpec((None, None, None, NP, LANE), lambda b, g, i: (0, b, g, 0, 0)),
                  pl.BlockSpec((None, None, None, NP, LANE), lambda b, g, i: (1, b, g, 0, 0)),
                  pl.BlockSpec((n_sel, NP), lambda b, g, i: (0, 0))],
        out_specs=[pl.BlockSpec((None, tq, qw), lambda b, g, i: (b, i, g)),
                   pl.BlockSpec((None, None, n_sel, tq), lambda b, g, i: (b, g, 0, i))],
        out_shape=[jax.ShapeDtypeStruct((B, S, NSA_HEADS * LANE), F32),
                   jax.ShapeDtypeStruct((B, G, n_sel, S), F32)],
        compiler_params=_cparams(("parallel", "parallel", "parallel")),
        name="nsa_cmp_attn",
    )(proj, kvc, kvc, overlap_t)


def _topk_kernel(imp_ref, o_ref, *, tq):
    i = pl.program_id(2)
    imp = imp_ref[...]
    n_sel = imp.shape[0]
    blk = lax.broadcasted_iota(jnp.int32, imp.shape, 0)
    t = i * tq + lax.broadcasted_iota(jnp.int32, imp.shape, 1)
    cur = t // SEL_BLOCK
    avail = blk <= cur
    forced = jnp.where(blk == 0, 1.0, jnp.where(blk == cur, 1.0, jnp.where(blk == cur - 1, 1.0, 0.0)))
    score = jnp.where(avail, imp + FORCED_BONUS * forced, -jnp.inf)
    picked = jnp.zeros(imp.shape, F32)
    for _ in range(min(SEL_TOPK, n_sel)):
        mx = jnp.max(score, axis=0, keepdims=True)
        first = jnp.min(jnp.where(score == mx, blk, n_sel), axis=0, keepdims=True)
        hit = blk == first
        picked = jnp.where(hit, 1.0, picked)
        score = jnp.where(hit, -jnp.inf, score)
    feat = jnp.where(avail, jnp.where(picked > 0.0, 0.0, SEL_OFF), SEL_OFF)
    o_ref[...] = feat.T.astype(o_ref.dtype)


def topk_select(imp_t, tq=256):
    B, G, n_sel, S = imp_t.shape
    return pl.pallas_call(
        functools.partial(_topk_kernel, tq=tq),
        grid=(B, G, S // tq),
        in_specs=[pl.BlockSpec((None, None, n_sel, tq), lambda b, g, i: (b, g, 0, i))],
        out_specs=pl.BlockSpec((None, None, tq, n_sel), lambda b, g, i: (b, g, i, 0)),
        out_shape=jax.ShapeDtypeStruct((B, G, S, n_sel), BF16),
        compiler_params=_cparams(("parallel", "parallel", "parallel")),
        name="nsa_topk",
    )(imp_t)


def _sel_attn_kernel(q_ref, mf_ref, k_ref, e_ref, v_ref, o_ref, *, tq, tkv):
    i = pl.program_id(2)
    R = NSA_REP
    qj = q_ref[...]
    mf = mf_ref[...]
    qa = jnp.concatenate(
        [jnp.concatenate([qj[:, r * LANE:(r + 1) * LANE], mf], axis=1) for r in range(R)], axis=0)
    rows = R * tq
    q0 = i * tq
    last = (q0 + tq - 1) // tkv
    rel = (lax.broadcasted_iota(jnp.int32, (rows, tkv), 0) & (tq - 1)) \
        - lax.broadcasted_iota(jnp.int32, (rows, tkv), 1)

    def step(j, carry, masked):
        m, l, acc = carry
        ks = pl.multiple_of(j * tkv, tkv)
        ka = jnp.concatenate([k_ref[pl.ds(ks, tkv), :], e_ref[pl.ds(ks, tkv), :]], axis=1)
        s = lax.dot_general(qa, ka, NT_DIMS, preferred_element_type=F32) * ATT_SCALE
        if masked:
            s = jnp.where(rel + (q0 - ks) >= 0, s, NEG_BIG)
        m_new = jnp.maximum(m, jnp.max(s, axis=-1, keepdims=True))
        alpha = jnp.exp(m - m_new)
        p = jnp.exp(s - m_new)
        l = alpha * l + jnp.sum(p, axis=-1, keepdims=True)
        acc = alpha * acc + jnp.dot(p.astype(BF16), v_ref[pl.ds(ks, tkv), :],
                                    preferred_element_type=F32)
        return m_new, l, acc

    carry = (jnp.full((rows, 1), NEG_BIG, F32), jnp.zeros((rows, 1), F32),
             jnp.zeros((rows, LANE), F32))
    carry = lax.fori_loop(0, last, lambda j, c: step(j, c, False), carry)
    _, l, acc = step(last, carry, True)
    o = acc / l
    for r in range(R):
        o_ref[:, r * LANE:(r + 1) * LANE] = o[r * tq:(r + 1) * tq]


def sel_attention(q_rot, mfeat, kvb, onehot, *, k_blk, v_blk, tq=128, tkv=512):
    B, S, _ = q_rot.shape
    G = NSA_KV_GROUPS
    n_sel = onehot.shape[1]
    qw = NSA_REP * LANE
    tkv = min(tkv, S)
    return pl.pallas_call(
        functools.partial(_sel_attn_kernel, tq=tq, tkv=tkv),
        grid=(B, G, S // tq),
        in_specs=[pl.BlockSpec((None, tq, qw), lambda b, g, i: (b, i, g)),
                  pl.BlockSpec((None, None, tq, n_sel), lambda b, g, i: (b, g, i, 0)),
                  pl.BlockSpec((None, S, LANE), lambda b, g, i: (b, 0, k_blk + g)),
                  pl.BlockSpec((S, n_sel), lambda b, g, i: (0, 0)),
                  pl.BlockSpec((None, S, LANE), lambda b, g, i: (b, 0, v_blk + g))],
        out_specs=pl.BlockSpec((None, tq, qw), lambda b, g, i: (b, i, g)),
        out_shape=jax.ShapeDtypeStruct((B, S, NSA_HEADS * LANE), F32),
        compiler_params=_cparams(("parallel", "parallel", "arbitrary")),
        name="nsa_sel_attn",
    )(q_rot, mfeat, kvb, onehot, kvb)


def _nsa_combine_kernel(g_ref, oc_ref, os_ref, ow_ref, o_ref):
    gates = jax.nn.sigmoid(g_ref[...])
    for h in range(NSA_HEADS):
        sl = slice(h * LANE, (h + 1) * LANE)
        acc = gates[:, 3 * h:3 * h + 1] * oc_ref[:, sl]
        acc = acc + gates[:, 3 * h + 1:3 * h + 2] * os_ref[:, sl]
        acc = acc + gates[:, 3 * h + 2:3 * h + 3] * ow_ref[:, sl]
        o_ref[:, sl] = acc.astype(o_ref.dtype)


def nsa_combine(gate_logits, o_cmp, o_sel, o_win, ts=256):
    B, S, W = o_cmp.shape
    spec = pl.BlockSpec((None, ts, W), lambda b, i: (b, i, 0))
    return pl.pallas_call(
        _nsa_combine_kernel,
        grid=(B, S // ts),
        in_specs=[pl.BlockSpec((None, ts, LANE), lambda b, i: (b, i, 0)), spec, spec, spec],
        out_specs=spec,
        out_shape=jax.ShapeDtypeStruct((B, S, W), BF16),
        compiler_params=_cparams(("parallel", "parallel")),
        name="nsa_combine",
    )(gate_logits, o_cmp, o_sel, o_win)


def _dil_combine_kernel(*refs, dils):
    n = len(dils)
    o_refs, lse_refs, out_ref = refs[:n], refs[n:2 * n], refs[2 * n]
    nat_o, nat_l = refs[2 * n + 1:3 * n + 1], refs[3 * n + 1:]
    ts = out_ref.shape[0]
    for h in range(out_ref.shape[1] // LANE):
        sl = slice(h * LANE, (h + 1) * LANE)
        for g, dil in enumerate(dils):
            for r in range(dil):
                rows = pl.ds(r, ts // dil, stride=dil) if dil > 1 else slice(None)
                nat_o[g][rows, :] = o_refs[g][r, :, sl]
                nat_l[g][rows, :] = lse_refs[g][r, :, sl]
        lses = [r[...] for r in nat_l]
        mx = functools.reduce(jnp.maximum, lses)
        ws = [jnp.exp(l - mx) for l in lses]
        den = functools.reduce(lambda a, b: a + b, ws)
        num = functools.reduce(lambda a, b: a + b, [w * r[...] for w, r in zip(ws, nat_o)])
        out_ref[:, sl] = (num / den).astype(out_ref.dtype)


def dil_combine(outs, lses, ts=128):
    dils = tuple(o.shape[1] for o in outs)
    B, W = outs[0].shape[0], outs[0].shape[-1]
    S = outs[0].shape[1] * outs[0].shape[2]
    specs = [pl.BlockSpec((None, d, ts // d, W), lambda b, i: (b, 0, i, 0)) for d in dils]
    return pl.pallas_call(
        functools.partial(_dil_combine_kernel, dils=dils),
        grid=(B, S // ts),
        in_specs=specs + specs,
        out_specs=pl.BlockSpec((None, ts, W), lambda b, i: (b, i, 0)),
        out_shape=jax.ShapeDtypeStruct((B, S, W), BF16),
        scratch_shapes=[pltpu.VMEM((ts, LANE), F32)] * (2 * len(dils)),
        compiler_params=_cparams(("parallel", "parallel")),
        name="dil_combine",
    )(*outs, *lses)


def _softplus(x):
    return jnp.maximum(x, 0.0) + jnp.log1p(jnp.exp(-jnp.abs(x)))


def _gate_band_starts():
    ntile = D_RNN // LANE
    starts = []
    for j in range(ntile):
        n_lo = (j * LANE) // RNN_BLOCK_DIM
        n_hi = (j * LANE + LANE - 1) // RNN_BLOCK_DIM
        lo = (n_lo * RNN_BLOCK_DIM) // LANE
        hi = -(-((n_hi + 1) * RNN_BLOCK_DIM) // LANE)
        assert hi - lo <= 4
        starts.append(min(lo, ntile - 4))
    return starts


def _rglru_kernel(y_ref, xr_ref, cw_ref, cb_ref, wg_ref, bg_ref, lam_ref, o_ref,
                  h_ref, tail_ref, *, ts, starts):
    i = pl.program_id(1)

    @pl.when(i == 0)
    def _():
        h_ref[...] = jnp.zeros_like(h_ref)
        tail_ref[...] = jnp.zeros_like(tail_ref)

    xr = xr_ref[...]
    ext = jnp.concatenate([tail_ref[...], xr], axis=0)
    x = cb_ref[...] + xr * cw_ref[CONV_WIDTH - 1:CONV_WIDTH, :]
    for d in range(1, CONV_WIDTH):
        shifted = pltpu.roll(ext, d, 0)[8:8 + ts]
        x = x + shifted * cw_ref[CONV_WIDTH - 1 - d:CONV_WIDTH - d, :]
    tail_ref[...] = xr[ts - 8:ts]

    xb = x.astype(BF16)
    gl = []
    for g in range(2):
        tiles = [jnp.dot(xb[:, a * LANE:(a + 4) * LANE], wg_ref[g, j],
                         preferred_element_type=F32) for j, a in enumerate(starts)]
        gl.append(jnp.concatenate(tiles, axis=1) + bg_ref[g:g + 1, :])
    r = jax.nn.sigmoid(gl[0])
    ig = jax.nn.sigmoid(gl[1])
    log_a = (-LRU_C) * r * _softplus(-lam_ref[...])
    a = jnp.exp(log_a)
    b = jnp.sqrt(-jnp.tanh(log_a) * (a * a + 1.0)) * (ig * x)

    row = lax.broadcasted_iota(jnp.int32, a.shape, 0)
    d = 1
    while d < ts:
        keep = row >= d
        b = b + a * jnp.where(keep, pltpu.roll(b, d, 0), 0.0)
        a = a * jnp.where(keep, pltpu.roll(a, d, 0), 1.0)
        d *= 2
    h = a * h_ref[0:1, :] + b
    h_ref[0:1, :] = h[ts - 1:ts, :]
    o_ref[...] = (h * jax.nn.gelu(y_ref[...])).astype(o_ref.dtype)


def rglru_scan(proj, conv_w, conv_b, wband, b_gate, lam, ts=128):
    B, S, _ = proj.shape
    C = D_RNN
    starts = _gate_band_starts()
    vec = lambda n: pl.BlockSpec((n, C), lambda b, i: (0, 0))
    return pl.pallas_call(
        functools.partial(_rglru_kernel, ts=ts, starts=starts),
        grid=(B, S // ts),
        in_specs=[pl.BlockSpec((None, ts, C), lambda b, i: (b, i, 0)),
                  pl.BlockSpec((None, ts, C), lambda b, i: (b, i, 1)),
                  vec(CONV_WIDTH), vec(1),
                  pl.BlockSpec(wband.shape, lambda b, i: (0, 0, 0, 0)),
                  vec(2), vec(1)],
        out_specs=pl.BlockSpec((None, ts, C), lambda b, i: (b, i, 0)),
        out_shape=jax.ShapeDtypeStruct((B, S, C), BF16),
        scratch_shapes=[pltpu.VMEM((8, C), F32), pltpu.VMEM((8, C), F32)],
        compiler_params=_cparams(("arbitrary", "arbitrary")),
        name="rglru_scan",
    )(proj, proj, conv_w, conv_b.reshape(1, C), wband, b_gate, lam.reshape(1, C))


def _gate_band_weights(w_gate):
    starts = _gate_band_starts()
    dense = jnp.stack([jax.scipy.linalg.block_diag(*[w_gate[g, n] for n in range(RNN_BLOCKS)])
                       for g in range(2)])
    tiles = [dense[:, a * LANE:(a + 4) * LANE, j * LANE:(j + 1) * LANE]
             for j, a in enumerate(starts)]
    return jnp.stack(tiles, axis=1).astype(BF16)


def _row_tile(T):
    return 512 if T % 512 == 0 else T


def nsa_mixer(hn, x, mod, gate_blk, w_in, cmp_pe, cmp_w1, cmp_w2, w_out, cos2, sin2):
    B, S, D = x.shape
    T = B * S
    G = NSA_KV_GROUPS
    hq = NSA_HEADS * HEAD_DIM
    main = hq + 6 * NSA_KV_WIDTH
    tm = _row_tile(T)
    hn2 = hn.reshape(T, D)
    proj = matmul(hn2, w_in[:, :main].astype(BF16), tm=tm, tn=main // 2, tk=D).reshape(B, S, main)
    w_gate = jnp.pad(w_in[:, main:], ((0, 0), (0, LANE - 3 * NSA_HEADS))).astype(BF16)
    gate_logits = matmul(hn2, w_gate, tm=tm, tn=LANE, tk=D).reshape(B, S, LANE)

    q_rot = rope_cast(proj, cos2, sin2, width=hq, first_blk=0, flags=[1])
    kvb = rope_cast(proj, cos2, sin2, width=NSA_KV_WIDTH, first_blk=hq // NSA_KV_WIDTH + 2,
                    flags=[1, 0, 1, 0])

    npiece = S // CMP_STRIDE
    kv_cmp = proj[:, :, hq:hq + 2 * NSA_KV_WIDTH].reshape(B, npiece, CMP_STRIDE, 2, G, HEAD_DIM)
    pieces = kv_cmp.transpose(3, 0, 4, 1, 2, 5).reshape(2, B, G, npiece, CMP_STRIDE * HEAD_DIM)
    kvc = compress(pieces, cmp_pe.reshape(2, 2, CMP_STRIDE * HEAD_DIM),
                   cmp_w1.astype(BF16), cmp_w2.astype(BF16))

    n_sb = S // SEL_BLOCK
    cmp_start = np.arange(npiece) * CMP_STRIDE
    sel_start = np.arange(n_sb) * SEL_BLOCK
    overlap = ((cmp_start[:, None] < sel_start[None, :] + SEL_BLOCK)
               & (cmp_start[:, None] + CMP_BLOCK > sel_start[None, :]))
    overlap[npiece - 1] = False
    overlap_t = jnp.asarray(overlap.T, BF16)
    onehot = jnp.asarray(np.arange(S)[:, None] // SEL_BLOCK == np.arange(n_sb)[None, :], BF16)

    o_cmp, imp_t = cmp_attention(proj, kvc, overlap_t)
    mfeat = topk_select(imp_t)
    kb = NSA_KV_WIDTH // LANE
    o_sel = sel_attention(q_rot.reshape(B, S, hq), mfeat, kvb.reshape(B, S, 4 * NSA_KV_WIDTH),
                          onehot, k_blk=0, v_blk=kb)
    (o_win,) = banded_attention(q_rot, kvb, n_kv=G, R=NSA_REP, q_blk=0, k_blk=2 * kb,
                                v_blk=3 * kb, nwin=(NSA_WINDOW - 1 + LANE - 1) // LANE,
                                window=NSA_WINDOW - 1, want_lse=False)
    o = nsa_combine(gate_logits, o_cmp, o_sel, o_win.reshape(B, S, hq))
    return matmul(o.reshape(T, hq), w_out.astype(BF16), tm=tm, tn=D, tk=hq // 2,
                  epilogue="residual", res=x.reshape(T, D), mod=mod, gate_blk=gate_blk,
                  rows_per_batch=S).reshape(B, S, D)


def dilated_mixer(hn, x, mod, gate_blk, w_in, w_out, cos2, sin2):
    B, S, D = x.shape
    T = B * S
    H = DIL_HEADS
    hw = H * HEAD_DIM
    tm = _row_tile(T)
    proj = matmul(hn.reshape(T, D), w_in.astype(BF16), tm=tm, tn=hw, tk=D)
    proj = proj.reshape(B, S, w_in.shape[1])
    outs, lses = [], []
    for g, (window, dil) in enumerate(DIL_PATTERNS):
        w = window // dil
        qkv = rope_cast(proj, cos2, sin2, width=hw, first_blk=3 * g, flags=[1, 1, 0], dil=dil)
        o, lse = banded_attention(qkv, qkv, n_kv=H, R=1, q_blk=0, k_blk=H, v_blk=2 * H,
                                  nwin=(w + LANE - 1) // LANE, window=w, want_lse=True)
        outs.append(o)
        lses.append(lse)
    o = dil_combine(outs, lses)
    return matmul(o.reshape(T, hw), w_out.astype(BF16), tm=tm, tn=D, tk=hw // 2,
                  epilogue="residual", res=x.reshape(T, D), mod=mod, gate_blk=gate_blk,
                  rows_per_batch=S).reshape(B, S, D)


def rglru_mixer(hn, x, mod, gate_blk, w_in, conv_w, conv_b, w_gate, b_gate, lam, w_out):
    B, S, D = x.shape
    T = B * S
    tm = _row_tile(T)
    proj = matmul(hn.reshape(T, D), w_in.astype(BF16), tm=tm, tn=D_RNN, tk=D)
    hy = rglru_scan(proj.reshape(B, S, 2 * D_RNN), conv_w, conv_b, _gate_band_weights(w_gate),
                    b_gate, lam)
    return matmul(hy.reshape(T, D_RNN), w_out.astype(BF16), tm=tm, tn=D, tk=D_RNN // 3,
                  epilogue="residual", res=x.reshape(T, D), mod=mod, gate_blk=gate_blk,
                  rows_per_batch=S).reshape(B, S, D)


def mlp(hn, x, mod, gate_blk, w1, w2):
    B, S, D = x.shape
    T = B * S
    tm = _row_tile(T)
    dff = w1.shape[1]
    h = matmul(hn.reshape(T, D), w1.astype(BF16), tm=tm, tn=1024, tk=D, out_dtype=BF16,
               epilogue="relu2")
    return matmul(h, w2.astype(BF16), tm=tm, tn=D, tk=1024, epilogue="residual",
                  res=x.reshape(T, D), mod=mod, gate_blk=gate_blk,
                  rows_per_batch=S).reshape(B, S, D)


def kernel(x, c, l0_w_ada, l0_b_ada, l0_norm1, l0_w_in, l0_cmp_pe, l0_cmp_w1, l0_cmp_w2, l0_w_out, l0_norm2, l0_w_ff1, l0_w_ff2, l1_w_ada, l1_b_ada, l1_norm1, l1_w_in, l1_w_out, l1_norm2, l1_w_ff1, l1_w_ff2, l2_w_ada, l2_b_ada, l2_norm1, l2_w_in, l2_conv_w, l2_conv_b, l2_w_gate, l2_b_gate, l2_lambda, l2_w_out, l2_norm2, l2_w_ff1, l2_w_ff2, l3_w_ada, l3_b_ada, l3_norm1, l3_w_in, l3_cmp_pe, l3_cmp_w1, l3_cmp_w2, l3_w_out, l3_norm2, l3_w_ff1, l3_w_ff2, norm_f):
    B, S, D = x.shape
    cos2, sin2 = rope_tables(S)
    layers = (
        (l0_w_ada, l0_b_ada, l0_norm1, l0_norm2, l0_w_ff1, l0_w_ff2,
         (l0_w_in, l0_cmp_pe, l0_cmp_w1, l0_cmp_w2, l0_w_out)),
        (l1_w_ada, l1_b_ada, l1_norm1, l1_norm2, l1_w_ff1, l1_w_ff2, (l1_w_in, l1_w_out)),
        (l2_w_ada, l2_b_ada, l2_norm1, l2_norm2, l2_w_ff1, l2_w_ff2,
         (l2_w_in, l2_conv_w, l2_conv_b, l2_w_gate, l2_b_gate, l2_lambda, l2_w_out)),
        (l3_w_ada, l3_b_ada, l3_norm1, l3_norm2, l3_w_ff1, l3_w_ff2,
         (l3_w_in, l3_cmp_pe, l3_cmp_w1, l3_cmp_w2, l3_w_out)),
    )
    for li in range(DEPTH):
        w_ada, b_ada, n1, n2, ff1, ff2, mix = layers[li]
        mod = adaln(c, w_ada, b_ada).reshape(B, 1, 6 * D)
        hn = modulate(x, n1, mod, 0, 1)
        kind = li % N_MIXERS
        if kind == 0:
            x = nsa_mixer(hn, x, mod, 2, *mix, cos2, sin2)
        elif kind == 1:
            x = dilated_mixer(hn, x, mod, 2, *mix, cos2, sin2)
        else:
            x = rglru_mixer(hn, x, mod, 2, *mix)
        hn = modulate(x, n2, mod, 3, 4)
        x = mlp(hn, x, mod, 5, ff1, ff2)
    return rmsnorm(x, norm_f)
```

```python
import functools
import math

import jax
import jax.numpy as jnp
import numpy as np
from jax import lax
from jax.experimental import pallas as pl
from jax.experimental.pallas import tpu as pltpu

F32 = jnp.float32
BF16 = jnp.bfloat16

D_MODEL = 2048
DEPTH = 4
N_MIXERS = 3
HEAD_DIM = 128
ROPE_THETA = 10000.0
NORM_EPS = 1e-6

NSA_HEADS = D_MODEL // HEAD_DIM
NSA_KV_GROUPS = 4
NSA_REP = NSA_HEADS // NSA_KV_GROUPS
NSA_KV_WIDTH = NSA_KV_GROUPS * HEAD_DIM
CMP_BLOCK = 32
CMP_STRIDE = 16
CMP_HIDDEN = 4 * HEAD_DIM
SEL_BLOCK = 64
SEL_TOPK = 16
NSA_WINDOW = 512
FORCED_BONUS = 1e9

DIL_HEADS = D_MODEL // HEAD_DIM
DIL_PATTERNS = ((128, 1), (512, 4), (2048, 16))

D_RNN = 2688
RNN_BLOCKS = 16
RNN_BLOCK_DIM = D_RNN // RNN_BLOCKS
CONV_WIDTH = 4
LRU_C = 8.0

LANE = 128
LOG2E = math.log2(math.e)
QK_SCALE = LOG2E / math.sqrt(HEAD_DIM)
NEG_BIG = -1e30
SEL_OFF = -float(2 ** 30)
VMEM_LIMIT = 56 * 1024 * 1024

NT_DIMS = (((1,), (1,)), ((), ()))


def _cparams(sem):
    return pltpu.CompilerParams(dimension_semantics=sem, vmem_limit_bytes=VMEM_LIMIT)


def _adaln_kernel(c_ref, w_ref, b_ref, o_ref):
    w = w_ref[...]
    for b in range(c_ref.shape[0]):
        c = c_ref[b]
        cond = c * jax.nn.sigmoid(c)
        o_ref[b:b + 1, :] = jnp.sum(w * cond, axis=0, keepdims=True) + b_ref[...]


def adaln(c, w_ada, b_ada, tn=1024):
    B, D = c.shape
    N = w_ada.shape[1]
    return pl.pallas_call(
        _adaln_kernel,
        grid=(N // tn,),
        in_specs=[pl.BlockSpec((B, D, 1), lambda j: (0, 0, 0)),
                  pl.BlockSpec((D, tn), lambda j: (0, j)),
                  pl.BlockSpec((1, tn), lambda j: (0, j))],
        out_specs=pl.BlockSpec((B, tn), lambda j: (0, j)),
        out_shape=jax.ShapeDtypeStruct((B, N), F32),
        compiler_params=_cparams(("parallel",)),
        name="adaln",
    )(c.reshape(B, D, 1), w_ada, b_ada.reshape(1, N))


def _modulate_kernel(x_ref, gain_ref, sh_ref, sc_ref, o_ref):
    x = x_ref[...]
    ms = jnp.mean(x * x, axis=-1, keepdims=True)
    y = x * lax.rsqrt(ms + NORM_EPS) * gain_ref[...]
    o_ref[...] = (y * (1.0 + sc_ref[...]) + sh_ref[...]).astype(o_ref.dtype)


def modulate(x, gain, mod, shift_blk, scale_blk, ts=512):
    B, S, D = x.shape
    return pl.pallas_call(
        _modulate_kernel,
        grid=(B, S // ts),
        in_specs=[pl.BlockSpec((None, ts, D), lambda b, i: (b, i, 0)),
                  pl.BlockSpec((1, D), lambda b, i: (0, 0)),
                  pl.BlockSpec((None, 1, D), lambda b, i: (b, 0, shift_blk)),
                  pl.BlockSpec((None, 1, D), lambda b, i: (b, 0, scale_blk))],
        out_specs=pl.BlockSpec((None, ts, D), lambda b, i: (b, i, 0)),
        out_shape=jax.ShapeDtypeStruct((B, S, D), BF16),
        compiler_params=_cparams(("parallel", "parallel")),
        name="modulate",
    )(x, gain.reshape(1, D), mod, mod)


def _rmsnorm_kernel(x_ref, gain_ref, o_ref):
    x = x_ref[...]
    ms = jnp.mean(x * x, axis=-1, keepdims=True)
    o_ref[...] = x * lax.rsqrt(ms + NORM_EPS) * gain_ref[...]


def rmsnorm(x, gain, ts=512):
    B, S, D = x.shape
    return pl.pallas_call(
        _rmsnorm_kernel,
        grid=(B, S // ts),
        in_specs=[pl.BlockSpec((None, ts, D), lambda b, i: (b, i, 0)),
                  pl.BlockSpec((1, D), lambda b, i: (0, 0))],
        out_specs=pl.BlockSpec((None, ts, D), lambda b, i: (b, i, 0)),
        out_shape=jax.ShapeDtypeStruct((B, S, D), F32),
        compiler_params=_cparams(("parallel", "parallel")),
        name="rmsnorm",
    )(x, gain.reshape(1, D))


def _mm_kernel(*refs, nk, epilogue):
    if epilogue == "residual":
        a_ref, w_ref, res_ref, gate_ref, o_ref = refs[:5]
        rest = refs[5:]
    else:
        a_ref, w_ref, o_ref = refs[:3]
        rest = refs[3:]

    def finish(acc):
        if epilogue == "relu2":
            r = jnp.maximum(acc, 0.0)
            acc = r * r
        elif epilogue == "residual":
            acc = res_ref[...] + gate_ref[...] * acc
        o_ref[...] = acc.astype(o_ref.dtype)

    part = jnp.dot(a_ref[...], w_ref[...], preferred_element_type=F32)
    if nk == 1:
        finish(part)
        return
    acc_ref = rest[0]
    k = pl.program_id(2)

    @pl.when(k == 0)
    def _():
        acc_ref[...] = part

    @pl.when(jnp.logical_and(k > 0, k < nk - 1))
    def _():
        acc_ref[...] += part

    @pl.when(k == nk - 1)
    def _():
        finish(acc_ref[...] + part)


def matmul(a, w, *, tm, tn, tk, out_dtype=F32, epilogue="none", res=None, mod=None,
           gate_blk=0, rows_per_batch=None):
    M, K = a.shape
    N = w.shape[1]
    nk = K // tk
    assert M % tm == 0 and N % tn == 0 and K % tk == 0
    in_specs = [pl.BlockSpec((tm, tk), lambda i, j, k: (i, k)),
                pl.BlockSpec((tk, tn), lambda i, j, k: (k, j))]
    args = [a, w]
    if epilogue == "residual":
        assert rows_per_batch % tm == 0
        gpb = D_MODEL // tn
        in_specs += [pl.BlockSpec((tm, tn), lambda i, j, k: (i, j)),
                     pl.BlockSpec((None, 1, tn),
                                  lambda i, j, k: (i * tm // rows_per_batch, 0, gate_blk * gpb + j))]
        args += [res, mod]
    scratch = [pltpu.VMEM((tm, tn), F32)] if nk > 1 else []
    return pl.pallas_call(
        functools.partial(_mm_kernel, nk=nk, epilogue=epilogue),
        grid=(M // tm, N // tn, nk),
        in_specs=in_specs,
        out_specs=pl.BlockSpec((tm, tn), lambda i, j, k: (i, j)),
        out_shape=jax.ShapeDtypeStruct((M, N), out_dtype),
        scratch_shapes=scratch,
        compiler_params=_cparams(("parallel", "parallel", "arbitrary")),
        name="mm_" + epilogue,
    )(*args)


def rope_tables(S):
    inv_freq = ROPE_THETA ** (-jnp.arange(0, HEAD_DIM, 2, dtype=F32) / HEAD_DIM)
    ang = jnp.arange(S, dtype=F32)[:, None] * inv_freq[None, :]
    cos, sin = jnp.cos(ang), jnp.sin(ang)
    return jnp.concatenate([cos, cos], axis=-1), jnp.concatenate([-sin, sin], axis=-1)


def _rope_cast_kernel(flags_ref, x_ref, c_ref, s_ref, o_ref, slab_ref, *, dil):
    j = pl.program_id(2)
    nslab = x_ref.shape[-1] // LANE
    rows = x_ref.shape[0] // dil

    def residue(r):
        return pl.ds(r, rows, stride=dil) if dil > 1 else slice(None)

    def slabs():
        for h in range(nslab):
            sl = slice(h * LANE, (h + 1) * LANE)
            if dil > 1:
                slab_ref[...] = x_ref[:, sl]
                yield sl, slab_ref
            else:
                yield sl, x_ref.at[:, sl]

    @pl.when(flags_ref[j] == 0)
    def _():
        for sl, src in slabs():
            for r in range(dil):
                o_ref[r, :, sl] = src[residue(r), :].astype(o_ref.dtype)

    @pl.when(flags_ref[j] != 0)
    def _():
        scale = jnp.where(flags_ref[j] == 2, QK_SCALE, 1.0).astype(F32)
        for sl, src in slabs():
            for r in range(dil):
                t = src[residue(r), :]
                rot = t * c_ref[residue(r), :] + pltpu.roll(t, HEAD_DIM // 2, 1) * s_ref[residue(r), :]
                o_ref[r, :, sl] = (rot * scale).astype(o_ref.dtype)


def rope_cast(x, cos2, sin2, *, width, first_blk, flags, dil=1, ts=512):
    B, S, _ = x.shape
    nblk = len(flags)
    grid_spec = pltpu.PrefetchScalarGridSpec(
        num_scalar_prefetch=1,
        grid=(B, S // ts, nblk),
        in_specs=[pl.BlockSpec((None, ts, width), lambda b, i, j, f: (b, i, first_blk + j)),
                  pl.BlockSpec((ts, LANE), lambda b, i, j, f: (i, 0)),
                  pl.BlockSpec((ts, LANE), lambda b, i, j, f: (i, 0))],
        out_specs=pl.BlockSpec((None, dil, ts // dil, width), lambda b, i, j, f: (b, 0, i, j)),
        scratch_shapes=[pltpu.VMEM((ts, LANE), F32)],
    )
    return pl.pallas_call(
        functools.partial(_rope_cast_kernel, dil=dil),
        grid_spec=grid_spec,
        out_shape=jax.ShapeDtypeStruct((B, dil, S // dil, nblk * width), BF16),
        compiler_params=_cparams(("parallel", "parallel", "arbitrary")),
        name="rope_cast",
    )(jnp.asarray(flags, jnp.int32), x, cos2, sin2)


def _ones_column(n):
    return jnp.where(lax.broadcasted_iota(jnp.int32, (n, LANE), 1) == 0, 1.0, 0.0).astype(BF16)


def _softmax_pv(s, m, v, ones):
    p = jnp.exp2(s - m).astype(BF16)
    res = jnp.dot(p, jnp.concatenate([v, ones], axis=1), preferred_element_type=F32)
    return res[:, LANE:LANE + 1], res[:, :LANE]


def _banded_kernel(q_ref, k_ref, v_ref, o_ref, *lse_refs, R, nwin, window, tq, qsub, nkeys):
    i = pl.program_id(3)
    L = k_ref.shape[0]
    rows = R * qsub
    rel = (lax.broadcasted_iota(jnp.int32, (rows, nkeys), 0) & (qsub - 1)) \
        - lax.broadcasted_iota(jnp.int32, (rows, nkeys), 1)
    ones = _ones_column(nkeys)
    for j in range(tq // qsub):
        q0 = i * tq + j * qsub
        kstart = pl.multiple_of(jnp.clip(q0 - nwin * LANE, 0, L - nkeys), LANE)
        qj = q_ref[j * qsub:(j + 1) * qsub, :]
        if R > 1:
            q = jnp.concatenate([qj[:, r * LANE:(r + 1) * LANE] for r in range(R)], axis=0)
        else:
            q = qj
        k = k_ref[pl.ds(kstart, nkeys), :]
        v = v_ref[pl.ds(kstart, nkeys), :]
        s = lax.dot_general(q, k, NT_DIMS, preferred_element_type=F32)
        diff = rel + (q0 - kstart)
        valid = lax.bitcast_convert_type(diff, jnp.uint32) <= jnp.uint32(window)
        s = jnp.where(valid, s, NEG_BIG)
        m = jnp.max(s, axis=-1, keepdims=True)
        l, o = _softmax_pv(s, m, v, ones)
        o = o / l
        for r in range(R):
            o_ref[j * qsub:(j + 1) * qsub, r * LANE:(r + 1) * LANE] = o[r * qsub:(r + 1) * qsub]
        if lse_refs:
            lse_refs[0][j * qsub:(j + 1) * qsub, :] = jnp.broadcast_to(
                m + jnp.log(l) * LOG2E, (rows, LANE))


def banded_attention(q_arr, kv_arr, *, n_kv, R, q_blk, k_blk, v_blk, nwin, window,
                     want_lse):
    B, dil, L, _ = q_arr.shape
    tq = min(512, L)
    qsub = LANE if R > 1 else tq
    nkeys = min(qsub + nwin * LANE, L)
    assert L % tq == 0 and q_blk % R == 0
    width = n_kv * R * LANE
    out_shape = [jax.ShapeDtypeStruct((B, dil, L, width), F32)]
    out_specs = [pl.BlockSpec((None, None, tq, R * LANE), lambda b, r, h, i: (b, r, i, h))]
    if want_lse:
        out_shape.append(out_shape[0])
        out_specs.append(out_specs[0])
    return pl.pallas_call(
        functools.partial(_banded_kernel, R=R, nwin=nwin, window=window, tq=tq, qsub=qsub,
                          nkeys=nkeys),
        grid=(B, dil, n_kv, L // tq),
        in_specs=[pl.BlockSpec((None, None, tq, R * LANE),
                               lambda b, r, h, i: (b, r, i, q_blk // R + h)),
                  pl.BlockSpec((None, None, L, LANE), lambda b, r, h, i: (b, r, 0, k_blk + h)),
                  pl.BlockSpec((None, None, L, LANE), lambda b, r, h, i: (b, r, 0, v_blk + h))],
        out_specs=out_specs,
        out_shape=out_shape,
        compiler_params=_cparams(("parallel", "parallel", "parallel", "arbitrary")),
        name="banded_attn",
    )(q_arr, kv_arr, kv_arr)


def _compress_kernel(p_ref, pe_ref, w1_ref, w2_ref, o_ref):
    half = CMP_STRIDE * HEAD_DIM
    pieces = p_ref[...]
    top = (pieces + pe_ref[0:1, :]).astype(BF16)
    bot = (pieces + pe_ref[1:2, :]).astype(BF16)
    a = jnp.dot(top, w1_ref[0:half, :], preferred_element_type=F32)
    b = jnp.dot(bot, w1_ref[half:2 * half, :], preferred_element_type=F32)
    n = a.shape[0]
    hid = a + pltpu.roll(b, n - 1, 0)
    o_ref[...] = jnp.dot(jax.nn.gelu(hid).astype(BF16), w2_ref[...], preferred_element_type=F32)


def compress(pieces, pe, w1, w2):
    _, B, G, NP, W = pieces.shape
    return pl.pallas_call(
        _compress_kernel,
        grid=(2, B, G),
        in_specs=[pl.BlockSpec((None, None, None, NP, W), lambda t, b, g: (t, b, g, 0, 0)),
                  pl.BlockSpec((None, 2, W), lambda t, b, g: (t, 0, 0)),
                  pl.BlockSpec((None, 2 * W, CMP_HIDDEN), lambda t, b, g: (t, 0, 0)),
                  pl.BlockSpec((None, CMP_HIDDEN, HEAD_DIM), lambda t, b, g: (t, 0, 0))],
        out_specs=pl.BlockSpec((None, None, None, NP, HEAD_DIM), lambda t, b, g: (t, b, g, 0, 0)),
        out_shape=jax.ShapeDtypeStruct((2, B, G, NP, HEAD_DIM), F32),
        compiler_params=_cparams(("parallel", "parallel", "parallel")),
        name="nsa_compress",
    )(pieces, pe, w1, w2)


def _cmp_attn_kernel(q_ref, kc_ref, vc_ref, ov_ref, o_ref, imp_ref, *, tq):
    i = pl.program_id(2)
    R = NSA_REP
    qj = q_ref[...]
    q = jnp.concatenate([qj[:, r * LANE:(r + 1) * LANE] for r in range(R)], axis=0)
    q = (q * QK_SCALE).astype(BF16)
    kc = kc_ref[...].astype(BF16)
    s = lax.dot_general(q, kc, NT_DIMS, preferred_element_type=F32)
    t = i * tq + (lax.broadcasted_iota(jnp.int32, s.shape, 0) & (tq - 1))
    cmp_end = lax.broadcasted_iota(jnp.int32, s.shape, 1) * CMP_STRIDE + (CMP_BLOCK - 1)
    valid = cmp_end <= t
    s = jnp.where(valid, s, NEG_BIG)
    m = jnp.max(s, axis=-1, keepdims=True)
    p = jnp.where(valid, jnp.exp2(s - m), 0.0)
    l = jnp.sum(p, axis=-1, keepdims=True)
    p = p / jnp.where(l > 0, l, 1.0)
    o = jnp.dot(p.astype(BF16), vc_ref[...].astype(BF16), preferred_element_type=F32)
    for r in range(R):
        o_ref[:, r * LANE:(r + 1) * LANE] = o[r * tq:(r + 1) * tq]
    psum = p[0:tq]
    for r in range(1, R):
        psum = psum + p[r * tq:(r + 1) * tq]
    p_hi = psum.astype(BF16)
    p_lo = (psum - p_hi.astype(F32)).astype(BF16)
    ov = ov_ref[...]
    imp_ref[...] = (lax.dot_general(ov, p_hi, NT_DIMS, preferred_element_type=F32)
                    + lax.dot_general(ov, p_lo, NT_DIMS, preferred_element_type=F32))


def cmp_attention(proj, kvc, overlap_t, tq=128):
    B, S, _ = proj.shape
    G = NSA_KV_GROUPS
    NP = kvc.shape[3]
    n_sel = overlap_t.shape[0]
    qw = NSA_REP * LANE
    return pl.pallas_call(
        functools.partial(_cmp_attn_kernel, tq=tq),
        grid=(B, G, S // tq),
        in_specs=[pl.BlockSpec((None, tq, qw), lambda b, g, i: (b, i, g)),
                  pl.BlockSpec((None, None, None, NP, LANE), lambda b, g, i: (0, b, g, 0, 0)),
                  pl.BlockSpec((None, None, None, NP, LANE), lambda b, g, i: (1, b, g, 0, 0)),
                  pl.BlockSpec((n_sel, NP), lambda b, g, i: (0, 0))],
        out_specs=[pl.BlockSpec((None, tq, qw), lambda b, g, i: (b, i, g)),
                   pl.BlockSpec((None, None, n_sel, tq), lambda b, g, i: (b, g, 0, i))],
        out_shape=[jax.ShapeDtypeStruct((B, S, NSA_HEADS * LANE), F32),
                   jax.ShapeDtypeStruct((B, G, n_sel, S), F32)],
        compiler_params=_cparams(("parallel", "parallel", "parallel")),
        name="nsa_cmp_attn",
    )(proj, kvc, kvc, overlap_t)


def _topk_kernel(imp_ref, o_ref, *, tq):
    i = pl.program_id(2)
    imp = imp_ref[...]
    n_sel = imp.shape[0]
    blk = lax.broadcasted_iota(jnp.int32, imp.shape, 0)
    t = i * tq + lax.broadcasted_iota(jnp.int32, imp.shape, 1)
    cur = t // SEL_BLOCK
    avail = blk <= cur
    forced = jnp.where(blk == 0, 1.0, jnp.where(blk == cur, 1.0, jnp.where(blk == cur - 1, 1.0, 0.0)))
    score = jnp.where(avail, imp + FORCED_BONUS * forced, -jnp.inf)
    picked = jnp.zeros(imp.shape, F32)
    for _ in range(min(SEL_TOPK, n_sel)):
        mx = jnp.max(score, axis=0, keepdims=True)
        first = jnp.min(jnp.where(score == mx, blk, n_sel), axis=0, keepdims=True)
        hit = blk == first
        picked = jnp.where(hit, 1.0, picked)
        score = jnp.where(hit, -jnp.inf, score)
    feat = jnp.where(avail, jnp.where(picked > 0.0, 0.0, SEL_OFF), SEL_OFF)
    o_ref[...] = feat.T.astype(o_ref.dtype)


def topk_select(imp_t, tq=256):
    B, G, n_sel, S = imp_t.shape
    return pl.pallas_call(
        functools.partial(_topk_kernel, tq=tq),
        grid=(B, G, S // tq),
        in_specs=[pl.BlockSpec((None, None, n_sel, tq), lambda b, g, i: (b, g, 0, i))],
        out_specs=pl.BlockSpec((None, None, tq, n_sel), lambda b, g, i: (b, g, i, 0)),
        out_shape=jax.ShapeDtypeStruct((B, G, S, n_sel), BF16),
        compiler_params=_cparams(("parallel", "parallel", "parallel")),
        name="nsa_topk",
    )(imp_t)


def _sel_attn_kernel(q_ref, mf_ref, k_ref, e_ref, vt_ref, o_ref, *, tq, qsub, tkv):
    i = pl.program_id(2)
    R = NSA_REP
    nsub = tq // qsub
    rows = R * qsub
    q0 = i * tq
    ntile = (q0 + tq - 1) // tkv + 1
    rel = (lax.broadcasted_iota(jnp.int32, (tkv, rows), 1) & (qsub - 1)) \
        - lax.broadcasted_iota(jnp.int32, (tkv, rows), 0)

    qa = []
    for u in range(nsub):
        qu = q_ref[u * qsub:(u + 1) * qsub, :]
        mf = mf_ref[u * qsub:(u + 1) * qsub, :]
        qa.append(jnp.concatenate(
            [jnp.concatenate([qu[:, r * LANE:(r + 1) * LANE], mf], axis=1) for r in range(R)],
            axis=0))

    def body(j, carry):
        ks = pl.multiple_of(j * tkv, tkv)
        ka = jnp.concatenate([k_ref[pl.ds(ks, tkv), :], e_ref[pl.ds(ks, tkv), :]], axis=1)
        vt = vt_ref[:, pl.ds(ks, tkv)]
        out = []
        for u in range(nsub):
            m, l, acc = carry[u]
            st = lax.dot_general(ka, qa[u], NT_DIMS, preferred_element_type=F32)
            st = jnp.where(rel >= ks - (q0 + u * qsub), st, NEG_BIG)
            m_new = jnp.maximum(m, jnp.max(st, axis=0, keepdims=True))
            a = jnp.exp2(m - m_new)
            res = jnp.dot(vt, jnp.exp2(st - m_new).astype(BF16), preferred_element_type=F32)
            out.append((m_new, a * l + res[LANE:LANE + 1, :], a * acc + res[:LANE, :]))
        return tuple(out)

    init = tuple((jnp.full((1, rows), NEG_BIG, F32), jnp.zeros((1, rows), F32),
                  jnp.zeros((LANE, rows), F32)) for _ in range(nsub))
    final = lax.fori_loop(0, ntile, body, init)
    for u in range(nsub):
        _, l, acc = final[u]
        ot = acc / l
        for r in range(R):
            o_ref[u * qsub:(u + 1) * qsub, r * LANE:(r + 1) * LANE] = \
                ot[:, r * qsub:(r + 1) * qsub].T


def sel_attention(q_rot, mfeat, kvb, onehot, *, k_blk, v_blk, tq=512, qsub=128, tkv=512):
    B, S, _ = q_rot.shape
    G = NSA_KV_GROUPS
    n_sel = onehot.shape[1]
    qw = NSA_REP * LANE
    tkv = min(tkv, S)
    v = kvb[:, :, v_blk * LANE:(v_blk + G) * LANE].reshape(B, S, G, LANE).transpose(0, 2, 3, 1)
    pad = jnp.zeros((B, G, 16, S), BF16).at[:, :, 0, :].set(1.0)
    vt = jnp.concatenate([v, pad], axis=2)
    return pl.pallas_call(
        functools.partial(_sel_attn_kernel, tq=tq, qsub=qsub, tkv=tkv),
        grid=(B, G, S // tq),
        in_specs=[pl.BlockSpec((None, tq, qw), lambda b, g, i: (b, i, g)),
                  pl.BlockSpec((None, None, tq, n_sel), lambda b, g, i: (b, g, i, 0)),
                  pl.BlockSpec((None, S, LANE), lambda b, g, i: (b, 0, k_blk + g)),
                  pl.BlockSpec((S, n_sel), lambda b, g, i: (0, 0)),
                  pl.BlockSpec((None, None, LANE + 16, S), lambda b, g, i: (b, g, 0, 0))],
        out_specs=pl.BlockSpec((None, tq, qw), lambda b, g, i: (b, i, g)),
        out_shape=jax.ShapeDtypeStruct((B, S, NSA_HEADS * LANE), F32),
        compiler_params=_cparams(("parallel", "parallel", "arbitrary")),
        name="nsa_sel_attn",
    )(q_rot, mfeat, kvb, onehot, vt)


def _nsa_combine_kernel(g_ref, oc_ref, os_ref, ow_ref, o_ref):
    gates = jax.nn.sigmoid(g_ref[...])
    for h in range(NSA_HEADS):
        sl = slice(h * LANE, (h + 1) * LANE)
        acc = gates[:, 3 * h:3 * h + 1] * oc_ref[:, sl]
        acc = acc + gates[:, 3 * h + 1:3 * h + 2] * os_ref[:, sl]
        acc = acc + gates[:, 3 * h + 2:3 * h + 3] * ow_ref[:, sl]
        o_ref[:, sl] = acc.astype(o_ref.dtype)


def nsa_combine(gate_logits, o_cmp, o_sel, o_win, ts=256):
    B, S, W = o_cmp.shape
    spec = pl.BlockSpec((None, ts, W), lambda b, i: (b, i, 0))
    return pl.pallas_call(
        _nsa_combine_kernel,
        grid=(B, S // ts),
        in_specs=[pl.BlockSpec((None, ts, LANE), lambda b, i: (b, i, 0)), spec, spec, spec],
        out_specs=spec,
        out_shape=jax.ShapeDtypeStruct((B, S, W), BF16),
        compiler_params=_cparams(("parallel", "parallel")),
        name="nsa_combine",
    )(gate_logits, o_cmp, o_sel, o_win)


def _dil_combine_kernel(*refs, dils):
    n = len(dils)
    o_refs, lse_refs, out_ref = refs[:n], refs[n:2 * n], refs[2 * n]
    nat_o, nat_l = refs[2 * n + 1:3 * n + 1], refs[3 * n + 1:]
    ts = out_ref.shape[0]
    for h in range(out_ref.shape[1] // LANE):
        sl = slice(h * LANE, (h + 1) * LANE)
        for g, dil in enumerate(dils):
            for r in range(dil):
                rows = pl.ds(r, ts // dil, stride=dil) if dil > 1 else slice(None)
                nat_o[g][rows, :] = o_refs[g][r, :, sl]
                nat_l[g][rows, :] = lse_refs[g][r, :, sl]
        lses = [r[...] for r in nat_l]
        mx = functools.reduce(jnp.maximum, lses)
        ws = [jnp.exp2(l - mx) for l in lses]
        den = functools.reduce(lambda a, b: a + b, ws)
        num = functools.reduce(lambda a, b: a + b, [w * r[...] for w, r in zip(ws, nat_o)])
        out_ref[:, sl] = (num / den).astype(out_ref.dtype)


def dil_combine(outs, lses, ts=128):
    dils = tuple(o.shape[1] for o in outs)
    B, W = outs[0].shape[0], outs[0].shape[-1]
    S = outs[0].shape[1] * outs[0].shape[2]
    specs = [pl.BlockSpec((None, d, ts // d, W), lambda b, i: (b, 0, i, 0)) for d in dils]
    return pl.pallas_call(
        functools.partial(_dil_combine_kernel, dils=dils),
        grid=(B, S // ts),
        in_specs=specs + specs,
        out_specs=pl.BlockSpec((None, ts, W), lambda b, i: (b, i, 0)),
        out_shape=jax.ShapeDtypeStruct((B, S, W), BF16),
        scratch_shapes=[pltpu.VMEM((ts, LANE), F32)] * (2 * len(dils)),
        compiler_params=_cparams(("parallel", "parallel")),
        name="dil_combine",
    )(*outs, *lses)


def _softplus(x):
    return jnp.maximum(x, 0.0) + jnp.log1p(jnp.exp(-jnp.abs(x)))


def _gate_band_starts():
    ntile = D_RNN // LANE
    starts = []
    for j in range(ntile):
        n_lo = (j * LANE) // RNN_BLOCK_DIM
        n_hi = (j * LANE + LANE - 1) // RNN_BLOCK_DIM
        lo = (n_lo * RNN_BLOCK_DIM) // LANE
        hi = -(-((n_hi + 1) * RNN_BLOCK_DIM) // LANE)
        assert hi - lo <= 4
        starts.append(min(lo, ntile - 4))
    return starts


def _rglru_kernel(y_ref, xr_ref, cw_ref, cb_ref, wg_ref, bg_ref, lam_ref, o_ref,
                  h_ref, tail_ref, *, ts, starts):
    i = pl.program_id(1)

    @pl.when(i == 0)
    def _():
        h_ref[...] = jnp.zeros_like(h_ref)
        tail_ref[...] = jnp.zeros_like(tail_ref)

    xr = xr_ref[...]
    ext = jnp.concatenate([tail_ref[...], xr], axis=0)
    x = cb_ref[...] + xr * cw_ref[CONV_WIDTH - 1:CONV_WIDTH, :]
    for d in range(1, CONV_WIDTH):
        shifted = pltpu.roll(ext, d, 0)[8:8 + ts]
        x = x + shifted * cw_ref[CONV_WIDTH - 1 - d:CONV_WIDTH - d, :]
    tail_ref[...] = xr[ts - 8:ts]

    xb = x.astype(BF16)
    gl = []
    for g in range(2):
        tiles = [jnp.dot(xb[:, a * LANE:(a + 4) * LANE], wg_ref[g, j],
                         preferred_element_type=F32) for j, a in enumerate(starts)]
        gl.append(jnp.concatenate(tiles, axis=1) + bg_ref[g:g + 1, :])
    r = jax.nn.sigmoid(gl[0])
    ig = jax.nn.sigmoid(gl[1])
    log_a = (-LRU_C) * r * _softplus(-lam_ref[...])
    a = jnp.exp(log_a)
    b = jnp.sqrt(-jnp.tanh(log_a) * (a * a + 1.0)) * (ig * x)

    row = lax.broadcasted_iota(jnp.int32, a.shape, 0)
    d = 1
    while d < ts:
        keep = row >= d
        b = b + a * jnp.where(keep, pltpu.roll(b, d, 0), 0.0)
        a = a * jnp.where(keep, pltpu.roll(a, d, 0), 1.0)
        d *= 2
    h = a * h_ref[0:1, :] + b
    h_ref[0:1, :] = h[ts - 1:ts, :]
    o_ref[...] = (h * jax.nn.gelu(y_ref[...])).astype(o_ref.dtype)


def rglru_scan(proj, conv_w, conv_b, wband, b_gate, lam, ts=128):
    B, S, _ = proj.shape
    C = D_RNN
    starts = _gate_band_starts()
    vec = lambda n: pl.BlockSpec((n, C), lambda b, i: (0, 0))
    return pl.pallas_call(
        functools.partial(_rglru_kernel, ts=ts, starts=starts),
        grid=(B, S // ts),
        in_specs=[pl.BlockSpec((None, ts, C), lambda b, i: (b, i, 0)),
                  pl.BlockSpec((None, ts, C), lambda b, i: (b, i, 1)),
                  vec(CONV_WIDTH), vec(1),
                  pl.BlockSpec(wband.shape, lambda b, i: (0, 0, 0, 0)),
                  vec(2), vec(1)],
        out_specs=pl.BlockSpec((None, ts, C), lambda b, i: (b, i, 0)),
        out_shape=jax.ShapeDtypeStruct((B, S, C), BF16),
        scratch_shapes=[pltpu.VMEM((8, C), F32), pltpu.VMEM((8, C), F32)],
        compiler_params=_cparams(("arbitrary", "arbitrary")),
        name="rglru_scan",
    )(proj, proj, conv_w, conv_b.reshape(1, C), wband, b_gate, lam.reshape(1, C))


def _gate_band_weights(w_gate):
    starts = _gate_band_starts()
    dense = jnp.stack([jax.scipy.linalg.block_diag(*[w_gate[g, n] for n in range(RNN_BLOCKS)])
                       for g in range(2)])
    tiles = [dense[:, a * LANE:(a + 4) * LANE, j * LANE:(j + 1) * LANE]
             for j, a in enumerate(starts)]
    return jnp.stack(tiles, axis=1).astype(BF16)


def _row_tile(T):
    return 512 if T % 512 == 0 else T


def nsa_mixer(hn, x, mod, gate_blk, w_in, cmp_pe, cmp_w1, cmp_w2, w_out, cos2, sin2):
    B, S, D = x.shape
    T = B * S
    G = NSA_KV_GROUPS
    hq = NSA_HEADS * HEAD_DIM
    main = hq + 6 * NSA_KV_WIDTH
    tm = _row_tile(T)
    hn2 = hn.reshape(T, D)
    proj = matmul(hn2, w_in[:, :main].astype(BF16), tm=tm, tn=main // 2, tk=D).reshape(B, S, main)
    w_gate = jnp.pad(w_in[:, main:], ((0, 0), (0, LANE - 3 * NSA_HEADS))).astype(BF16)
    gate_logits = matmul(hn2, w_gate, tm=tm, tn=LANE, tk=D).reshape(B, S, LANE)

    q_rot = rope_cast(proj, cos2, sin2, width=hq, first_blk=0, flags=[2])
    kvb = rope_cast(proj, cos2, sin2, width=NSA_KV_WIDTH, first_blk=hq // NSA_KV_WIDTH + 2,
                    flags=[1, 0, 1, 0])

    npiece = S // CMP_STRIDE
    kv_cmp = proj[:, :, hq:hq + 2 * NSA_KV_WIDTH].reshape(B, npiece, CMP_STRIDE, 2, G, HEAD_DIM)
    pieces = kv_cmp.transpose(3, 0, 4, 1, 2, 5).reshape(2, B, G, npiece, CMP_STRIDE * HEAD_DIM)
    kvc = compress(pieces, cmp_pe.reshape(2, 2, CMP_STRIDE * HEAD_DIM),
                   cmp_w1.astype(BF16), cmp_w2.astype(BF16))

    n_sb = S // SEL_BLOCK
    cmp_start = np.arange(npiece) * CMP_STRIDE
    sel_start = np.arange(n_sb) * SEL_BLOCK
    overlap = ((cmp_start[:, None] < sel_start[None, :] + SEL_BLOCK)
               & (cmp_start[:, None] + CMP_BLOCK > sel_start[None, :]))
    overlap[npiece - 1] = False
    overlap_t = jnp.asarray(overlap.T, BF16)
    onehot = jnp.asarray(np.arange(S)[:, None] // SEL_BLOCK == np.arange(n_sb)[None, :], BF16)

    o_cmp, imp_t = cmp_attention(proj, kvc, overlap_t)
    mfeat = topk_select(imp_t)
    kb = NSA_KV_WIDTH // LANE
    o_sel = sel_attention(q_rot.reshape(B, S, hq), mfeat, kvb.reshape(B, S, 4 * NSA_KV_WIDTH),
                          onehot, k_blk=0, v_blk=kb)
    (o_win,) = banded_attention(q_rot, kvb, n_kv=G, R=NSA_REP, q_blk=0, k_blk=2 * kb,
                                v_blk=3 * kb, nwin=(NSA_WINDOW - 1 + LANE - 1) // LANE,
                                window=NSA_WINDOW - 1, want_lse=False)
    o = nsa_combine(gate_logits, o_cmp, o_sel, o_win.reshape(B, S, hq))
    return matmul(o.reshape(T, hq), w_out.astype(BF16), tm=tm, tn=D, tk=hq // 2,
                  epilogue="residual", res=x.reshape(T, D), mod=mod, gate_blk=gate_blk,
                  rows_per_batch=S).reshape(B, S, D)


def dilated_mixer(hn, x, mod, gate_blk, w_in, w_out, cos2, sin2):
    B, S, D = x.shape
    T = B * S
    H = DIL_HEADS
    hw = H * HEAD_DIM
    tm = _row_tile(T)
    proj = matmul(hn.reshape(T, D), w_in.astype(BF16), tm=tm, tn=hw, tk=D)
    proj = proj.reshape(B, S, w_in.shape[1])
    outs, lses = [], []
    for g, (window, dil) in enumerate(DIL_PATTERNS):
        w = window // dil
        qkv = rope_cast(proj, cos2, sin2, width=hw, first_blk=3 * g, flags=[2, 1, 0], dil=dil)
        o, lse = banded_attention(qkv, qkv, n_kv=H, R=1, q_blk=0, k_blk=H, v_blk=2 * H,
                                  nwin=(w + LANE - 1) // LANE, window=w, want_lse=True)
        outs.append(o)
        lses.append(lse)
    o = dil_combine(outs, lses)
    return matmul(o.reshape(T, hw), w_out.astype(BF16), tm=tm, tn=D, tk=hw // 2,
                  epilogue="residual", res=x.reshape(T, D), mod=mod, gate_blk=gate_blk,
                  rows_per_batch=S).reshape(B, S, D)


def rglru_mixer(hn, x, mod, gate_blk, w_in, conv_w, conv_b, w_gate, b_gate, lam, w_out):
    B, S, D = x.shape
    T = B * S
    tm = _row_tile(T)
    proj = matmul(hn.reshape(T, D), w_in.astype(BF16), tm=tm, tn=D_RNN, tk=D)
    hy = rglru_scan(proj.reshape(B, S, 2 * D_RNN), conv_w, conv_b, _gate_band_weights(w_gate),
                    b_gate, lam)
    return matmul(hy.reshape(T, D_RNN), w_out.astype(BF16), tm=tm, tn=D, tk=D_RNN // 3,
                  epilogue="residual", res=x.reshape(T, D), mod=mod, gate_blk=gate_blk,
                  rows_per_batch=S).reshape(B, S, D)


def mlp(hn, x, mod, gate_blk, w1, w2):
    B, S, D = x.shape
    T = B * S
    tm = _row_tile(T)
    dff = w1.shape[1]
    h = matmul(hn.reshape(T, D), w1.astype(BF16), tm=tm, tn=1024, tk=D, out_dtype=BF16,
               epilogue="relu2")
    return matmul(h, w2.astype(BF16), tm=tm, tn=D, tk=1024, epilogue="residual",
                  res=x.reshape(T, D), mod=mod, gate_blk=gate_blk,
                  rows_per_batch=S).reshape(B, S, D)


def kernel(x, c, l0_w_ada, l0_b_ada, l0_norm1, l0_w_in, l0_cmp_pe, l0_cmp_w1, l0_cmp_w2, l0_w_out, l0_norm2, l0_w_ff1, l0_w_ff2, l1_w_ada, l1_b_ada, l1_norm1, l1_w_in, l1_w_out, l1_norm2, l1_w_ff1, l1_w_ff2, l2_w_ada, l2_b_ada, l2_norm1, l2_w_in, l2_conv_w, l2_conv_b, l2_w_gate, l2_b_gate, l2_lambda, l2_w_out, l2_norm2, l2_w_ff1, l2_w_ff2, l3_w_ada, l3_b_ada, l3_norm1, l3_w_in, l3_cmp_pe, l3_cmp_w1, l3_cmp_w2, l3_w_out, l3_norm2, l3_w_ff1, l3_w_ff2, norm_f):
    B, S, D = x.shape
    cos2, sin2 = rope_tables(S)
    layers = (
        (l0_w_ada, l0_b_ada, l0_norm1, l0_norm2, l0_w_ff1, l0_w_ff2,
         (l0_w_in, l0_cmp_pe, l0_cmp_w1, l0_cmp_w2, l0_w_out)),
        (l1_w_ada, l1_b_ada, l1_norm1, l1_norm2, l1_w_ff1, l1_w_ff2, (l1_w_in, l1_w_out)),
        (l2_w_ada, l2_b_ada, l2_norm1, l2_norm2, l2_w_ff1, l2_w_ff2,
         (l2_w_in, l2_conv_w, l2_conv_b, l2_w_gate, l2_b_gate, l2_lambda, l2_w_out)),
        (l3_w_ada, l3_b_ada, l3_norm1, l3_norm2, l3_w_ff1, l3_w_ff2,
         (l3_w_in, l3_cmp_pe, l3_cmp_w1, l3_cmp_w2, l3_w_out)),
    )
    for li in range(DEPTH):
        w_ada, b_ada, n1, n2, ff1, ff2, mix = layers[li]
        mod = adaln(c, w_ada, b_ada).reshape(B, 1, 6 * D)
        hn = modulate(x, n1, mod, 0, 1)
        kind = li % N_MIXERS
        if kind == 0:
            x = nsa_mixer(hn, x, mod, 2, *mix, cos2, sin2)
        elif kind == 1:
            x = dilated_mixer(hn, x, mod, 2, *mix, cos2, sin2)
        else:
            x = rglru_mixer(hn, x, mod, 2, *mix)
        hn = modulate(x, n2, mod, 3, 4)
        x = mlp(hn, x, mod, 5, ff1, ff2)
    return rmsnorm(x, norm_f)
```

```python
import functools
import math

import jax
import jax.numpy as jnp
import numpy as np
from jax import lax
from jax.experimental import pallas as pl
from jax.experimental.pallas import tpu as pltpu

F32 = jnp.float32
BF16 = jnp.bfloat16

D_MODEL = 2048
DEPTH = 4
N_MIXERS = 3
HEAD_DIM = 128
ROPE_THETA = 10000.0
NORM_EPS = 1e-6

NSA_HEADS = D_MODEL // HEAD_DIM
NSA_KV_GROUPS = 4
NSA_REP = NSA_HEADS // NSA_KV_GROUPS
NSA_KV_WIDTH = NSA_KV_GROUPS * HEAD_DIM
CMP_BLOCK = 32
CMP_STRIDE = 16
CMP_HIDDEN = 4 * HEAD_DIM
SEL_BLOCK = 64
SEL_TOPK = 16
NSA_WINDOW = 512
FORCED_BONUS = 1e9

DIL_HEADS = D_MODEL // HEAD_DIM
DIL_PATTERNS = ((128, 1), (512, 4), (2048, 16))

D_RNN = 2688
RNN_BLOCKS = 16
RNN_BLOCK_DIM = D_RNN // RNN_BLOCKS
CONV_WIDTH = 4
LRU_C = 8.0

LANE = 128
LOG2E = math.log2(math.e)
QK_SCALE = LOG2E / math.sqrt(HEAD_DIM)
NEG_BIG = -1e30
SEL_OFF = -float(2 ** 30)
VMEM_LIMIT = 56 * 1024 * 1024

NT_DIMS = (((1,), (1,)), ((), ()))


def _cparams(sem):
    return pltpu.CompilerParams(dimension_semantics=sem, vmem_limit_bytes=VMEM_LIMIT)


def _adaln_kernel(c_ref, w_ref, b_ref, o_ref):
    w = w_ref[...]
    for b in range(c_ref.shape[0]):
        c = c_ref[b]
        cond = c * jax.nn.sigmoid(c)
        o_ref[b:b + 1, :] = jnp.sum(w * cond, axis=0, keepdims=True) + b_ref[...]


def adaln(c, w_ada, b_ada, tn=1024):
    B, D = c.shape
    N = w_ada.shape[1]
    return pl.pallas_call(
        _adaln_kernel,
        grid=(N // tn,),
        in_specs=[pl.BlockSpec((B, D, 1), lambda j: (0, 0, 0)),
                  pl.BlockSpec((D, tn), lambda j: (0, j)),
                  pl.BlockSpec((1, tn), lambda j: (0, j))],
        out_specs=pl.BlockSpec((B, tn), lambda j: (0, j)),
        out_shape=jax.ShapeDtypeStruct((B, N), F32),
        compiler_params=_cparams(("parallel",)),
        name="adaln",
    )(c.reshape(B, D, 1), w_ada, b_ada.reshape(1, N))


def _modulate_kernel(x_ref, gain_ref, sh_ref, sc_ref, o_ref):
    x = x_ref[...]
    ms = jnp.mean(x * x, axis=-1, keepdims=True)
    y = x * lax.rsqrt(ms + NORM_EPS) * gain_ref[...]
    o_ref[...] = (y * (1.0 + sc_ref[...]) + sh_ref[...]).astype(o_ref.dtype)


def modulate(x, gain, mod, shift_blk, scale_blk, ts=512):
    B, S, D = x.shape
    return pl.pallas_call(
        _modulate_kernel,
        grid=(B, S // ts),
        in_specs=[pl.BlockSpec((None, ts, D), lambda b, i: (b, i, 0)),
                  pl.BlockSpec((1, D), lambda b, i: (0, 0)),
                  pl.BlockSpec((None, 1, D), lambda b, i: (b, 0, shift_blk)),
                  pl.BlockSpec((None, 1, D), lambda b, i: (b, 0, scale_blk))],
        out_specs=pl.BlockSpec((None, ts, D), lambda b, i: (b, i, 0)),
        out_shape=jax.ShapeDtypeStruct((B, S, D), BF16),
        compiler_params=_cparams(("parallel", "parallel")),
        name="modulate",
    )(x, gain.reshape(1, D), mod, mod)


def _rmsnorm_kernel(x_ref, gain_ref, o_ref):
    x = x_ref[...]
    ms = jnp.mean(x * x, axis=-1, keepdims=True)
    o_ref[...] = x * lax.rsqrt(ms + NORM_EPS) * gain_ref[...]


def rmsnorm(x, gain, ts=512):
    B, S, D = x.shape
    return pl.pallas_call(
        _rmsnorm_kernel,
        grid=(B, S // ts),
        in_specs=[pl.BlockSpec((None, ts, D), lambda b, i: (b, i, 0)),
                  pl.BlockSpec((1, D), lambda b, i: (0, 0))],
        out_specs=pl.BlockSpec((None, ts, D), lambda b, i: (b, i, 0)),
        out_shape=jax.ShapeDtypeStruct((B, S, D), F32),
        compiler_params=_cparams(("parallel", "parallel")),
        name="rmsnorm",
    )(x, gain.reshape(1, D))


def _mm_kernel(*refs, nk, epilogue):
    if epilogue == "residual":
        a_ref, w_ref, res_ref, gate_ref, o_ref = refs[:5]
        rest = refs[5:]
    else:
        a_ref, w_ref, o_ref = refs[:3]
        rest = refs[3:]

    def finish(acc):
        if epilogue == "relu2":
            r = jnp.maximum(acc, 0.0)
            acc = r * r
        elif epilogue == "residual":
            acc = res_ref[...] + gate_ref[...] * acc
        o_ref[...] = acc.astype(o_ref.dtype)

    def part():
        return jnp.dot(a_ref[...], w_ref[...], preferred_element_type=F32)

    if nk == 1:
        finish(part())
        return
    acc_ref = rest[0]
    k = pl.program_id(2)

    @pl.when(k == 0)
    def _():
        acc_ref[...] = part()

    if nk > 2:
        @pl.when(jnp.logical_and(k > 0, k < nk - 1))
        def _():
            acc_ref[...] += part()

    @pl.when(k == nk - 1)
    def _():
        finish(acc_ref[...] + part())


def matmul(a, w, *, tm, tn, tk, out_dtype=F32, epilogue="none", res=None, mod=None,
           gate_blk=0, rows_per_batch=None):
    M, K = a.shape
    N = w.shape[1]
    nk = K // tk
    assert M % tm == 0 and N % tn == 0 and K % tk == 0
    in_specs = [pl.BlockSpec((tm, tk), lambda i, j, k: (i, k)),
                pl.BlockSpec((tk, tn), lambda i, j, k: (k, j))]
    args = [a, w]
    if epilogue == "residual":
        assert rows_per_batch % tm == 0
        gpb = D_MODEL // tn
        in_specs += [pl.BlockSpec((tm, tn), lambda i, j, k: (i, j)),
                     pl.BlockSpec((None, 1, tn),
                                  lambda i, j, k: (i * tm // rows_per_batch, 0, gate_blk * gpb + j))]
        args += [res, mod]
    scratch = [pltpu.VMEM((tm, tn), F32)] if nk > 1 else []
    return pl.pallas_call(
        functools.partial(_mm_kernel, nk=nk, epilogue=epilogue),
        grid=(M // tm, N // tn, nk),
        in_specs=in_specs,
        out_specs=pl.BlockSpec((tm, tn), lambda i, j, k: (i, j)),
        out_shape=jax.ShapeDtypeStruct((M, N), out_dtype),
        scratch_shapes=scratch,
        compiler_params=_cparams(("parallel", "parallel", "arbitrary")),
        name="mm_" + epilogue,
    )(*args)


def rope_tables(S):
    inv_freq = ROPE_THETA ** (-jnp.arange(0, HEAD_DIM, 2, dtype=F32) / HEAD_DIM)
    ang = jnp.arange(S, dtype=F32)[:, None] * inv_freq[None, :]
    cos, sin = jnp.cos(ang), jnp.sin(ang)
    return jnp.concatenate([cos, cos], axis=-1), jnp.concatenate([-sin, sin], axis=-1)


def _rope_cast_kernel(flags_ref, x_ref, c_ref, s_ref, o_ref, slab_ref, *, dil):
    j = pl.program_id(2)
    nslab = x_ref.shape[-1] // LANE
    rows = x_ref.shape[0] // dil

    def residue(r):
        return pl.ds(r, rows, stride=dil) if dil > 1 else slice(None)

    def slabs():
        for h in range(nslab):
            sl = slice(h * LANE, (h + 1) * LANE)
            if dil > 1:
                slab_ref[...] = x_ref[:, sl]
                yield sl, slab_ref
            else:
                yield sl, x_ref.at[:, sl]

    @pl.when(flags_ref[j] == 0)
    def _():
        for sl, src in slabs():
            for r in range(dil):
                o_ref[r, :, sl] = src[residue(r), :].astype(o_ref.dtype)

    @pl.when(flags_ref[j] != 0)
    def _():
        scale = jnp.where(flags_ref[j] == 2, QK_SCALE, 1.0).astype(F32)
        for sl, src in slabs():
            for r in range(dil):
                t = src[residue(r), :]
                rot = t * c_ref[residue(r), :] + pltpu.roll(t, HEAD_DIM // 2, 1) * s_ref[residue(r), :]
                o_ref[r, :, sl] = (rot * scale).astype(o_ref.dtype)


def rope_cast(x, cos2, sin2, *, width, first_blk, flags, dil=1, ts=512):
    B, S, _ = x.shape
    nblk = len(flags)
    grid_spec = pltpu.PrefetchScalarGridSpec(
        num_scalar_prefetch=1,
        grid=(B, S // ts, nblk),
        in_specs=[pl.BlockSpec((None, ts, width), lambda b, i, j, f: (b, i, first_blk + j)),
                  pl.BlockSpec((ts, LANE), lambda b, i, j, f: (i, 0)),
                  pl.BlockSpec((ts, LANE), lambda b, i, j, f: (i, 0))],
        out_specs=pl.BlockSpec((None, dil, ts // dil, width), lambda b, i, j, f: (b, 0, i, j)),
        scratch_shapes=[pltpu.VMEM((ts, LANE), F32)],
    )
    return pl.pallas_call(
        functools.partial(_rope_cast_kernel, dil=dil),
        grid_spec=grid_spec,
        out_shape=jax.ShapeDtypeStruct((B, dil, S // dil, nblk * width), BF16),
        compiler_params=_cparams(("parallel", "parallel", "arbitrary")),
        name="rope_cast",
    )(jnp.asarray(flags, jnp.int32), x, cos2, sin2)


def _ones_column(n):
    return jnp.where(lax.broadcasted_iota(jnp.int32, (n, LANE), 1) == 0, 1.0, 0.0).astype(BF16)


def _softmax_pv(s, m, v, ones):
    p = jnp.exp2(s - m).astype(BF16)
    res = jnp.dot(p, jnp.concatenate([v, ones], axis=1), preferred_element_type=F32)
    return res[:, LANE:LANE + 1], res[:, :LANE]


def _banded_kernel(q_ref, k_ref, v_ref, o_ref, *lse_refs, R, nwin, window, tq, qsub, nkeys):
    i = pl.program_id(3)
    L = k_ref.shape[0]
    rows = R * qsub
    rel = (lax.broadcasted_iota(jnp.int32, (rows, nkeys), 0) & (qsub - 1)) \
        - lax.broadcasted_iota(jnp.int32, (rows, nkeys), 1)
    ones = _ones_column(nkeys)
    nsub = tq // qsub

    def scores(j):
        q0 = i * tq + j * qsub
        kstart = pl.multiple_of(jnp.clip(q0 - nwin * LANE, 0, L - nkeys), LANE)
        qj = q_ref[j * qsub:(j + 1) * qsub, :]
        if R > 1:
            q = jnp.concatenate([qj[:, r * LANE:(r + 1) * LANE] for r in range(R)], axis=0)
        else:
            q = qj
        s = lax.dot_general(q, k_ref[pl.ds(kstart, nkeys), :], NT_DIMS,
                            preferred_element_type=F32)
        return q0, kstart, s

    ahead = 2
    pending = [scores(j) for j in range(min(ahead, nsub))]
    for j in range(nsub):
        if j + ahead < nsub:
            pending.append(scores(j + ahead))
        q0, kstart, s = pending[j]
        v = v_ref[pl.ds(kstart, nkeys), :]
        diff = rel + (q0 - kstart)
        valid = lax.bitcast_convert_type(diff, jnp.uint32) <= jnp.uint32(window)
        s = jnp.where(valid, s, NEG_BIG)
        m = jnp.max(s, axis=-1, keepdims=True)
        l, o = _softmax_pv(s, m, v, ones)
        o = o / l
        for r in range(R):
            o_ref[j * qsub:(j + 1) * qsub, r * LANE:(r + 1) * LANE] = o[r * qsub:(r + 1) * qsub]
        if lse_refs:
            lse_refs[0][j * qsub:(j + 1) * qsub, :] = jnp.broadcast_to(
                m + jnp.log(l) * LOG2E, (rows, LANE))


def banded_attention(q_arr, kv_arr, *, n_kv, R, q_blk, k_blk, v_blk, nwin, window,
                     want_lse):
    B, dil, L, _ = q_arr.shape
    tq = min(512, L)
    qsub = LANE
    nkeys = min(qsub + nwin * LANE, L)
    assert L % tq == 0 and q_blk % R == 0
    width = n_kv * R * LANE
    out_shape = [jax.ShapeDtypeStruct((B, dil, L, width), F32)]
    out_specs = [pl.BlockSpec((None, None, tq, R * LANE), lambda b, r, h, i: (b, r, i, h))]
    if want_lse:
        out_shape.append(out_shape[0])
        out_specs.append(out_specs[0])
    return pl.pallas_call(
        functools.partial(_banded_kernel, R=R, nwin=nwin, window=window, tq=tq, qsub=qsub,
                          nkeys=nkeys),
        grid=(B, dil, n_kv, L // tq),
        in_specs=[pl.BlockSpec((None, None, tq, R * LANE),
                               lambda b, r, h, i: (b, r, i, q_blk // R + h)),
                  pl.BlockSpec((None, None, L, LANE), lambda b, r, h, i: (b, r, 0, k_blk + h)),
                  pl.BlockSpec((None, None, L, LANE), lambda b, r, h, i: (b, r, 0, v_blk + h))],
        out_specs=out_specs,
        out_shape=out_shape,
        compiler_params=_cparams(("parallel", "parallel", "parallel", "arbitrary")),
        name="banded_attn",
    )(q_arr, kv_arr, kv_arr)


def _compress_kernel(p_ref, pe_ref, w1_ref, w2_ref, o_ref):
    half = CMP_STRIDE * HEAD_DIM
    pieces = p_ref[...]
    top = (pieces + pe_ref[0:1, :]).astype(BF16)
    bot = (pieces + pe_ref[1:2, :]).astype(BF16)
    a = jnp.dot(top, w1_ref[0:half, :], preferred_element_type=F32)
    b = jnp.dot(bot, w1_ref[half:2 * half, :], preferred_element_type=F32)
    n = a.shape[0]
    hid = a + pltpu.roll(b, n - 1, 0)
    o_ref[...] = jnp.dot(jax.nn.gelu(hid).astype(BF16), w2_ref[...], preferred_element_type=F32)


def compress(pieces, pe, w1, w2):
    _, B, G, NP, W = pieces.shape
    return pl.pallas_call(
        _compress_kernel,
        grid=(2, B, G),
        in_specs=[pl.BlockSpec((None, None, None, NP, W), lambda t, b, g: (t, b, g, 0, 0)),
                  pl.BlockSpec((None, 2, W), lambda t, b, g: (t, 0, 0)),
                  pl.BlockSpec((None, 2 * W, CMP_HIDDEN), lambda t, b, g: (t, 0, 0)),
                  pl.BlockSpec((None, CMP_HIDDEN, HEAD_DIM), lambda t, b, g: (t, 0, 0))],
        out_specs=pl.BlockSpec((None, None, None, NP, HEAD_DIM), lambda t, b, g: (t, b, g, 0, 0)),
        out_shape=jax.ShapeDtypeStruct((2, B, G, NP, HEAD_DIM), F32),
        compiler_params=_cparams(("parallel", "parallel", "parallel")),
        name="nsa_compress",
    )(pieces, pe, w1, w2)


def _cmp_attn_kernel(q_ref, kc_ref, vc_ref, ov_ref, o_ref, imp_ref, *, tq):
    i = pl.program_id(2)
    R = NSA_REP
    qj = q_ref[...]
    q = jnp.concatenate([qj[:, r * LANE:(r + 1) * LANE] for r in range(R)], axis=0)
    q = (q * QK_SCALE).astype(BF16)
    kc = kc_ref[...].astype(BF16)
    s = lax.dot_general(q, kc, NT_DIMS, preferred_element_type=F32)
    t = i * tq + (lax.broadcasted_iota(jnp.int32, s.shape, 0) & (tq - 1))
    cmp_end = lax.broadcasted_iota(jnp.int32, s.shape, 1) * CMP_STRIDE + (CMP_BLOCK - 1)
    valid = cmp_end <= t
    s = jnp.where(valid, s, NEG_BIG)
    m = jnp.max(s, axis=-1, keepdims=True)
    p = jnp.where(valid, jnp.exp2(s - m), 0.0)
    l = jnp.sum(p, axis=-1, keepdims=True)
    p = p / jnp.where(l > 0, l, 1.0)
    o = jnp.dot(p.astype(BF16), vc_ref[...].astype(BF16), preferred_element_type=F32)
    for r in range(R):
        o_ref[:, r * LANE:(r + 1) * LANE] = o[r * tq:(r + 1) * tq]
    psum = p[0:tq]
    for r in range(1, R):
        psum = psum + p[r * tq:(r + 1) * tq]
    p_hi = psum.astype(BF16)
    p_lo = (psum - p_hi.astype(F32)).astype(BF16)
    ov = ov_ref[...]
    imp_ref[...] = (lax.dot_general(ov, p_hi, NT_DIMS, preferred_element_type=F32)
                    + lax.dot_general(ov, p_lo, NT_DIMS, preferred_element_type=F32))


def cmp_attention(proj, kvc, overlap_t, tq=128):
    B, S, _ = proj.shape
    G = NSA_KV_GROUPS
    NP = kvc.shape[3]
    n_sel = overlap_t.shape[0]
    qw = NSA_REP * LANE
    return pl.pallas_call(
        functools.partial(_cmp_attn_kernel, tq=tq),
        grid=(B, G, S // tq),
        in_specs=[pl.BlockSpec((None, tq, qw), lambda b, g, i: (b, i, g)),
                  pl.BlockSpec((None, None, None, NP, LANE), lambda b, g, i: (0, b, g, 0, 0)),
                  pl.BlockSpec((None, None, None, NP, LANE), lambda b, g, i: (1, b, g, 0, 0)),
                  pl.BlockSpec((n_sel, NP), lambda b, g, i: (0, 0))],
        out_specs=[pl.BlockSpec((None, tq, qw), lambda b, g, i: (b, i, g)),
                   pl.BlockSpec((None, None, n_sel, tq), lambda b, g, i: (b, g, 0, i))],
        out_shape=[jax.ShapeDtypeStruct((B, S, NSA_HEADS * LANE), F32),
                   jax.ShapeDtypeStruct((B, G, n_sel, S), F32)],
        compiler_params=_cparams(("parallel", "parallel", "parallel")),
        name="nsa_cmp_attn",
    )(proj, kvc, kvc, overlap_t)


def _topk_kernel(imp_ref, o_ref, *, tq):
    i = pl.program_id(2)
    imp = imp_ref[...]
    n_sel = imp.shape[0]
    blk = lax.broadcasted_iota(jnp.int32, imp.shape, 0)
    t = i * tq + lax.broadcasted_iota(jnp.int32, imp.shape, 1)
    cur = t // SEL_BLOCK
    avail = blk <= cur
    forced = jnp.where(blk == 0, 1.0, jnp.where(blk == cur, 1.0, jnp.where(blk == cur - 1, 1.0, 0.0)))
    score = jnp.where(avail, imp + FORCED_BONUS * forced, -jnp.inf)
    picked = jnp.zeros(imp.shape, F32)
    for _ in range(min(SEL_TOPK, n_sel)):
        mx = jnp.max(score, axis=0, keepdims=True)
        first = jnp.min(jnp.where(score == mx, blk, n_sel), axis=0, keepdims=True)
        hit = blk == first
        picked = jnp.where(hit, 1.0, picked)
        score = jnp.where(hit, -jnp.inf, score)
    feat = jnp.where(avail, jnp.where(picked > 0.0, 0.0, SEL_OFF), SEL_OFF)
    o_ref[...] = feat.T.astype(o_ref.dtype)


def topk_select(imp_t, tq=256):
    B, G, n_sel, S = imp_t.shape
    return pl.pallas_call(
        functools.partial(_topk_kernel, tq=tq),
        grid=(B, G, S // tq),
        in_specs=[pl.BlockSpec((None, None, n_sel, tq), lambda b, g, i: (b, g, 0, i))],
        out_specs=pl.BlockSpec((None, None, tq, n_sel), lambda b, g, i: (b, g, i, 0)),
        out_shape=jax.ShapeDtypeStruct((B, G, S, n_sel), BF16),
        compiler_params=_cparams(("parallel", "parallel", "parallel")),
        name="nsa_topk",
    )(imp_t)


def _sel_attn_kernel(q_ref, mf_ref, k_ref, e_ref, vt_ref, o_ref, *, tq, qsub, tkv):
    i = pl.program_id(2)
    R = NSA_REP
    nsub = tq // qsub
    rows = R * qsub
    q0 = i * tq
    ntile = (q0 + tq - 1) // tkv + 1
    rel = (lax.broadcasted_iota(jnp.int32, (tkv, rows), 1) & (qsub - 1)) \
        - lax.broadcasted_iota(jnp.int32, (tkv, rows), 0)

    qa = []
    for u in range(nsub):
        qu = q_ref[u * qsub:(u + 1) * qsub, :]
        mf = mf_ref[u * qsub:(u + 1) * qsub, :]
        qa.append(jnp.concatenate(
            [jnp.concatenate([qu[:, r * LANE:(r + 1) * LANE], mf], axis=1) for r in range(R)],
            axis=0))

    def body(j, carry, masked):
        ks = pl.multiple_of(j * tkv, tkv)
        ka = jnp.concatenate([k_ref[pl.ds(ks, tkv), :], e_ref[pl.ds(ks, tkv), :]], axis=1)
        vt = vt_ref[:, pl.ds(ks, tkv)]
        out = []
        scores = lambda u: lax.dot_general(ka, qa[u], NT_DIMS, preferred_element_type=F32)
        ahead = 2
        sts = [scores(u) for u in range(min(ahead, nsub))]
        for u in range(nsub):
            m, l, acc = carry[u]
            if u + ahead < nsub:
                sts.append(scores(u + ahead))
            st = sts[u]
            if masked:
                st = jnp.where(rel >= ks - (q0 + u * qsub), st, NEG_BIG)
            m_new = jnp.maximum(m, jnp.max(st, axis=0, keepdims=True))
            a = jnp.exp2(m - m_new)
            res = jnp.dot(vt, jnp.exp2(st - m_new).astype(BF16), preferred_element_type=F32)
            out.append((m_new, a * l + res[LANE:LANE + 1, :], a * acc + res[:LANE, :]))
        return tuple(out)

    init = tuple((jnp.full((1, rows), NEG_BIG, F32), jnp.zeros((1, rows), F32),
                  jnp.zeros((LANE, rows), F32)) for _ in range(nsub))
    carry = lax.fori_loop(0, ntile - 1, lambda j, c: body(j, c, False), init)
    final = body(ntile - 1, carry, True)
    for u in range(nsub):
        _, l, acc = final[u]
        ot = acc / l
        for r in range(R):
            o_ref[u * qsub:(u + 1) * qsub, r * LANE:(r + 1) * LANE] = \
                ot[:, r * qsub:(r + 1) * qsub].T


def sel_attention(q_rot, mfeat, kvb, onehot, *, k_blk, v_blk, tq=512, qsub=128, tkv=1024):
    B, S, _ = q_rot.shape
    G = NSA_KV_GROUPS
    n_sel = onehot.shape[1]
    qw = NSA_REP * LANE
    tkv = min(tkv, S)
    assert tkv % tq == 0 and S % tkv == 0
    v = kvb[:, :, v_blk * LANE:(v_blk + G) * LANE].reshape(B, S, G, LANE).transpose(0, 2, 3, 1)
    pad = jnp.zeros((B, G, 16, S), BF16).at[:, :, 0, :].set(1.0)
    vt = jnp.concatenate([v, pad], axis=2)
    return pl.pallas_call(
        functools.partial(_sel_attn_kernel, tq=tq, qsub=qsub, tkv=tkv),
        grid=(B, G, S // tq),
        in_specs=[pl.BlockSpec((None, tq, qw), lambda b, g, i: (b, i, g)),
                  pl.BlockSpec((None, None, tq, n_sel), lambda b, g, i: (b, g, i, 0)),
                  pl.BlockSpec((None, S, LANE), lambda b, g, i: (b, 0, k_blk + g)),
                  pl.BlockSpec((S, n_sel), lambda b, g, i: (0, 0)),
                  pl.BlockSpec((None, None, LANE + 16, S), lambda b, g, i: (b, g, 0, 0))],
        out_specs=pl.BlockSpec((None, tq, qw), lambda b, g, i: (b, i, g)),
        out_shape=jax.ShapeDtypeStruct((B, S, NSA_HEADS * LANE), F32),
        compiler_params=_cparams(("parallel", "parallel", "arbitrary")),
        name="nsa_sel_attn",
    )(q_rot, mfeat, kvb, onehot, vt)


def _nsa_combine_kernel(g_ref, oc_ref, os_ref, ow_ref, o_ref):
    gates = jax.nn.sigmoid(g_ref[...])
    for h in range(NSA_HEADS):
        sl = slice(h * LANE, (h + 1) * LANE)
        acc = gates[:, 3 * h:3 * h + 1] * oc_ref[:, sl]
        acc = acc + gates[:, 3 * h + 1:3 * h + 2] * os_ref[:, sl]
        acc = acc + gates[:, 3 * h + 2:3 * h + 3] * ow_ref[:, sl]
        o_ref[:, sl] = acc.astype(o_ref.dtype)


def nsa_combine(gate_logits, o_cmp, o_sel, o_win, ts=256):
    B, S, W = o_cmp.shape
    spec = pl.BlockSpec((None, ts, W), lambda b, i: (b, i, 0))
    return pl.pallas_call(
        _nsa_combine_kernel,
        grid=(B, S // ts),
        in_specs=[pl.BlockSpec((None, ts, LANE), lambda b, i: (b, i, 0)), spec, spec, spec],
        out_specs=spec,
        out_shape=jax.ShapeDtypeStruct((B, S, W), BF16),
        compiler_params=_cparams(("parallel", "parallel")),
        name="nsa_combine",
    )(gate_logits, o_cmp, o_sel, o_win)


def _dil_combine_kernel(*refs, dils):
    n = len(dils)
    o_refs, lse_refs, out_ref = refs[:n], refs[n:2 * n], refs[2 * n]
    nat_o, nat_l = refs[2 * n + 1:3 * n + 1], refs[3 * n + 1:]
    ts = out_ref.shape[0]
    for h in range(out_ref.shape[1] // LANE):
        sl = slice(h * LANE, (h + 1) * LANE)
        for g, dil in enumerate(dils):
            for r in range(dil):
                rows = pl.ds(r, ts // dil, stride=dil) if dil > 1 else slice(None)
                nat_o[g][rows, :] = o_refs[g][r, :, sl]
                nat_l[g][rows, :] = lse_refs[g][r, :, sl]
        lses = [r[...] for r in nat_l]
        mx = functools.reduce(jnp.maximum, lses)
        ws = [jnp.exp2(l - mx) for l in lses]
        den = functools.reduce(lambda a, b: a + b, ws)
        num = functools.reduce(lambda a, b: a + b, [w * r[...] for w, r in zip(ws, nat_o)])
        out_ref[:, sl] = (num / den).astype(out_ref.dtype)


def dil_combine(outs, lses, ts=128):
    dils = tuple(o.shape[1] for o in outs)
    B, W = outs[0].shape[0], outs[0].shape[-1]
    S = outs[0].shape[1] * outs[0].shape[2]
    specs = [pl.BlockSpec((None, d, ts // d, W), lambda b, i: (b, 0, i, 0)) for d in dils]
    return pl.pallas_call(
        functools.partial(_dil_combine_kernel, dils=dils),
        grid=(B, S // ts),
        in_specs=specs + specs,
        out_specs=pl.BlockSpec((None, ts, W), lambda b, i: (b, i, 0)),
        out_shape=jax.ShapeDtypeStruct((B, S, W), BF16),
        scratch_shapes=[pltpu.VMEM((ts, LANE), F32)] * (2 * len(dils)),
        compiler_params=_cparams(("parallel", "parallel")),
        name="dil_combine",
    )(*outs, *lses)


def _softplus(x):
    return jnp.maximum(x, 0.0) + jnp.log1p(jnp.exp(-jnp.abs(x)))


def _gate_band_starts():
    ntile = D_RNN // LANE
    starts = []
    for j in range(ntile):
        n_lo = (j * LANE) // RNN_BLOCK_DIM
        n_hi = (j * LANE + LANE - 1) // RNN_BLOCK_DIM
        lo = (n_lo * RNN_BLOCK_DIM) // LANE
        hi = -(-((n_hi + 1) * RNN_BLOCK_DIM) // LANE)
        assert hi - lo <= 4
        starts.append(min(lo, ntile - 4))
    return starts


def _rglru_kernel(y_ref, xr_ref, cw_ref, cb_ref, wg_ref, bg_ref, lam_ref, o_ref,
                  h_ref, tail_ref, *, ts, starts):
    i = pl.program_id(1)

    @pl.when(i == 0)
    def _():
        h_ref[...] = jnp.zeros_like(h_ref)
        tail_ref[...] = jnp.zeros_like(tail_ref)

    xr = xr_ref[...]
    ext = jnp.concatenate([tail_ref[...], xr], axis=0)
    x = cb_ref[...] + xr * cw_ref[CONV_WIDTH - 1:CONV_WIDTH, :]
    for d in range(1, CONV_WIDTH):
        shifted = pltpu.roll(ext, d, 0)[8:8 + ts]
        x = x + shifted * cw_ref[CONV_WIDTH - 1 - d:CONV_WIDTH - d, :]
    tail_ref[...] = xr[ts - 8:ts]

    xb = x.astype(BF16)
    gl = []
    for g in range(2):
        tiles = [jnp.dot(xb[:, a * LANE:(a + 4) * LANE], wg_ref[g, j],
                         preferred_element_type=F32) for j, a in enumerate(starts)]
        gl.append(jnp.concatenate(tiles, axis=1) + bg_ref[g:g + 1, :])
    r = jax.nn.sigmoid(gl[0])
    ig = jax.nn.sigmoid(gl[1])
    log_a = (-LRU_C) * r * _softplus(-lam_ref[...])
    a = jnp.exp(log_a)
    b = jnp.sqrt(-jnp.tanh(log_a) * (a * a + 1.0)) * (ig * x)

    row = lax.broadcasted_iota(jnp.int32, a.shape, 0)
    d = 1
    while d < ts:
        keep = row >= d
        b = b + a * jnp.where(keep, pltpu.roll(b, d, 0), 0.0)
        a = a * jnp.where(keep, pltpu.roll(a, d, 0), 1.0)
        d *= 2
    h = a * h_ref[0:1, :] + b
    h_ref[0:1, :] = h[ts - 1:ts, :]
    o_ref[...] = (h * jax.nn.gelu(y_ref[...])).astype(o_ref.dtype)


def rglru_scan(proj, conv_w, conv_b, wband, b_gate, lam, ts=128):
    B, S, _ = proj.shape
    C = D_RNN
    starts = _gate_band_starts()
    vec = lambda n: pl.BlockSpec((n, C), lambda b, i: (0, 0))
    return pl.pallas_call(
        functools.partial(_rglru_kernel, ts=ts, starts=starts),
        grid=(B, S // ts),
        in_specs=[pl.BlockSpec((None, ts, C), lambda b, i: (b, i, 0)),
                  pl.BlockSpec((None, ts, C), lambda b, i: (b, i, 1)),
                  vec(CONV_WIDTH), vec(1),
                  pl.BlockSpec(wband.shape, lambda b, i: (0, 0, 0, 0)),
                  vec(2), vec(1)],
        out_specs=pl.BlockSpec((None, ts, C), lambda b, i: (b, i, 0)),
        out_shape=jax.ShapeDtypeStruct((B, S, C), BF16),
        scratch_shapes=[pltpu.VMEM((8, C), F32), pltpu.VMEM((8, C), F32)],
        compiler_params=_cparams(("arbitrary", "arbitrary")),
        name="rglru_scan",
    )(proj, proj, conv_w, conv_b.reshape(1, C), wband, b_gate, lam.reshape(1, C))


def _gate_band_weights(w_gate):
    starts = _gate_band_starts()
    dense = jnp.stack([jax.scipy.linalg.block_diag(*[w_gate[g, n] for n in range(RNN_BLOCKS)])
                       for g in range(2)])
    tiles = [dense[:, a * LANE:(a + 4) * LANE, j * LANE:(j + 1) * LANE]
             for j, a in enumerate(starts)]
    return jnp.stack(tiles, axis=1).astype(BF16)


def _row_tile(T):
    return 512 if T % 512 == 0 else T


def nsa_mixer(hn, x, mod, gate_blk, w_in, cmp_pe, cmp_w1, cmp_w2, w_out, cos2, sin2):
    B, S, D = x.shape
    T = B * S
    G = NSA_KV_GROUPS
    hq = NSA_HEADS * HEAD_DIM
    main = hq + 6 * NSA_KV_WIDTH
    tm = _row_tile(T)
    hn2 = hn.reshape(T, D)
    proj = matmul(hn2, w_in[:, :main].astype(BF16), tm=tm, tn=main // 2, tk=D).reshape(B, S, main)
    w_gate = jnp.pad(w_in[:, main:], ((0, 0), (0, LANE - 3 * NSA_HEADS))).astype(BF16)
    gate_logits = matmul(hn2, w_gate, tm=tm, tn=LANE, tk=D).reshape(B, S, LANE)

    q_rot = rope_cast(proj, cos2, sin2, width=hq, first_blk=0, flags=[2])
    kvb = rope_cast(proj, cos2, sin2, width=NSA_KV_WIDTH, first_blk=hq // NSA_KV_WIDTH + 2,
                    flags=[1, 0, 1, 0])

    npiece = S // CMP_STRIDE
    kv_cmp = proj[:, :, hq:hq + 2 * NSA_KV_WIDTH].reshape(B, npiece, CMP_STRIDE, 2, G, HEAD_DIM)
    pieces = kv_cmp.transpose(3, 0, 4, 1, 2, 5).reshape(2, B, G, npiece, CMP_STRIDE * HEAD_DIM)
    kvc = compress(pieces, cmp_pe.reshape(2, 2, CMP_STRIDE * HEAD_DIM),
                   cmp_w1.astype(BF16), cmp_w2.astype(BF16))

    n_sb = S // SEL_BLOCK
    cmp_start = np.arange(npiece) * CMP_STRIDE
    sel_start = np.arange(n_sb) * SEL_BLOCK
    overlap = ((cmp_start[:, None] < sel_start[None, :] + SEL_BLOCK)
               & (cmp_start[:, None] + CMP_BLOCK > sel_start[None, :]))
    overlap[npiece - 1] = False
    overlap_t = jnp.asarray(overlap.T, BF16)
    onehot = jnp.asarray(np.arange(S)[:, None] // SEL_BLOCK == np.arange(n_sb)[None, :], BF16)

    o_cmp, imp_t = cmp_attention(proj, kvc, overlap_t)
    mfeat = topk_select(imp_t)
    kb = NSA_KV_WIDTH // LANE
    o_sel = sel_attention(q_rot.reshape(B, S, hq), mfeat, kvb.reshape(B, S, 4 * NSA_KV_WIDTH),
                          onehot, k_blk=0, v_blk=kb)
    (o_win,) = banded_attention(q_rot, kvb, n_kv=G, R=NSA_REP, q_blk=0, k_blk=2 * kb,
                                v_blk=3 * kb, nwin=(NSA_WINDOW - 1 + LANE - 1) // LANE,
                                window=NSA_WINDOW - 1, want_lse=False)
    o = nsa_combine(gate_logits, o_cmp, o_sel, o_win.reshape(B, S, hq))
    return matmul(o.reshape(T, hq), w_out.astype(BF16), tm=tm, tn=D, tk=hq,
                  epilogue="residual", res=x.reshape(T, D), mod=mod, gate_blk=gate_blk,
                  rows_per_batch=S).reshape(B, S, D)


def dilated_mixer(hn, x, mod, gate_blk, w_in, w_out, cos2, sin2):
    B, S, D = x.shape
    T = B * S
    H = DIL_HEADS
    hw = H * HEAD_DIM
    tm = _row_tile(T)
    proj = matmul(hn.reshape(T, D), w_in.astype(BF16), tm=tm, tn=hw, tk=D)
    proj = proj.reshape(B, S, w_in.shape[1])
    outs, lses = [], []
    for g, (window, dil) in enumerate(DIL_PATTERNS):
        w = window // dil
        qkv = rope_cast(proj, cos2, sin2, width=hw, first_blk=3 * g, flags=[2, 1, 0], dil=dil)
        o, lse = banded_attention(qkv, qkv, n_kv=H, R=1, q_blk=0, k_blk=H, v_blk=2 * H,
                                  nwin=(w + LANE - 1) // LANE, window=w, want_lse=True)
        outs.append(o)
        lses.append(lse)
    o = dil_combine(outs, lses)
    return matmul(o.reshape(T, hw), w_out.astype(BF16), tm=tm, tn=D, tk=hw,
                  epilogue="residual", res=x.reshape(T, D), mod=mod, gate_blk=gate_blk,
                  rows_per_batch=S).reshape(B, S, D)


def rglru_mixer(hn, x, mod, gate_blk, w_in, conv_w, conv_b, w_gate, b_gate, lam, w_out):
    B, S, D = x.shape
    T = B * S
    tm = _row_tile(T)
    proj = matmul(hn.reshape(T, D), w_in.astype(BF16), tm=tm, tn=D_RNN, tk=D)
    hy = rglru_scan(proj.reshape(B, S, 2 * D_RNN), conv_w, conv_b, _gate_band_weights(w_gate),
                    b_gate, lam)
    return matmul(hy.reshape(T, D_RNN), w_out.astype(BF16), tm=tm, tn=D, tk=D_RNN,
                  epilogue="residual", res=x.reshape(T, D), mod=mod, gate_blk=gate_blk,
                  rows_per_batch=S).reshape(B, S, D)


def mlp(hn, x, mod, gate_blk, w1, w2):
    B, S, D = x.shape
    T = B * S
    tm = _row_tile(T)
    dff = w1.shape[1]
    h = matmul(hn.reshape(T, D), w1.astype(BF16), tm=tm, tn=1024, tk=D, out_dtype=BF16,
               epilogue="relu2")
    return matmul(h, w2.astype(BF16), tm=tm, tn=D, tk=2048, epilogue="residual",
                  res=x.reshape(T, D), mod=mod, gate_blk=gate_blk,
                  rows_per_batch=S).reshape(B, S, D)


def kernel(x, c, l0_w_ada, l0_b_ada, l0_norm1, l0_w_in, l0_cmp_pe, l0_cmp_w1, l0_cmp_w2, l0_w_out, l0_norm2, l0_w_ff1, l0_w_ff2, l1_w_ada, l1_b_ada, l1_norm1, l1_w_in, l1_w_out, l1_norm2, l1_w_ff1, l1_w_ff2, l2_w_ada, l2_b_ada, l2_norm1, l2_w_in, l2_conv_w, l2_conv_b, l2_w_gate, l2_b_gate, l2_lambda, l2_w_out, l2_norm2, l2_w_ff1, l2_w_ff2, l3_w_ada, l3_b_ada, l3_norm1, l3_w_in, l3_cmp_pe, l3_cmp_w1, l3_cmp_w2, l3_w_out, l3_norm2, l3_w_ff1, l3_w_ff2, norm_f):
    B, S, D = x.shape
    cos2, sin2 = rope_tables(S)
    layers = (
        (l0_w_ada, l0_b_ada, l0_norm1, l0_norm2, l0_w_ff1, l0_w_ff2,
         (l0_w_in, l0_cmp_pe, l0_cmp_w1, l0_cmp_w2, l0_w_out)),
        (l1_w_ada, l1_b_ada, l1_norm1, l1_norm2, l1_w_ff1, l1_w_ff2, (l1_w_in, l1_w_out)),
        (l2_w_ada, l2_b_ada, l2_norm1, l2_norm2, l2_w_ff1, l2_w_ff2,
         (l2_w_in, l2_conv_w, l2_conv_b, l2_w_gate, l2_b_gate, l2_lambda, l2_w_out)),
        (l3_w_ada, l3_b_ada, l3_norm1, l3_norm2, l3_w_ff1, l3_w_ff2,
         (l3_w_in, l3_cmp_pe, l3_cmp_w1, l3_cmp_w2, l3_w_out)),
    )
    for li in range(DEPTH):
        w_ada, b_ada, n1, n2, ff1, ff2, mix = layers[li]
        mod = adaln(c, w_ada, b_ada).reshape(B, 1, 6 * D)
        hn = modulate(x, n1, mod, 0, 1)
        kind = li % N_MIXERS
        if kind == 0:
            x = nsa_mixer(hn, x, mod, 2, *mix, cos2, sin2)
        elif kind == 1:
            x = dilated_mixer(hn, x, mod, 2, *mix, cos2, sin2)
        else:
            x = rglru_mixer(hn, x, mod, 2, *mix)
        hn = modulate(x, n2, mod, 3, 4)
        x = mlp(hn, x, mod, 5, ff1, ff2)
    return rmsnorm(x, norm_f)
```

```python
import functools
import math

import jax
import jax.numpy as jnp
import numpy as np
from jax import lax
from jax.experimental import pallas as pl
from jax.experimental.pallas import tpu as pltpu

F32 = jnp.float32
BF16 = jnp.bfloat16

D_MODEL = 2048
DEPTH = 4
N_MIXERS = 3
HEAD_DIM = 128
ROPE_THETA = 10000.0
NORM_EPS = 1e-6

NSA_HEADS = D_MODEL // HEAD_DIM
NSA_KV_GROUPS = 4
NSA_REP = NSA_HEADS // NSA_KV_GROUPS
NSA_KV_WIDTH = NSA_KV_GROUPS * HEAD_DIM
CMP_BLOCK = 32
CMP_STRIDE = 16
CMP_HIDDEN = 4 * HEAD_DIM
SEL_BLOCK = 64
SEL_TOPK = 16
NSA_WINDOW = 512
FORCED_BONUS = 1e9

DIL_HEADS = D_MODEL // HEAD_DIM
DIL_PATTERNS = ((128, 1), (512, 4), (2048, 16))

D_RNN = 2688
RNN_BLOCKS = 16
RNN_BLOCK_DIM = D_RNN // RNN_BLOCKS
CONV_WIDTH = 4
LRU_C = 8.0

LANE = 128
LOG2E = math.log2(math.e)
QK_SCALE = LOG2E / math.sqrt(HEAD_DIM)
NEG_BIG = -1e30
SEL_OFF = -float(2 ** 30)
VMEM_LIMIT = 56 * 1024 * 1024

NT_DIMS = (((1,), (1,)), ((), ()))


def _cparams(sem):
    return pltpu.CompilerParams(dimension_semantics=sem, vmem_limit_bytes=VMEM_LIMIT)


def _adaln_kernel(c_ref, w_ref, b_ref, o_ref):
    w = w_ref[...]
    for b in range(c_ref.shape[0]):
        c = c_ref[b]
        cond = c * jax.nn.sigmoid(c)
        o_ref[b:b + 1, :] = jnp.sum(w * cond, axis=0, keepdims=True) + b_ref[...]


def adaln(c, w_ada, b_ada, tn=1024):
    B, D = c.shape
    N = w_ada.shape[1]
    return pl.pallas_call(
        _adaln_kernel,
        grid=(N // tn,),
        in_specs=[pl.BlockSpec((B, D, 1), lambda j: (0, 0, 0)),
                  pl.BlockSpec((D, tn), lambda j: (0, j)),
                  pl.BlockSpec((1, tn), lambda j: (0, j))],
        out_specs=pl.BlockSpec((B, tn), lambda j: (0, j)),
        out_shape=jax.ShapeDtypeStruct((B, N), F32),
        compiler_params=_cparams(("parallel",)),
        name="adaln",
    )(c.reshape(B, D, 1), w_ada, b_ada.reshape(1, N))


def _modulate_kernel(x_ref, gain_ref, sh_ref, sc_ref, o_ref):
    x = x_ref[...]
    ms = jnp.mean(x * x, axis=-1, keepdims=True)
    y = x * lax.rsqrt(ms + NORM_EPS) * gain_ref[...]
    o_ref[...] = (y * (1.0 + sc_ref[...]) + sh_ref[...]).astype(o_ref.dtype)


def modulate(x, gain, mod, shift_blk, scale_blk, ts=512):
    B, S, D = x.shape
    return pl.pallas_call(
        _modulate_kernel,
        grid=(B, S // ts),
        in_specs=[pl.BlockSpec((None, ts, D), lambda b, i: (b, i, 0)),
                  pl.BlockSpec((1, D), lambda b, i: (0, 0)),
                  pl.BlockSpec((None, 1, D), lambda b, i: (b, 0, shift_blk)),
                  pl.BlockSpec((None, 1, D), lambda b, i: (b, 0, scale_blk))],
        out_specs=pl.BlockSpec((None, ts, D), lambda b, i: (b, i, 0)),
        out_shape=jax.ShapeDtypeStruct((B, S, D), BF16),
        compiler_params=_cparams(("parallel", "parallel")),
        name="modulate",
    )(x, gain.reshape(1, D), mod, mod)


def _rmsnorm_kernel(x_ref, gain_ref, o_ref):
    x = x_ref[...]
    ms = jnp.mean(x * x, axis=-1, keepdims=True)
    o_ref[...] = x * lax.rsqrt(ms + NORM_EPS) * gain_ref[...]


def rmsnorm(x, gain, ts=512):
    B, S, D = x.shape
    return pl.pallas_call(
        _rmsnorm_kernel,
        grid=(B, S // ts),
        in_specs=[pl.BlockSpec((None, ts, D), lambda b, i: (b, i, 0)),
                  pl.BlockSpec((1, D), lambda b, i: (0, 0))],
        out_specs=pl.BlockSpec((None, ts, D), lambda b, i: (b, i, 0)),
        out_shape=jax.ShapeDtypeStruct((B, S, D), F32),
        compiler_params=_cparams(("parallel", "parallel")),
        name="rmsnorm",
    )(x, gain.reshape(1, D))


def _mm_kernel(*refs, nk, epilogue):
    if epilogue == "residual":
        a_ref, w_ref, res_ref, gate_ref, o_ref = refs[:5]
        rest = refs[5:]
    else:
        a_ref, w_ref, o_ref = refs[:3]
        rest = refs[3:]

    def finish(acc):
        if epilogue == "relu2":
            r = jnp.maximum(acc, 0.0)
            acc = r * r
        elif epilogue == "residual":
            acc = res_ref[...] + gate_ref[...] * acc
        o_ref[...] = acc.astype(o_ref.dtype)

    def part():
        return jnp.dot(a_ref[...], w_ref[...], preferred_element_type=F32)

    if nk == 1:
        finish(part())
        return
    acc_ref = rest[0]
    k = pl.program_id(2)

    @pl.when(k == 0)
    def _():
        acc_ref[...] = part()

    if nk > 2:
        @pl.when(jnp.logical_and(k > 0, k < nk - 1))
        def _():
            acc_ref[...] += part()

    @pl.when(k == nk - 1)
    def _():
        finish(acc_ref[...] + part())


def matmul(a, w, *, tm, tn, tk, out_dtype=F32, epilogue="none", res=None, mod=None,
           gate_blk=0, rows_per_batch=None):
    M, K = a.shape
    N = w.shape[1]
    nk = K // tk
    assert M % tm == 0 and N % tn == 0 and K % tk == 0
    in_specs = [pl.BlockSpec((tm, tk), lambda i, j, k: (i, k)),
                pl.BlockSpec((tk, tn), lambda i, j, k: (k, j))]
    args = [a, w]
    if epilogue == "residual":
        assert rows_per_batch % tm == 0
        gpb = D_MODEL // tn
        in_specs += [pl.BlockSpec((tm, tn), lambda i, j, k: (i, j)),
                     pl.BlockSpec((None, 1, tn),
                                  lambda i, j, k: (i * tm // rows_per_batch, 0, gate_blk * gpb + j))]
        args += [res, mod]
    scratch = [pltpu.VMEM((tm, tn), F32)] if nk > 1 else []
    return pl.pallas_call(
        functools.partial(_mm_kernel, nk=nk, epilogue=epilogue),
        grid=(M // tm, N // tn, nk),
        in_specs=in_specs,
        out_specs=pl.BlockSpec((tm, tn), lambda i, j, k: (i, j)),
        out_shape=jax.ShapeDtypeStruct((M, N), out_dtype),
        scratch_shapes=scratch,
        compiler_params=_cparams(("parallel", "parallel", "arbitrary")),
        name="mm_" + epilogue,
    )(*args)


def rope_tables(S):
    inv_freq = ROPE_THETA ** (-jnp.arange(0, HEAD_DIM, 2, dtype=F32) / HEAD_DIM)
    ang = jnp.arange(S, dtype=F32)[:, None] * inv_freq[None, :]
    cos, sin = jnp.cos(ang), jnp.sin(ang)
    return jnp.concatenate([cos, cos], axis=-1), jnp.concatenate([-sin, sin], axis=-1)


CAST, ROPE, ROPE_Q, SCALE_Q = 0, 1, 2, 3


def _proj_heads_kernel(a_ref, w_ref, c_ref, s_ref, *o_refs, kinds):
    acc = jnp.dot(a_ref[...], w_ref[...], preferred_element_type=F32)
    c = c_ref[...]
    s = s_ref[...]
    for o_ref, head_kinds in zip(o_refs, kinds):
        for h, kind in enumerate(head_kinds):
            sl = slice(h * LANE, (h + 1) * LANE)
            t = acc[:, sl]
            if kind in (ROPE, ROPE_Q):
                t = t * c + pltpu.roll(t, HEAD_DIM // 2, 1) * s
            if kind in (ROPE_Q, SCALE_Q):
                t = t * QK_SCALE
            o_ref[:, sl] = t.astype(o_ref.dtype)


def proj_heads(a, w, cos2, sin2, kinds, *, tm):
    M, K = a.shape
    N = w.shape[1]
    nt = cos2.shape[0] // tm
    tab = pl.BlockSpec((tm, LANE), lambda i: (i % nt, 0))
    out_spec = pl.BlockSpec((tm, N), lambda i: (i, 0))
    return pl.pallas_call(
        functools.partial(_proj_heads_kernel, kinds=tuple(kinds)),
        grid=(M // tm,),
        in_specs=[pl.BlockSpec((tm, K), lambda i: (i, 0)), pl.BlockSpec((K, N), lambda i: (0, 0)),
                  tab, tab],
        out_specs=[out_spec] * len(kinds),
        out_shape=[jax.ShapeDtypeStruct((M, N), BF16)] * len(kinds),
        compiler_params=_cparams(("parallel",)),
        name="proj_heads",
    )(a, w, cos2, sin2)


def _ones_column(n):
    return jnp.where(lax.broadcasted_iota(jnp.int32, (n, LANE), 1) == 0, 1.0, 0.0).astype(BF16)


def _softmax_pv(s, m, v, ones):
    p = jnp.exp2(s - m).astype(BF16)
    res = jnp.dot(p, jnp.concatenate([v, ones], axis=1), preferred_element_type=F32)
    return res[:, LANE:LANE + 1], res[:, :LANE]


def _banded_kernel(q_ref, k_ref, v_ref, o_ref, *lse_refs, R, nwin, window, tq, qsub, nkeys):
    i = pl.program_id(3)
    L = k_ref.shape[0]
    rows = R * qsub
    rel = (lax.broadcasted_iota(jnp.int32, (rows, nkeys), 0) & (qsub - 1)) \
        - lax.broadcasted_iota(jnp.int32, (rows, nkeys), 1)
    ones = _ones_column(nkeys)
    nsub = tq // qsub

    def scores(j):
        q0 = i * tq + j * qsub
        kstart = pl.multiple_of(jnp.clip(q0 - nwin * LANE, 0, L - nkeys), LANE)
        qj = q_ref[j * qsub:(j + 1) * qsub, :]
        if R > 1:
            q = jnp.concatenate([qj[:, r * LANE:(r + 1) * LANE] for r in range(R)], axis=0)
        else:
            q = qj
        s = lax.dot_general(q, k_ref[pl.ds(kstart, nkeys), :], NT_DIMS,
                            preferred_element_type=F32)
        return q0, kstart, s

    ahead = 2
    pending = [scores(j) for j in range(min(ahead, nsub))]
    for j in range(nsub):
        if j + ahead < nsub:
            pending.append(scores(j + ahead))
        q0, kstart, s = pending[j]
        v = v_ref[pl.ds(kstart, nkeys), :]
        diff = rel + (q0 - kstart)
        valid = lax.bitcast_convert_type(diff, jnp.uint32) <= jnp.uint32(window)
        s = jnp.where(valid, s, NEG_BIG)
        m = jnp.max(s, axis=-1, keepdims=True)
        l, o = _softmax_pv(s, m, v, ones)
        o = o / l
        for r in range(R):
            o_ref[j * qsub:(j + 1) * qsub, r * LANE:(r + 1) * LANE] = o[r * qsub:(r + 1) * qsub]
        if lse_refs:
            lse_refs[0][j * qsub:(j + 1) * qsub, :] = jnp.broadcast_to(
                m + jnp.log(l) * LOG2E, (rows, LANE))


def banded_attention(q_arr, k_arr, v_arr, *, n_kv, R, nwin, window, want_lse, q_blk=0, k_blk=0,
                     v_blk=0):
    B, dil, L, _ = q_arr.shape
    tq = min(512, L)
    qsub = LANE
    nkeys = min(qsub + nwin * LANE, L)
    assert L % tq == 0 and q_blk % R == 0
    width = n_kv * R * LANE
    out_shape = [jax.ShapeDtypeStruct((B, dil, L, width), F32)]
    out_specs = [pl.BlockSpec((None, None, tq, R * LANE), lambda b, r, h, i: (b, r, i, h))]
    if want_lse:
        out_shape.append(out_shape[0])
        out_specs.append(out_specs[0])
    return pl.pallas_call(
        functools.partial(_banded_kernel, R=R, nwin=nwin, window=window, tq=tq, qsub=qsub,
                          nkeys=nkeys),
        grid=(B, dil, n_kv, L // tq),
        in_specs=[pl.BlockSpec((None, None, tq, R * LANE),
                               lambda b, r, h, i: (b, r, i, q_blk // R + h)),
                  pl.BlockSpec((None, None, L, LANE), lambda b, r, h, i: (b, r, 0, k_blk + h)),
                  pl.BlockSpec((None, None, L, LANE), lambda b, r, h, i: (b, r, 0, v_blk + h))],
        out_specs=out_specs,
        out_shape=out_shape,
        compiler_params=_cparams(("parallel", "parallel", "parallel", "arbitrary")),
        name="banded_attn",
    )(q_arr, k_arr, v_arr)


def _compress_kernel(p_ref, pe_ref, w1_ref, w2_ref, o_ref):
    half = CMP_STRIDE * HEAD_DIM
    pieces = p_ref[...]
    top = (pieces + pe_ref[0:1, :]).astype(BF16)
    bot = (pieces + pe_ref[1:2, :]).astype(BF16)
    a = jnp.dot(top, w1_ref[0:half, :], preferred_element_type=F32)
    b = jnp.dot(bot, w1_ref[half:2 * half, :], preferred_element_type=F32)
    n = a.shape[0]
    hid = a + pltpu.roll(b, n - 1, 0)
    o_ref[...] = jnp.dot(jax.nn.gelu(hid).astype(BF16), w2_ref[...], preferred_element_type=F32)


def compress(pieces, pe, w1, w2):
    _, B, G, NP, W = pieces.shape
    return pl.pallas_call(
        _compress_kernel,
        grid=(2, B, G),
        in_specs=[pl.BlockSpec((None, None, None, NP, W), lambda t, b, g: (t, b, g, 0, 0)),
                  pl.BlockSpec((None, 2, W), lambda t, b, g: (t, 0, 0)),
                  pl.BlockSpec((None, 2 * W, CMP_HIDDEN), lambda t, b, g: (t, 0, 0)),
                  pl.BlockSpec((None, CMP_HIDDEN, HEAD_DIM), lambda t, b, g: (t, 0, 0))],
        out_specs=pl.BlockSpec((None, None, None, NP, HEAD_DIM), lambda t, b, g: (t, b, g, 0, 0)),
        out_shape=jax.ShapeDtypeStruct((2, B, G, NP, HEAD_DIM), F32),
        compiler_params=_cparams(("parallel", "parallel", "parallel")),
        name="nsa_compress",
    )(pieces, pe, w1, w2)


def _cmp_attn_kernel(q_ref, kc_ref, vc_ref, ov_ref, o_ref, imp_ref, *, tq, qsub):
    i = pl.program_id(2)
    R = NSA_REP
    nsub = tq // qsub
    rows = R * qsub
    n_var = kc_ref.shape[0] // LANE

    def run(nk):
        kc = kc_ref[0:nk, :].astype(BF16)
        vc = vc_ref[0:nk, :].astype(BF16)
        ov = ov_ref[:, 0:nk]
        tloc = lax.broadcasted_iota(jnp.int32, (rows, nk), 0) & (qsub - 1)
        cmp_end = lax.broadcasted_iota(jnp.int32, (rows, nk), 1) * CMP_STRIDE + (CMP_BLOCK - 1)

        def scores(u):
            qu = q_ref[u * qsub:(u + 1) * qsub, :]
            q = jnp.concatenate([qu[:, r * LANE:(r + 1) * LANE] for r in range(R)], axis=0)
            return lax.dot_general(q, kc, NT_DIMS, preferred_element_type=F32)

        ahead = 2
        pending = [scores(u) for u in range(min(ahead, nsub))]
        for u in range(nsub):
            if u + ahead < nsub:
                pending.append(scores(u + ahead))
            valid = cmp_end <= tloc + (i * tq + u * qsub)
            s = jnp.where(valid, pending[u], NEG_BIG)
            m = jnp.max(s, axis=-1, keepdims=True)
            p = jnp.where(valid, jnp.exp2(s - m), 0.0)
            l = jnp.sum(p, axis=-1, keepdims=True)
            p = p / jnp.where(l > 0, l, 1.0)
            o = jnp.dot(p.astype(BF16), vc, preferred_element_type=F32)
            for r in range(R):
                o_ref[u * qsub:(u + 1) * qsub, r * LANE:(r + 1) * LANE] = \
                    o[r * qsub:(r + 1) * qsub]
            psum = p[0:qsub]
            for r in range(1, R):
                psum = psum + p[r * qsub:(r + 1) * qsub]
            p_hi = psum.astype(BF16)
            p_lo = (psum - p_hi.astype(F32)).astype(BF16)
            imp_ref[:, u * qsub:(u + 1) * qsub] = (
                lax.dot_general(ov, p_hi, NT_DIMS, preferred_element_type=F32)
                + lax.dot_general(ov, p_lo, NT_DIMS, preferred_element_type=F32))

    need = ((i + 1) * tq - CMP_BLOCK) // CMP_STRIDE + 1
    var = jnp.clip((need - 1) // LANE, 0, n_var - 1)
    for v in range(n_var):
        pl.when(var == v)(functools.partial(run, (v + 1) * LANE))


def cmp_attention(q, kvc, overlap_t, tq=512, qsub=128):
    B, S, _ = q.shape
    G = NSA_KV_GROUPS
    NP = kvc.shape[3]
    n_sel = overlap_t.shape[0]
    qw = NSA_REP * LANE
    return pl.pallas_call(
        functools.partial(_cmp_attn_kernel, tq=tq, qsub=qsub),
        grid=(B, G, S // tq),
        in_specs=[pl.BlockSpec((None, tq, qw), lambda b, g, i: (b, i, g)),
                  pl.BlockSpec((None, None, None, NP, LANE), lambda b, g, i: (0, b, g, 0, 0)),
                  pl.BlockSpec((None, None, None, NP, LANE), lambda b, g, i: (1, b, g, 0, 0)),
                  pl.BlockSpec((n_sel, NP), lambda b, g, i: (0, 0))],
        out_specs=[pl.BlockSpec((None, tq, qw), lambda b, g, i: (b, i, g)),
                   pl.BlockSpec((None, None, n_sel, tq), lambda b, g, i: (b, g, 0, i))],
        out_shape=[jax.ShapeDtypeStruct((B, S, NSA_HEADS * LANE), F32),
                   jax.ShapeDtypeStruct((B, G, n_sel, S), F32)],
        compiler_params=_cparams(("parallel", "parallel", "parallel")),
        name="nsa_cmp_attn",
    )(q, kvc, kvc, overlap_t)


def _topk_kernel(imp_ref, o_ref, *, tq):
    i = pl.program_id(2)
    imp = imp_ref[...]
    n_sel = imp.shape[0]
    blk = lax.broadcasted_iota(jnp.int32, imp.shape, 0)
    t = i * tq + lax.broadcasted_iota(jnp.int32, imp.shape, 1)
    cur = t // SEL_BLOCK
    avail = blk <= cur
    forced = jnp.where(blk == 0, 1.0, jnp.where(blk == cur, 1.0, jnp.where(blk == cur - 1, 1.0, 0.0)))
    score = jnp.where(avail, imp + FORCED_BONUS * forced, -jnp.inf)
    picked = jnp.zeros(imp.shape, F32)
    for _ in range(min(SEL_TOPK, n_sel)):
        mx = jnp.max(score, axis=0, keepdims=True)
        first = jnp.min(jnp.where(score == mx, blk, n_sel), axis=0, keepdims=True)
        hit = blk == first
        picked = jnp.where(hit, 1.0, picked)
        score = jnp.where(hit, -jnp.inf, score)
    feat = jnp.where(avail, jnp.where(picked > 0.0, 0.0, SEL_OFF), SEL_OFF)
    o_ref[...] = feat.T.astype(o_ref.dtype)


def topk_select(imp_t, tq=256):
    B, G, n_sel, S = imp_t.shape
    return pl.pallas_call(
        functools.partial(_topk_kernel, tq=tq),
        grid=(B, G, S // tq),
        in_specs=[pl.BlockSpec((None, None, n_sel, tq), lambda b, g, i: (b, g, 0, i))],
        out_specs=pl.BlockSpec((None, None, tq, n_sel), lambda b, g, i: (b, g, i, 0)),
        out_shape=jax.ShapeDtypeStruct((B, G, S, n_sel), BF16),
        compiler_params=_cparams(("parallel", "parallel", "parallel")),
        name="nsa_topk",
    )(imp_t)


def _sel_attn_kernel(q_ref, mf_ref, k_ref, e_ref, vt_ref, o_ref, *, tq, qsub, tkv):
    i = pl.program_id(2)
    R = NSA_REP
    nsub = tq // qsub
    rows = R * qsub
    q0 = i * tq
    ntile = (q0 + tq - 1) // tkv + 1
    rel = (lax.broadcasted_iota(jnp.int32, (tkv, rows), 1) & (qsub - 1)) \
        - lax.broadcasted_iota(jnp.int32, (tkv, rows), 0)

    qa = []
    for u in range(nsub):
        qu = q_ref[u * qsub:(u + 1) * qsub, :]
        mf = mf_ref[u * qsub:(u + 1) * qsub, :]
        qa.append(jnp.concatenate(
            [jnp.concatenate([qu[:, r * LANE:(r + 1) * LANE], mf], axis=1) for r in range(R)],
            axis=0))

    def body(j, carry, masked):
        ks = pl.multiple_of(j * tkv, tkv)
        ka = jnp.concatenate([k_ref[pl.ds(ks, tkv), :], e_ref[pl.ds(ks, tkv), :]], axis=1)
        vt = vt_ref[:, pl.ds(ks, tkv)]
        out = []
        scores = lambda u: lax.dot_general(ka, qa[u], NT_DIMS, preferred_element_type=F32)
        ahead = 2
        sts = [scores(u) for u in range(min(ahead, nsub))]
        for u in range(nsub):
            m, l, acc = carry[u]
            if u + ahead < nsub:
                sts.append(scores(u + ahead))
            st = sts[u]
            if masked:
                st = jnp.where(rel >= ks - (q0 + u * qsub), st, NEG_BIG)
            m_new = jnp.maximum(m, jnp.max(st, axis=0, keepdims=True))
            a = jnp.exp2(m - m_new)
            res = jnp.dot(vt, jnp.exp2(st - m_new).astype(BF16), preferred_element_type=F32)
            out.append((m_new, a * l + res[LANE:LANE + 1, :], a * acc + res[:LANE, :]))
        return tuple(out)

    init = tuple((jnp.full((1, rows), NEG_BIG, F32), jnp.zeros((1, rows), F32),
                  jnp.zeros((LANE, rows), F32)) for _ in range(nsub))
    carry = lax.fori_loop(0, ntile - 1, lambda j, c: body(j, c, False), init)
    final = body(ntile - 1, carry, True)
    for u in range(nsub):
        _, l, acc = final[u]
        ot = acc / l
        for r in range(R):
            o_ref[u * qsub:(u + 1) * qsub, r * LANE:(r + 1) * LANE] = \
                ot[:, r * qsub:(r + 1) * qsub].T


def sel_attention(q_rot, mfeat, kvb, onehot, *, k_blk, v_blk, tq=512, qsub=128, tkv=1024):
    B, S, _ = q_rot.shape
    G = NSA_KV_GROUPS
    n_sel = onehot.shape[1]
    qw = NSA_REP * LANE
    tkv = min(tkv, S)
    assert tkv % tq == 0 and S % tkv == 0
    v = kvb[:, :, v_blk * LANE:(v_blk + G) * LANE].reshape(B, S, G, LANE).transpose(0, 2, 3, 1)
    pad = jnp.zeros((B, G, 16, S), BF16).at[:, :, 0, :].set(1.0)
    vt = jnp.concatenate([v, pad], axis=2)
    return pl.pallas_call(
        functools.partial(_sel_attn_kernel, tq=tq, qsub=qsub, tkv=tkv),
        grid=(B, G, S // tq),
        in_specs=[pl.BlockSpec((None, tq, qw), lambda b, g, i: (b, i, g)),
                  pl.BlockSpec((None, None, tq, n_sel), lambda b, g, i: (b, g, i, 0)),
                  pl.BlockSpec((None, S, LANE), lambda b, g, i: (b, 0, k_blk + g)),
                  pl.BlockSpec((S, n_sel), lambda b, g, i: (0, 0)),
                  pl.BlockSpec((None, None, LANE + 16, S), lambda b, g, i: (b, g, 0, 0))],
        out_specs=pl.BlockSpec((None, tq, qw), lambda b, g, i: (b, i, g)),
        out_shape=jax.ShapeDtypeStruct((B, S, NSA_HEADS * LANE), F32),
        compiler_params=_cparams(("parallel", "parallel", "arbitrary")),
        name="nsa_sel_attn",
    )(q_rot, mfeat, kvb, onehot, vt)


def _nsa_combine_kernel(g_ref, oc_ref, os_ref, ow_ref, o_ref):
    gates = jax.nn.sigmoid(g_ref[...])
    for h in range(NSA_HEADS):
        sl = slice(h * LANE, (h + 1) * LANE)
        acc = gates[:, 3 * h:3 * h + 1] * oc_ref[:, sl]
        acc = acc + gates[:, 3 * h + 1:3 * h + 2] * os_ref[:, sl]
        acc = acc + gates[:, 3 * h + 2:3 * h + 3] * ow_ref[:, sl]
        o_ref[:, sl] = acc.astype(o_ref.dtype)


def nsa_combine(gate_logits, o_cmp, o_sel, o_win, ts=256):
    B, S, W = o_cmp.shape
    spec = pl.BlockSpec((None, ts, W), lambda b, i: (b, i, 0))
    return pl.pallas_call(
        _nsa_combine_kernel,
        grid=(B, S // ts),
        in_specs=[pl.BlockSpec((None, ts, LANE), lambda b, i: (b, i, 0)), spec, spec, spec],
        out_specs=spec,
        out_shape=jax.ShapeDtypeStruct((B, S, W), BF16),
        compiler_params=_cparams(("parallel", "parallel")),
        name="nsa_combine",
    )(gate_logits, o_cmp, o_sel, o_win)


def _dil_combine_kernel(*refs, dils):
    n = len(dils)
    o_refs, lse_refs, out_ref = refs[:n], refs[n:2 * n], refs[2 * n]
    nat_o, nat_l = refs[2 * n + 1:3 * n + 1], refs[3 * n + 1:]
    ts = out_ref.shape[0]
    for h in range(out_ref.shape[1] // LANE):
        sl = slice(h * LANE, (h + 1) * LANE)
        for g, dil in enumerate(dils):
            for r in range(dil):
                rows = pl.ds(r, ts // dil, stride=dil) if dil > 1 else slice(None)
                nat_o[g][rows, :] = o_refs[g][r, :, sl]
                nat_l[g][rows, :] = lse_refs[g][r, :, sl]
        lses = [r[...] for r in nat_l]
        mx = functools.reduce(jnp.maximum, lses)
        ws = [jnp.exp2(l - mx) for l in lses]
        den = functools.reduce(lambda a, b: a + b, ws)
        num = functools.reduce(lambda a, b: a + b, [w * r[...] for w, r in zip(ws, nat_o)])
        out_ref[:, sl] = (num / den).astype(out_ref.dtype)


def dil_combine(outs, lses, ts=128):
    dils = tuple(o.shape[1] for o in outs)
    B, W = outs[0].shape[0], outs[0].shape[-1]
    S = outs[0].shape[1] * outs[0].shape[2]
    specs = [pl.BlockSpec((None, d, ts // d, W), lambda b, i: (b, 0, i, 0)) for d in dils]
    return pl.pallas_call(
        functools.partial(_dil_combine_kernel, dils=dils),
        grid=(B, S // ts),
        in_specs=specs + specs,
        out_specs=pl.BlockSpec((None, ts, W), lambda b, i: (b, i, 0)),
        out_shape=jax.ShapeDtypeStruct((B, S, W), BF16),
        scratch_shapes=[pltpu.VMEM((ts, LANE), F32)] * (2 * len(dils)),
        compiler_params=_cparams(("parallel", "parallel")),
        name="dil_combine",
    )(*outs, *lses)


def _softplus(x):
    return jnp.maximum(x, 0.0) + jnp.log1p(jnp.exp(-jnp.abs(x)))


def _gate_band_starts():
    ntile = D_RNN // LANE
    starts = []
    for j in range(ntile):
        n_lo = (j * LANE) // RNN_BLOCK_DIM
        n_hi = (j * LANE + LANE - 1) // RNN_BLOCK_DIM
        lo = (n_lo * RNN_BLOCK_DIM) // LANE
        hi = -(-((n_hi + 1) * RNN_BLOCK_DIM) // LANE)
        assert hi - lo <= 4
        starts.append(min(lo, ntile - 4))
    return starts


def _rglru_kernel(y_ref, xr_ref, cw_ref, cb_ref, wg_ref, bg_ref, lam_ref, o_ref,
                  h_ref, tail_ref, *, ts, starts):
    i = pl.program_id(1)

    @pl.when(i == 0)
    def _():
        h_ref[...] = jnp.zeros_like(h_ref)
        tail_ref[...] = jnp.zeros_like(tail_ref)

    xr = xr_ref[...]
    ext = jnp.concatenate([tail_ref[...], xr], axis=0)
    x = cb_ref[...] + xr * cw_ref[CONV_WIDTH - 1:CONV_WIDTH, :]
    for d in range(1, CONV_WIDTH):
        shifted = pltpu.roll(ext, d, 0)[8:8 + ts]
        x = x + shifted * cw_ref[CONV_WIDTH - 1 - d:CONV_WIDTH - d, :]
    tail_ref[...] = xr[ts - 8:ts]

    xb = x.astype(BF16)
    gl = []
    for g in range(2):
        tiles = [jnp.dot(xb[:, a * LANE:(a + 4) * LANE], wg_ref[g, j],
                         preferred_element_type=F32) for j, a in enumerate(starts)]
        gl.append(jnp.concatenate(tiles, axis=1) + bg_ref[g:g + 1, :])
    r = jax.nn.sigmoid(gl[0])
    ig = jax.nn.sigmoid(gl[1])
    log_a = (-LRU_C) * r * _softplus(-lam_ref[...])
    a = jnp.exp(log_a)
    b = jnp.sqrt(-jnp.tanh(log_a) * (a * a + 1.0)) * (ig * x)

    row = lax.broadcasted_iota(jnp.int32, a.shape, 0)
    d = 1
    while d < ts:
        keep = row >= d
        b = b + a * jnp.where(keep, pltpu.roll(b, d, 0), 0.0)
        a = a * jnp.where(keep, pltpu.roll(a, d, 0), 1.0)
        d *= 2
    h = a * h_ref[0:1, :] + b
    h_ref[0:1, :] = h[ts - 1:ts, :]
    o_ref[...] = (h * jax.nn.gelu(y_ref[...])).astype(o_ref.dtype)


def rglru_scan(proj, conv_w, conv_b, wband, b_gate, lam, ts=128):
    B, S, _ = proj.shape
    C = D_RNN
    starts = _gate_band_starts()
    vec = lambda n: pl.BlockSpec((n, C), lambda b, i: (0, 0))
    return pl.pallas_call(
        functools.partial(_rglru_kernel, ts=ts, starts=starts),
        grid=(B, S // ts),
        in_specs=[pl.BlockSpec((None, ts, C), lambda b, i: (b, i, 0)),
                  pl.BlockSpec((None, ts, C), lambda b, i: (b, i, 1)),
                  vec(CONV_WIDTH), vec(1),
                  pl.BlockSpec(wband.shape, lambda b, i: (0, 0, 0, 0)),
                  vec(2), vec(1)],
        out_specs=pl.BlockSpec((None, ts, C), lambda b, i: (b, i, 0)),
        out_shape=jax.ShapeDtypeStruct((B, S, C), BF16),
        scratch_shapes=[pltpu.VMEM((8, C), F32), pltpu.VMEM((8, C), F32)],
        compiler_params=_cparams(("arbitrary", "arbitrary")),
        name="rglru_scan",
    )(proj, proj, conv_w, conv_b.reshape(1, C), wband, b_gate, lam.reshape(1, C))


def _gate_band_weights(w_gate):
    starts = _gate_band_starts()
    dense = jnp.stack([jax.scipy.linalg.block_diag(*[w_gate[g, n] for n in range(RNN_BLOCKS)])
                       for g in range(2)])
    tiles = [dense[:, a * LANE:(a + 4) * LANE, j * LANE:(j + 1) * LANE]
             for j, a in enumerate(starts)]
    return jnp.stack(tiles, axis=1).astype(BF16)


def _row_tile(T):
    return 512 if T % 512 == 0 else T


def nsa_mixer(hn, x, mod, gate_blk, w_in, cmp_pe, cmp_w1, cmp_w2, w_out, cos2, sin2):
    B, S, D = x.shape
    T = B * S
    G = NSA_KV_GROUPS
    hq = NSA_HEADS * HEAD_DIM
    kvw = NSA_KV_WIDTH
    main = hq + 6 * kvw
    tm = _row_tile(T)
    hn2 = hn.reshape(T, D)
    w_gate = jnp.pad(w_in[:, main:], ((0, 0), (0, LANE - 3 * NSA_HEADS))).astype(BF16)
    gate_logits = matmul(hn2, w_gate, tm=tm, tn=LANE, tk=D).reshape(B, S, LANE)

    q_cmp, q_rot = proj_heads(hn2, w_in[:, :hq].astype(BF16), cos2, sin2,
                              [(SCALE_Q,) * NSA_HEADS, (ROPE_Q,) * NSA_HEADS], tm=tm)
    g_rope, g_cast = (ROPE,) * G, (CAST,) * G
    (kvb,) = proj_heads(hn2, w_in[:, hq + 2 * kvw:main].astype(BF16), cos2, sin2,
                        [g_rope + g_cast + g_rope + g_cast], tm=tm)
    q_rot = q_rot.reshape(B, 1, S, hq)
    kvb = kvb.reshape(B, 1, S, 4 * kvw)

    npiece = S // CMP_STRIDE
    kv_cmp = matmul(hn2, w_in[:, hq:hq + 2 * kvw].astype(BF16), tm=tm, tn=2 * kvw, tk=D)
    kv_cmp = kv_cmp.reshape(B, npiece, CMP_STRIDE, 2, G, HEAD_DIM)
    pieces = kv_cmp.transpose(3, 0, 4, 1, 2, 5).reshape(2, B, G, npiece, CMP_STRIDE * HEAD_DIM)
    kvc = compress(pieces, cmp_pe.reshape(2, 2, CMP_STRIDE * HEAD_DIM),
                   cmp_w1.astype(BF16), cmp_w2.astype(BF16))

    n_sb = S // SEL_BLOCK
    cmp_start = np.arange(npiece) * CMP_STRIDE
    sel_start = np.arange(n_sb) * SEL_BLOCK
    overlap = ((cmp_start[:, None] < sel_start[None, :] + SEL_BLOCK)
               & (cmp_start[:, None] + CMP_BLOCK > sel_start[None, :]))
    overlap[npiece - 1] = False
    overlap_t = jnp.asarray(overlap.T, BF16)
    onehot = jnp.asarray(np.arange(S)[:, None] // SEL_BLOCK == np.arange(n_sb)[None, :], BF16)

    o_cmp, imp_t = cmp_attention(q_cmp.reshape(B, S, hq), kvc, overlap_t)
    mfeat = topk_select(imp_t)
    kb = kvw // LANE
    o_sel = sel_attention(q_rot.reshape(B, S, hq), mfeat, kvb.reshape(B, S, 4 * kvw),
                          onehot, k_blk=0, v_blk=kb)
    (o_win,) = banded_attention(q_rot, kvb, kvb, n_kv=G, R=NSA_REP, k_blk=2 * kb, v_blk=3 * kb,
                                nwin=(NSA_WINDOW - 1 + LANE - 1) // LANE,
                                window=NSA_WINDOW - 1, want_lse=False)
    o = nsa_combine(gate_logits, o_cmp, o_sel, o_win.reshape(B, S, hq))
    return matmul(o.reshape(T, hq), w_out.astype(BF16), tm=tm, tn=D, tk=hq,
                  epilogue="residual", res=x.reshape(T, D), mod=mod, gate_blk=gate_blk,
                  rows_per_batch=S).reshape(B, S, D)


def dilated_mixer(hn, x, mod, gate_blk, w_in, w_out, cos2, sin2):
    B, S, D = x.shape
    T = B * S
    H = DIL_HEADS
    hw = H * HEAD_DIM
    tm = _row_tile(T)
    outs, lses = [], []
    for g, (window, dil) in enumerate(DIL_PATTERNS):
        w = window // dil
        L = S // dil
        regroup = lambda t: t.reshape(t.shape[:-2] + (L, dil, t.shape[-1])).swapaxes(-3, -2)
        hn_g = regroup(hn).reshape(T, D)
        cos_g, sin_g = regroup(cos2).reshape(S, LANE), regroup(sin2).reshape(S, LANE)
        q, k, v = [proj_heads(hn_g, w_in[:, (3 * g + c) * hw:(3 * g + c + 1) * hw].astype(BF16),
                              cos_g, sin_g, [(kind,) * H], tm=tm)[0].reshape(B, dil, L, hw)
                   for c, kind in enumerate((ROPE_Q, ROPE, CAST))]
        o, lse = banded_attention(q, k, v, n_kv=H, R=1, nwin=(w + LANE - 1) // LANE, window=w,
                                  want_lse=True)
        outs.append(o)
        lses.append(lse)
    o = dil_combine(outs, lses)
    return matmul(o.reshape(T, hw), w_out.astype(BF16), tm=tm, tn=D, tk=hw,
                  epilogue="residual", res=x.reshape(T, D), mod=mod, gate_blk=gate_blk,
                  rows_per_batch=S).reshape(B, S, D)


def rglru_mixer(hn, x, mod, gate_blk, w_in, conv_w, conv_b, w_gate, b_gate, lam, w_out):
    B, S, D = x.shape
    T = B * S
    tm = _row_tile(T)
    proj = matmul(hn.reshape(T, D), w_in.astype(BF16), tm=tm, tn=D_RNN, tk=D)
    hy = rglru_scan(proj.reshape(B, S, 2 * D_RNN), conv_w, conv_b, _gate_band_weights(w_gate),
                    b_gate, lam)
    return matmul(hy.reshape(T, D_RNN), w_out.astype(BF16), tm=tm, tn=D, tk=D_RNN,
                  epilogue="residual", res=x.reshape(T, D), mod=mod, gate_blk=gate_blk,
                  rows_per_batch=S).reshape(B, S, D)


def mlp(hn, x, mod, gate_blk, w1, w2):
    B, S, D = x.shape
    T = B * S
    tm = _row_tile(T)
    dff = w1.shape[1]
    h = matmul(hn.reshape(T, D), w1.astype(BF16), tm=tm, tn=1024, tk=D, out_dtype=BF16,
               epilogue="relu2")
    return matmul(h, w2.astype(BF16), tm=tm, tn=D, tk=2048, epilogue="residual",
                  res=x.reshape(T, D), mod=mod, gate_blk=gate_blk,
                  rows_per_batch=S).reshape(B, S, D)


def kernel(x, c, l0_w_ada, l0_b_ada, l0_norm1, l0_w_in, l0_cmp_pe, l0_cmp_w1, l0_cmp_w2, l0_w_out, l0_norm2, l0_w_ff1, l0_w_ff2, l1_w_ada, l1_b_ada, l1_norm1, l1_w_in, l1_w_out, l1_norm2, l1_w_ff1, l1_w_ff2, l2_w_ada, l2_b_ada, l2_norm1, l2_w_in, l2_conv_w, l2_conv_b, l2_w_gate, l2_b_gate, l2_lambda, l2_w_out, l2_norm2, l2_w_ff1, l2_w_ff2, l3_w_ada, l3_b_ada, l3_norm1, l3_w_in, l3_cmp_pe, l3_cmp_w1, l3_cmp_w2, l3_w_out, l3_norm2, l3_w_ff1, l3_w_ff2, norm_f):
    B, S, D = x.shape
    cos2, sin2 = rope_tables(S)
    layers = (
        (l0_w_ada, l0_b_ada, l0_norm1, l0_norm2, l0_w_ff1, l0_w_ff2,
         (l0_w_in, l0_cmp_pe, l0_cmp_w1, l0_cmp_w2, l0_w_out)),
        (l1_w_ada, l1_b_ada, l1_norm1, l1_norm2, l1_w_ff1, l1_w_ff2, (l1_w_in, l1_w_out)),
        (l2_w_ada, l2_b_ada, l2_norm1, l2_norm2, l2_w_ff1, l2_w_ff2,
         (l2_w_in, l2_conv_w, l2_conv_b, l2_w_gate, l2_b_gate, l2_lambda, l2_w_out)),
        (l3_w_ada, l3_b_ada, l3_norm1, l3_norm2, l3_w_ff1, l3_w_ff2,
         (l3_w_in, l3_cmp_pe, l3_cmp_w1, l3_cmp_w2, l3_w_out)),
    )
    for li in range(DEPTH):
        w_ada, b_ada, n1, n2, ff1, ff2, mix = layers[li]
        mod = adaln(c, w_ada, b_ada).reshape(B, 1, 6 * D)
        hn = modulate(x, n1, mod, 0, 1)
        kind = li % N_MIXERS
        if kind == 0:
            x = nsa_mixer(hn, x, mod, 2, *mix, cos2, sin2)
        elif kind == 1:
            x = dilated_mixer(hn, x, mod, 2, *mix, cos2, sin2)
        else:
            x = rglru_mixer(hn, x, mod, 2, *mix)
        hn = modulate(x, n2, mod, 3, 4)
        x = mlp(hn, x, mod, 5, ff1, ff2)
    return rmsnorm(x, norm_f)
```

```python
import functools
import math

import jax
import jax.numpy as jnp
import numpy as np
from jax import lax
from jax.experimental import pallas as pl
from jax.experimental.pallas import tpu as pltpu

F32 = jnp.float32
BF16 = jnp.bfloat16

D_MODEL = 2048
DEPTH = 4
N_MIXERS = 3
HEAD_DIM = 128
ROPE_THETA = 10000.0
NORM_EPS = 1e-6

NSA_HEADS = D_MODEL // HEAD_DIM
NSA_KV_GROUPS = 4
NSA_REP = NSA_HEADS // NSA_KV_GROUPS
NSA_KV_WIDTH = NSA_KV_GROUPS * HEAD_DIM
CMP_BLOCK = 32
CMP_STRIDE = 16
CMP_HIDDEN = 4 * HEAD_DIM
SEL_BLOCK = 64
SEL_TOPK = 16
NSA_WINDOW = 512
FORCED_BONUS = 1e9

DIL_HEADS = D_MODEL // HEAD_DIM
DIL_PATTERNS = ((128, 1), (512, 4), (2048, 16))

D_RNN = 2688
RNN_BLOCKS = 16
RNN_BLOCK_DIM = D_RNN // RNN_BLOCKS
CONV_WIDTH = 4
LRU_C = 8.0

LANE = 128
LOG2E = math.log2(math.e)
QK_SCALE = LOG2E / math.sqrt(HEAD_DIM)
NEG_BIG = -1e30
SEL_OFF = -float(2 ** 30)
VMEM_LIMIT = 56 * 1024 * 1024

NT_DIMS = (((1,), (1,)), ((), ()))


def _cparams(sem):
    return pltpu.CompilerParams(dimension_semantics=sem, vmem_limit_bytes=VMEM_LIMIT)


def _adaln_kernel(c_ref, w_ref, b_ref, o_ref):
    w = w_ref[...]
    for b in range(c_ref.shape[0]):
        c = c_ref[b]
        cond = c * jax.nn.sigmoid(c)
        o_ref[b:b + 1, :] = jnp.sum(w * cond, axis=0, keepdims=True) + b_ref[...]


def adaln(c, w_ada, b_ada, tn=1024):
    B, D = c.shape
    N = w_ada.shape[1]
    return pl.pallas_call(
        _adaln_kernel,
        grid=(N // tn,),
        in_specs=[pl.BlockSpec((B, D, 1), lambda j: (0, 0, 0)),
                  pl.BlockSpec((D, tn), lambda j: (0, j)),
                  pl.BlockSpec((1, tn), lambda j: (0, j))],
        out_specs=pl.BlockSpec((B, tn), lambda j: (0, j)),
        out_shape=jax.ShapeDtypeStruct((B, N), F32),
        compiler_params=_cparams(("parallel",)),
        name="adaln",
    )(c.reshape(B, D, 1), w_ada, b_ada.reshape(1, N))


def _modulate_kernel(x_ref, gain_ref, sh_ref, sc_ref, o_ref):
    x = x_ref[...]
    ms = jnp.mean(x * x, axis=-1, keepdims=True)
    y = x * lax.rsqrt(ms + NORM_EPS) * gain_ref[...]
    o_ref[...] = (y * (1.0 + sc_ref[...]) + sh_ref[...]).astype(o_ref.dtype)


def modulate(x, gain, mod, shift_blk, scale_blk, ts=512):
    B, S, D = x.shape
    return pl.pallas_call(
        _modulate_kernel,
        grid=(B, S // ts),
        in_specs=[pl.BlockSpec((None, ts, D), lambda b, i: (b, i, 0)),
                  pl.BlockSpec((1, D), lambda b, i: (0, 0)),
                  pl.BlockSpec((None, 1, D), lambda b, i: (b, 0, shift_blk)),
                  pl.BlockSpec((None, 1, D), lambda b, i: (b, 0, scale_blk))],
        out_specs=pl.BlockSpec((None, ts, D), lambda b, i: (b, i, 0)),
        out_shape=jax.ShapeDtypeStruct((B, S, D), BF16),
        compiler_params=_cparams(("parallel", "parallel")),
        name="modulate",
    )(x, gain.reshape(1, D), mod, mod)


def _mm_kernel(*refs, nk, epilogue):
    refs = list(refs)
    a_ref, w_ref = refs[:2]
    del refs[:2]
    if epilogue.startswith("residual"):
        res_ref, gate_ref = refs[:2]
        del refs[:2]
    if epilogue == "residual_norm":
        gain_ref, sh_ref, sc_ref = refs[:3]
        del refs[:3]
    elif epilogue == "residual_final":
        gain_ref = refs.pop(0)
    o_ref = refs.pop(0)
    hn_ref = refs.pop(0) if epilogue == "residual_norm" else None
    rest = refs

    def finish(acc):
        if epilogue == "relu2":
            r = jnp.maximum(acc, 0.0)
            acc = r * r
        elif epilogue.startswith("residual"):
            acc = res_ref[...] + gate_ref[...] * acc
        if epilogue in ("residual_norm", "residual_final"):
            ms = jnp.mean(acc * acc, axis=-1, keepdims=True)
            y = acc * lax.rsqrt(ms + NORM_EPS) * gain_ref[...]
            if epilogue == "residual_final":
                o_ref[...] = y
                return
            hn_ref[...] = (y * (1.0 + sc_ref[...]) + sh_ref[...]).astype(hn_ref.dtype)
        o_ref[...] = acc.astype(o_ref.dtype)

    def part():
        return jnp.dot(a_ref[...], w_ref[...], preferred_element_type=F32)

    if nk == 1:
        finish(part())
        return
    acc_ref = rest[0]
    k = pl.program_id(2)

    @pl.when(k == 0)
    def _():
        acc_ref[...] = part()

    if nk > 2:
        @pl.when(jnp.logical_and(k > 0, k < nk - 1))
        def _():
            acc_ref[...] += part()

    @pl.when(k == nk - 1)
    def _():
        finish(acc_ref[...] + part())


def matmul(a, w, *, tm, tn, tk, out_dtype=F32, epilogue="none", res=None, mod=None,
           gate_blk=0, rows_per_batch=None, norm_gain=None, norm_mod=None, shift_blk=0,
           scale_blk=0):
    M, K = a.shape
    N = w.shape[1]
    nk = K // tk
    assert M % tm == 0 and N % tn == 0 and K % tk == 0
    in_specs = [pl.BlockSpec((tm, tk), lambda i, j, k: (i, k)),
                pl.BlockSpec((tk, tn), lambda i, j, k: (k, j))]
    args = [a, w]
    out_spec = pl.BlockSpec((tm, tn), lambda i, j, k: (i, j))
    out_specs, out_shape = out_spec, jax.ShapeDtypeStruct((M, N), out_dtype)
    if epilogue.startswith("residual"):
        assert rows_per_batch % tm == 0 and tn == D_MODEL == N
        batch = lambda i: i * tm // rows_per_batch
        chunk = lambda blk: pl.BlockSpec((None, 1, tn), lambda i, j, k: (batch(i), 0, blk))
        in_specs += [out_spec, chunk(gate_blk)]
        args += [res, mod]
        if epilogue != "residual":
            in_specs.append(pl.BlockSpec((1, tn), lambda i, j, k: (0, 0)))
            args.append(norm_gain.reshape(1, tn))
        if epilogue == "residual_norm":
            in_specs += [chunk(shift_blk), chunk(scale_blk)]
            args += [norm_mod, norm_mod]
            out_specs = [out_spec, out_spec]
            out_shape = [out_shape, jax.ShapeDtypeStruct((M, N), BF16)]
    scratch = [pltpu.VMEM((tm, tn), F32)] if nk > 1 else []
    return pl.pallas_call(
        functools.partial(_mm_kernel, nk=nk, epilogue=epilogue),
        grid=(M // tm, N // tn, nk),
        in_specs=in_specs,
        out_specs=out_specs,
        out_shape=out_shape,
        scratch_shapes=scratch,
        compiler_params=_cparams(("parallel", "parallel", "arbitrary")),
        name="mm_" + epilogue,
    )(*args)


def rope_tables(S):
    inv_freq = ROPE_THETA ** (-jnp.arange(0, HEAD_DIM, 2, dtype=F32) / HEAD_DIM)
    ang = jnp.arange(S, dtype=F32)[:, None] * inv_freq[None, :]
    cos, sin = jnp.cos(ang), jnp.sin(ang)
    return jnp.concatenate([cos, cos], axis=-1), jnp.concatenate([-sin, sin], axis=-1)


CAST, ROPE, ROPE_Q, SCALE_Q = 0, 1, 2, 3


def _proj_heads_kernel(a_ref, w_ref, c_ref, s_ref, *o_refs, kinds):
    acc = jnp.dot(a_ref[...], w_ref[...], preferred_element_type=F32)
    c = c_ref[...]
    s = s_ref[...]
    for o_ref, head_kinds in zip(o_refs, kinds):
        for h, kind in enumerate(head_kinds):
            sl = slice(h * LANE, (h + 1) * LANE)
            t = acc[:, sl]
            if kind in (ROPE, ROPE_Q):
                t = t * c + pltpu.roll(t, HEAD_DIM // 2, 1) * s
            if kind in (ROPE_Q, SCALE_Q):
                t = t * QK_SCALE
            o_ref[:, sl] = t.astype(o_ref.dtype)


def proj_heads(a, w, cos2, sin2, kinds, *, tm):
    M, K = a.shape
    N = w.shape[1]
    nt = cos2.shape[0] // tm
    tab = pl.BlockSpec((tm, LANE), lambda i: (i % nt, 0))
    out_spec = pl.BlockSpec((tm, N), lambda i: (i, 0))
    return pl.pallas_call(
        functools.partial(_proj_heads_kernel, kinds=tuple(kinds)),
        grid=(M // tm,),
        in_specs=[pl.BlockSpec((tm, K), lambda i: (i, 0)), pl.BlockSpec((K, N), lambda i: (0, 0)),
                  tab, tab],
        out_specs=[out_spec] * len(kinds),
        out_shape=[jax.ShapeDtypeStruct((M, N), BF16)] * len(kinds),
        compiler_params=_cparams(("parallel",)),
        name="proj_heads",
    )(a, w, cos2, sin2)


def _ones_column(n):
    return jnp.where(lax.broadcasted_iota(jnp.int32, (n, LANE), 1) == 0, 1.0, 0.0).astype(BF16)


def _softmax_pv(s, m, v, ones):
    p = jnp.exp2(s - m).astype(BF16)
    res = jnp.dot(p, jnp.concatenate([v, ones], axis=1), preferred_element_type=F32)
    return res[:, LANE:LANE + 1], res[:, :LANE]


def _banded_kernel(q_ref, k_ref, v_ref, o_ref, *lse_refs, R, hb, nwin, window, tq, qsub, nkeys):
    i = pl.program_id(3)
    L = k_ref.shape[0]
    rows = R * qsub
    rel = (lax.broadcasted_iota(jnp.int32, (rows, nkeys), 0) & (qsub - 1)) \
        - lax.broadcasted_iota(jnp.int32, (rows, nkeys), 1)
    ones = _ones_column(nkeys)
    units = [(h, j) for h in range(hb) for j in range(tq // qsub)]

    def scores(h, j):
        q0 = i * tq + j * qsub
        kstart = pl.multiple_of(jnp.clip(q0 - nwin * LANE, 0, L - nkeys), LANE)
        qj = q_ref[j * qsub:(j + 1) * qsub, h * R * LANE:(h + 1) * R * LANE]
        if R > 1:
            q = jnp.concatenate([qj[:, r * LANE:(r + 1) * LANE] for r in range(R)], axis=0)
        else:
            q = qj
        s = lax.dot_general(q, k_ref[pl.ds(kstart, nkeys), h * LANE:(h + 1) * LANE], NT_DIMS,
                            preferred_element_type=F32)
        return q0, kstart, s

    ahead = 2
    pending = [scores(*u) for u in units[:ahead]]
    for n, (h, j) in enumerate(units):
        if n + ahead < len(units):
            pending.append(scores(*units[n + ahead]))
        q0, kstart, s = pending[n]
        v = v_ref[pl.ds(kstart, nkeys), h * LANE:(h + 1) * LANE]
        diff = rel + (q0 - kstart)
        valid = lax.bitcast_convert_type(diff, jnp.uint32) <= jnp.uint32(window)
        s = jnp.where(valid, s, NEG_BIG)
        m = jnp.max(s, axis=-1, keepdims=True)
        l, o = _softmax_pv(s, m, v, ones)
        o = o / l
        for r in range(R):
            col = (h * R + r) * LANE
            o_ref[j * qsub:(j + 1) * qsub, col:col + LANE] = o[r * qsub:(r + 1) * qsub]
        if lse_refs:
            lse_refs[0][j * qsub:(j + 1) * qsub, h * LANE:(h + 1) * LANE] = jnp.broadcast_to(
                m + jnp.log(l) * LOG2E, (rows, LANE))


def banded_attention(q_arr, k_arr, v_arr, *, n_kv, R, nwin, window, want_lse, q_blk=0, k_blk=0,
                     v_blk=0):
    B, dil, L, _ = q_arr.shape
    tq = min(512, L)
    qsub = LANE
    nkeys = min(qsub + nwin * LANE, L)
    hb = 4 if R == 1 else 1
    assert L % tq == 0 and n_kv % hb == 0
    assert q_blk % (hb * R) == 0 and k_blk % hb == 0 and v_blk % hb == 0
    width = n_kv * R * LANE
    out_shape = [jax.ShapeDtypeStruct((B, dil, L, width), F32)]
    out_specs = [pl.BlockSpec((None, None, tq, hb * R * LANE), lambda b, r, h, i: (b, r, i, h))]
    if want_lse:
        out_shape.append(out_shape[0])
        out_specs.append(out_specs[0])
    kv_spec = lambda blk: pl.BlockSpec((None, None, L, hb * LANE),
                                       lambda b, r, h, i: (b, r, 0, blk // hb + h))
    return pl.pallas_call(
        functools.partial(_banded_kernel, R=R, hb=hb, nwin=nwin, window=window, tq=tq, qsub=qsub,
                          nkeys=nkeys),
        grid=(B, dil, n_kv // hb, L // tq),
        in_specs=[pl.BlockSpec((None, None, tq, hb * R * LANE),
                               lambda b, r, h, i: (b, r, i, q_blk // (hb * R) + h)),
                  kv_spec(k_blk), kv_spec(v_blk)],
        out_specs=out_specs,
        out_shape=out_shape,
        compiler_params=_cparams(("parallel", "parallel", "parallel", "arbitrary")),
        name="banded_attn",
    )(q_arr, k_arr, v_arr)


def _compress_kernel(p_ref, pe_ref, w1_ref, w2_ref, o_ref):
    half = CMP_STRIDE * HEAD_DIM
    pieces = p_ref[...]
    top = (pieces + pe_ref[0:1, :]).astype(BF16)
    bot = (pieces + pe_ref[1:2, :]).astype(BF16)
    a = jnp.dot(top, w1_ref[0:half, :], preferred_element_type=F32)
    b = jnp.dot(bot, w1_ref[half:2 * half, :], preferred_element_type=F32)
    n = a.shape[0]
    hid = a + pltpu.roll(b, n - 1, 0)
    o_ref[...] = jnp.dot(jax.nn.gelu(hid).astype(BF16), w2_ref[...], preferred_element_type=F32)


def compress(pieces, pe, w1, w2):
    _, B, G, NP, W = pieces.shape
    return pl.pallas_call(
        _compress_kernel,
        grid=(2, B, G),
        in_specs=[pl.BlockSpec((None, None, None, NP, W), lambda t, b, g: (t, b, g, 0, 0)),
                  pl.BlockSpec((None, 2, W), lambda t, b, g: (t, 0, 0)),
                  pl.BlockSpec((None, 2 * W, CMP_HIDDEN), lambda t, b, g: (t, 0, 0)),
                  pl.BlockSpec((None, CMP_HIDDEN, HEAD_DIM), lambda t, b, g: (t, 0, 0))],
        out_specs=pl.BlockSpec((None, None, None, NP, HEAD_DIM), lambda t, b, g: (t, b, g, 0, 0)),
        out_shape=jax.ShapeDtypeStruct((2, B, G, NP, HEAD_DIM), F32),
        compiler_params=_cparams(("parallel", "parallel", "parallel")),
        name="nsa_compress",
    )(pieces, pe, w1, w2)


def _cmp_attn_kernel(q_ref, kc_ref, vc_ref, ov_ref, o_ref, imp_ref, *, tq, qsub):
    i = pl.program_id(2)
    R = NSA_REP
    nsub = tq // qsub
    rows = R * qsub
    n_var = kc_ref.shape[0] // LANE

    def run(nk):
        kc = kc_ref[0:nk, :].astype(BF16)
        vc = vc_ref[0:nk, :].astype(BF16)
        ov = ov_ref[:, 0:nk]
        tloc = lax.broadcasted_iota(jnp.int32, (rows, nk), 0) & (qsub - 1)
        cmp_end = lax.broadcasted_iota(jnp.int32, (rows, nk), 1) * CMP_STRIDE + (CMP_BLOCK - 1)

        def scores(u):
            qu = q_ref[u * qsub:(u + 1) * qsub, :]
            q = jnp.concatenate([qu[:, r * LANE:(r + 1) * LANE] for r in range(R)], axis=0)
            return lax.dot_general(q, kc, NT_DIMS, preferred_element_type=F32)

        ahead = 2
        pending = [scores(u) for u in range(min(ahead, nsub))]
        for u in range(nsub):
            if u + ahead < nsub:
                pending.append(scores(u + ahead))
            valid = cmp_end <= tloc + (i * tq + u * qsub)
            s = jnp.where(valid, pending[u], NEG_BIG)
            m = jnp.max(s, axis=-1, keepdims=True)
            p = jnp.where(valid, jnp.exp2(s - m), 0.0)
            l = jnp.sum(p, axis=-1, keepdims=True)
            p = p / jnp.where(l > 0, l, 1.0)
            o = jnp.dot(p.astype(BF16), vc, preferred_element_type=F32)
            for r in range(R):
                o_ref[u * qsub:(u + 1) * qsub, r * LANE:(r + 1) * LANE] = \
                    o[r * qsub:(r + 1) * qsub]
            psum = p[0:qsub]
            for r in range(1, R):
                psum = psum + p[r * qsub:(r + 1) * qsub]
            p_hi = psum.astype(BF16)
            p_lo = (psum - p_hi.astype(F32)).astype(BF16)
            imp_ref[:, u * qsub:(u + 1) * qsub] = (
                lax.dot_general(ov, p_hi, NT_DIMS, preferred_element_type=F32)
                + lax.dot_general(ov, p_lo, NT_DIMS, preferred_element_type=F32))

    need = ((i + 1) * tq - CMP_BLOCK) // CMP_STRIDE + 1
    var = jnp.clip((need - 1) // LANE, 0, n_var - 1)
    for v in range(n_var):
        pl.when(var == v)(functools.partial(run, (v + 1) * LANE))


def cmp_attention(q, kvc, overlap_t, tq=512, qsub=128):
    B, S, _ = q.shape
    G = NSA_KV_GROUPS
    NP = kvc.shape[3]
    n_sel = overlap_t.shape[0]
    qw = NSA_REP * LANE
    return pl.pallas_call(
        functools.partial(_cmp_attn_kernel, tq=tq, qsub=qsub),
        grid=(B, G, S // tq),
        in_specs=[pl.BlockSpec((None, tq, qw), lambda b, g, i: (b, i, g)),
                  pl.BlockSpec((None, None, None, NP, LANE), lambda b, g, i: (0, b, g, 0, 0)),
                  pl.BlockSpec((None, None, None, NP, LANE), lambda b, g, i: (1, b, g, 0, 0)),
                  pl.BlockSpec((n_sel, NP), lambda b, g, i: (0, 0))],
        out_specs=[pl.BlockSpec((None, tq, qw), lambda b, g, i: (b, i, g)),
                   pl.BlockSpec((None, None, n_sel, tq), lambda b, g, i: (b, g, 0, i))],
        out_shape=[jax.ShapeDtypeStruct((B, S, NSA_HEADS * LANE), F32),
                   jax.ShapeDtypeStruct((B, G, n_sel, S), F32)],
        compiler_params=_cparams(("parallel", "parallel", "parallel")),
        name="nsa_cmp_attn",
    )(q, kvc, kvc, overlap_t)


def _topk_kernel(imp_ref, o_ref, *, tq):
    i = pl.program_id(2)
    imp = imp_ref[...]
    n_sel = imp.shape[0]
    blk = lax.broadcasted_iota(jnp.int32, imp.shape, 0)
    t = i * tq + lax.broadcasted_iota(jnp.int32, imp.shape, 1)
    cur = t // SEL_BLOCK
    avail = blk <= cur
    forced = jnp.where(blk == 0, 1.0, jnp.where(blk == cur, 1.0, jnp.where(blk == cur - 1, 1.0, 0.0)))
    score = jnp.where(avail, imp + FORCED_BONUS * forced, -jnp.inf)
    picked = jnp.zeros(imp.shape, F32)
    for _ in range(min(SEL_TOPK, n_sel)):
        mx = jnp.max(score, axis=0, keepdims=True)
        first = jnp.min(jnp.where(score == mx, blk, n_sel), axis=0, keepdims=True)
        hit = blk == first
        picked = jnp.where(hit, 1.0, picked)
        score = jnp.where(hit, -jnp.inf, score)
    feat = jnp.where(avail, jnp.where(picked > 0.0, 0.0, SEL_OFF), SEL_OFF)
    o_ref[...] = feat.T.astype(o_ref.dtype)


def topk_select(imp_t, tq=1024):
    B, G, n_sel, S = imp_t.shape
    tq = min(tq, S)
    return pl.pallas_call(
        functools.partial(_topk_kernel, tq=tq),
        grid=(B, G, S // tq),
        in_specs=[pl.BlockSpec((None, None, n_sel, tq), lambda b, g, i: (b, g, 0, i))],
        out_specs=pl.BlockSpec((None, None, tq, n_sel), lambda b, g, i: (b, g, i, 0)),
        out_shape=jax.ShapeDtypeStruct((B, G, S, n_sel), BF16),
        compiler_params=_cparams(("parallel", "parallel", "parallel")),
        name="nsa_topk",
    )(imp_t)


def _sel_attn_kernel(q_ref, mf_ref, k_ref, e_ref, vt_ref, o_ref, *, tq, qsub, tkv):
    i = pl.program_id(2)
    R = NSA_REP
    nsub = tq // qsub
    rows = R * qsub
    q0 = i * tq
    ntile = (q0 + tq - 1) // tkv + 1
    rel = (lax.broadcasted_iota(jnp.int32, (tkv, rows), 1) & (qsub - 1)) \
        - lax.broadcasted_iota(jnp.int32, (tkv, rows), 0)

    qa = []
    for u in range(nsub):
        qu = q_ref[u * qsub:(u + 1) * qsub, :]
        mf = mf_ref[u * qsub:(u + 1) * qsub, :]
        qa.append(jnp.concatenate(
            [jnp.concatenate([qu[:, r * LANE:(r + 1) * LANE], mf], axis=1) for r in range(R)],
            axis=0))

    def body(j, carry, masked):
        ks = pl.multiple_of(j * tkv, tkv)
        ka = jnp.concatenate([k_ref[pl.ds(ks, tkv), :], e_ref[pl.ds(ks, tkv), :]], axis=1)
        vt = vt_ref[:, pl.ds(ks, tkv)]
        out = []
        scores = lambda u: lax.dot_general(ka, qa[u], NT_DIMS, preferred_element_type=F32)
        ahead = 2
        sts = [scores(u) for u in range(min(ahead, nsub))]
        for u in range(nsub):
            m, l, acc = carry[u]
            if u + ahead < nsub:
                sts.append(scores(u + ahead))
            st = sts[u]
            if masked:
                st = jnp.where(rel >= ks - (q0 + u * qsub), st, NEG_BIG)
            m_new = jnp.maximum(m, jnp.max(st, axis=0, keepdims=True))
            a = jnp.exp2(m - m_new)
            res = jnp.dot(vt, jnp.exp2(st - m_new).astype(BF16), preferred_element_type=F32)
            out.append((m_new, a * l + res[LANE:LANE + 1, :], a * acc + res[:LANE, :]))
        return tuple(out)

    init = tuple((jnp.full((1, rows), NEG_BIG, F32), jnp.zeros((1, rows), F32),
                  jnp.zeros((LANE, rows), F32)) for _ in range(nsub))
    carry = lax.fori_loop(0, ntile - 1, lambda j, c: body(j, c, False), init)
    final = body(ntile - 1, carry, True)
    for u in range(nsub):
        _, l, acc = final[u]
        ot = acc / l
        for r in range(R):
            o_ref[u * qsub:(u + 1) * qsub, r * LANE:(r + 1) * LANE] = \
                ot[:, r * qsub:(r + 1) * qsub].T


def sel_attention(q_rot, mfeat, kvb, onehot, *, k_blk, v_blk, tq=512, qsub=128, tkv=1024):
    B, S, _ = q_rot.shape
    G = NSA_KV_GROUPS
    n_sel = onehot.shape[1]
    qw = NSA_REP * LANE
    tkv = min(tkv, S)
    assert tkv % tq == 0 and S % tkv == 0
    v = kvb[:, :, v_blk * LANE:(v_blk + G) * LANE].reshape(B, S, G, LANE).transpose(0, 2, 3, 1)
    pad = jnp.zeros((B, G, 16, S), BF16).at[:, :, 0, :].set(1.0)
    vt = jnp.concatenate([v, pad], axis=2)
    return pl.pallas_call(
        functools.partial(_sel_attn_kernel, tq=tq, qsub=qsub, tkv=tkv),
        grid=(B, G, S // tq),
        in_specs=[pl.BlockSpec((None, tq, qw), lambda b, g, i: (b, i, g)),
                  pl.BlockSpec((None, None, tq, n_sel), lambda b, g, i: (b, g, i, 0)),
                  pl.BlockSpec((None, S, LANE), lambda b, g, i: (b, 0, k_blk + g)),
                  pl.BlockSpec((S, n_sel), lambda b, g, i: (0, 0)),
                  pl.BlockSpec((None, None, LANE + 16, S), lambda b, g, i: (b, g, 0, 0))],
        out_specs=pl.BlockSpec((None, tq, qw), lambda b, g, i: (b, i, g)),
        out_shape=jax.ShapeDtypeStruct((B, S, NSA_HEADS * LANE), F32),
        compiler_params=_cparams(("parallel", "parallel", "arbitrary")),
        name="nsa_sel_attn",
    )(q_rot, mfeat, kvb, onehot, vt)


def _nsa_combine_kernel(g_ref, oc_ref, os_ref, ow_ref, o_ref):
    gates = jax.nn.sigmoid(g_ref[...])
    for h in range(NSA_HEADS):
        sl = slice(h * LANE, (h + 1) * LANE)
        acc = gates[:, 3 * h:3 * h + 1] * oc_ref[:, sl]
        acc = acc + gates[:, 3 * h + 1:3 * h + 2] * os_ref[:, sl]
        acc = acc + gates[:, 3 * h + 2:3 * h + 3] * ow_ref[:, sl]
        o_ref[:, sl] = acc.astype(o_ref.dtype)


def nsa_combine(gate_logits, o_cmp, o_sel, o_win, ts=256):
    B, S, W = o_cmp.shape
    spec = pl.BlockSpec((None, ts, W), lambda b, i: (b, i, 0))
    return pl.pallas_call(
        _nsa_combine_kernel,
        grid=(B, S // ts),
        in_specs=[pl.BlockSpec((None, ts, LANE), lambda b, i: (b, i, 0)), spec, spec, spec],
        out_specs=spec,
        out_shape=jax.ShapeDtypeStruct((B, S, W), BF16),
        compiler_params=_cparams(("parallel", "parallel")),
        name="nsa_combine",
    )(gate_logits, o_cmp, o_sel, o_win)


def _dil_combine_kernel(*refs, dils):
    n = len(dils)
    o_refs, lse_refs, out_ref = refs[:n], refs[n:2 * n], refs[2 * n]
    nat_o, nat_l = refs[2 * n + 1:3 * n + 1], refs[3 * n + 1:]
    ts = out_ref.shape[0]
    for h in range(out_ref.shape[1] // LANE):
        sl = slice(h * LANE, (h + 1) * LANE)
        for g, dil in enumerate(dils):
            for r in range(dil):
                rows = pl.ds(r, ts // dil, stride=dil) if dil > 1 else slice(None)
                nat_o[g][rows, :] = o_refs[g][r, :, sl]
                nat_l[g][rows, :] = lse_refs[g][r, :, sl]
        lses = [r[...] for r in nat_l]
        mx = functools.reduce(jnp.maximum, lses)
        ws = [jnp.exp2(l - mx) for l in lses]
        den = functools.reduce(lambda a, b: a + b, ws)
        num = functools.reduce(lambda a, b: a + b, [w * r[...] for w, r in zip(ws, nat_o)])
        out_ref[:, sl] = (num / den).astype(out_ref.dtype)


def dil_combine(outs, lses, ts=128):
    dils = tuple(o.shape[1] for o in outs)
    B, W = outs[0].shape[0], outs[0].shape[-1]
    S = outs[0].shape[1] * outs[0].shape[2]
    specs = [pl.BlockSpec((None, d, ts // d, W), lambda b, i: (b, 0, i, 0)) for d in dils]
    return pl.pallas_call(
        functools.partial(_dil_combine_kernel, dils=dils),
        grid=(B, S // ts),
        in_specs=specs + specs,
        out_specs=pl.BlockSpec((None, ts, W), lambda b, i: (b, i, 0)),
        out_shape=jax.ShapeDtypeStruct((B, S, W), BF16),
        scratch_shapes=[pltpu.VMEM((ts, LANE), F32)] * (2 * len(dils)),
        compiler_params=_cparams(("parallel", "parallel")),
        name="dil_combine",
    )(*outs, *lses)


def _softplus(x):
    return jnp.maximum(x, 0.0) + jnp.log1p(jnp.exp(-jnp.abs(x)))


def _gate_band_starts():
    ntile = D_RNN // LANE
    starts = []
    for j in range(ntile):
        n_lo = (j * LANE) // RNN_BLOCK_DIM
        n_hi = (j * LANE + LANE - 1) // RNN_BLOCK_DIM
        lo = (n_lo * RNN_BLOCK_DIM) // LANE
        hi = -(-((n_hi + 1) * RNN_BLOCK_DIM) // LANE)
        assert hi - lo <= 4
        starts.append(min(lo, ntile - 4))
    return starts


def _rglru_kernel(y_ref, xr_ref, cw_ref, cb_ref, wg_ref, bg_ref, lam_ref, o_ref,
                  h_ref, tail_ref, *, ts, starts):
    i = pl.program_id(1)

    @pl.when(i == 0)
    def _():
        h_ref[...] = jnp.zeros_like(h_ref)
        tail_ref[...] = jnp.zeros_like(tail_ref)

    xr = xr_ref[...]
    ext = jnp.concatenate([tail_ref[...], xr], axis=0)
    x = cb_ref[...] + xr * cw_ref[CONV_WIDTH - 1:CONV_WIDTH, :]
    for d in range(1, CONV_WIDTH):
        shifted = pltpu.roll(ext, d, 0)[8:8 + ts]
        x = x + shifted * cw_ref[CONV_WIDTH - 1 - d:CONV_WIDTH - d, :]
    tail_ref[...] = xr[ts - 8:ts]

    xb = x.astype(BF16)
    gl = []
    for g in range(2):
        tiles = [jnp.dot(xb[:, a * LANE:(a + 4) * LANE], wg_ref[g, j],
                         preferred_element_type=F32) for j, a in enumerate(starts)]
        gl.append(jnp.concatenate(tiles, axis=1) + bg_ref[g:g + 1, :])
    r = jax.nn.sigmoid(gl[0])
    ig = jax.nn.sigmoid(gl[1])
    log_a = (-LRU_C) * r * _softplus(-lam_ref[...])
    a = jnp.exp(log_a)
    b = jnp.sqrt(-jnp.tanh(log_a) * (a * a + 1.0)) * (ig * x)

    row = lax.broadcasted_iota(jnp.int32, a.shape, 0)
    d = 1
    while d < ts:
        keep = row >= d
        b = b + a * jnp.where(keep, pltpu.roll(b, d, 0), 0.0)
        a = a * jnp.where(keep, pltpu.roll(a, d, 0), 1.0)
        d *= 2
    h = a * h_ref[0:1, :] + b
    h_ref[0:1, :] = h[ts - 1:ts, :]
    o_ref[...] = (h * jax.nn.gelu(y_ref[...])).astype(o_ref.dtype)


def rglru_scan(proj, conv_w, conv_b, wband, b_gate, lam, ts=128):
    B, S, _ = proj.shape
    C = D_RNN
    starts = _gate_band_starts()
    vec = lambda n: pl.BlockSpec((n, C), lambda b, i: (0, 0))
    return pl.pallas_call(
        functools.partial(_rglru_kernel, ts=ts, starts=starts),
        grid=(B, S // ts),
        in_specs=[pl.BlockSpec((None, ts, C), lambda b, i: (b, i, 0)),
                  pl.BlockSpec((None, ts, C), lambda b, i: (b, i, 1)),
                  vec(CONV_WIDTH), vec(1),
                  pl.BlockSpec(wband.shape, lambda b, i: (0, 0, 0, 0)),
                  vec(2), vec(1)],
        out_specs=pl.BlockSpec((None, ts, C), lambda b, i: (b, i, 0)),
        out_shape=jax.ShapeDtypeStruct((B, S, C), BF16),
        scratch_shapes=[pltpu.VMEM((8, C), F32), pltpu.VMEM((8, C), F32)],
        compiler_params=_cparams(("arbitrary", "arbitrary")),
        name="rglru_scan",
    )(proj, proj, conv_w, conv_b.reshape(1, C), wband, b_gate, lam.reshape(1, C))


def _gate_band_weights(w_gate):
    starts = _gate_band_starts()
    dense = jnp.stack([jax.scipy.linalg.block_diag(*[w_gate[g, n] for n in range(RNN_BLOCKS)])
                       for g in range(2)])
    tiles = [dense[:, a * LANE:(a + 4) * LANE, j * LANE:(j + 1) * LANE]
             for j, a in enumerate(starts)]
    return jnp.stack(tiles, axis=1).astype(BF16)


def _row_tile(T):
    return 512 if T % 512 == 0 else T


RESIDUAL = dict(epilogue="residual")


def out_proj(a, w, x, mod, gate_blk, tk, tail):
    B, S, D = x.shape
    T = B * S
    out = matmul(a.reshape(T, -1), w.astype(BF16), tm=_row_tile(T), tn=D, tk=tk,
                 res=x.reshape(T, D), mod=mod, gate_blk=gate_blk, rows_per_batch=S, **tail)
    if isinstance(out, (list, tuple)):
        return tuple(o.reshape(B, S, D) for o in out)
    return out.reshape(B, S, D)


def nsa_mixer(hn, x, mod, gate_blk, w_in, cmp_pe, cmp_w1, cmp_w2, w_out, cos2, sin2,
              tail=RESIDUAL):
    B, S, D = x.shape
    T = B * S
    G = NSA_KV_GROUPS
    hq = NSA_HEADS * HEAD_DIM
    kvw = NSA_KV_WIDTH
    main = hq + 6 * kvw
    tm = _row_tile(T)
    hn2 = hn.reshape(T, D)
    w_gate = jnp.pad(w_in[:, main:], ((0, 0), (0, LANE - 3 * NSA_HEADS))).astype(BF16)
    gate_logits = matmul(hn2, w_gate, tm=tm, tn=LANE, tk=D).reshape(B, S, LANE)

    q_cmp, q_rot = proj_heads(hn2, w_in[:, :hq].astype(BF16), cos2, sin2,
                              [(SCALE_Q,) * NSA_HEADS, (ROPE_Q,) * NSA_HEADS], tm=tm)
    g_rope, g_cast = (ROPE,) * G, (CAST,) * G
    (kvb,) = proj_heads(hn2, w_in[:, hq + 2 * kvw:main].astype(BF16), cos2, sin2,
                        [g_rope + g_cast + g_rope + g_cast], tm=tm)
    q_rot = q_rot.reshape(B, 1, S, hq)
    kvb = kvb.reshape(B, 1, S, 4 * kvw)

    npiece = S // CMP_STRIDE
    kv_cmp = matmul(hn2, w_in[:, hq:hq + 2 * kvw].astype(BF16), tm=tm, tn=2 * kvw, tk=D)
    kv_cmp = kv_cmp.reshape(B, npiece, CMP_STRIDE, 2, G, HEAD_DIM)
    pieces = kv_cmp.transpose(3, 0, 4, 1, 2, 5).reshape(2, B, G, npiece, CMP_STRIDE * HEAD_DIM)
    kvc = compress(pieces, cmp_pe.reshape(2, 2, CMP_STRIDE * HEAD_DIM),
                   cmp_w1.astype(BF16), cmp_w2.astype(BF16))

    n_sb = S // SEL_BLOCK
    cmp_start = np.arange(npiece) * CMP_STRIDE
    sel_start = np.arange(n_sb) * SEL_BLOCK
    overlap = ((cmp_start[:, None] < sel_start[None, :] + SEL_BLOCK)
               & (cmp_start[:, None] + CMP_BLOCK > sel_start[None, :]))
    overlap[npiece - 1] = False
    overlap_t = jnp.asarray(overlap.T, BF16)
    onehot = jnp.asarray(np.arange(S)[:, None] // SEL_BLOCK == np.arange(n_sb)[None, :], BF16)

    o_cmp, imp_t = cmp_attention(q_cmp.reshape(B, S, hq), kvc, overlap_t)
    mfeat = topk_select(imp_t)
    kb = kvw // LANE
    o_sel = sel_attention(q_rot.reshape(B, S, hq), mfeat, kvb.reshape(B, S, 4 * kvw),
                          onehot, k_blk=0, v_blk=kb)
    (o_win,) = banded_attention(q_rot, kvb, kvb, n_kv=G, R=NSA_REP, k_blk=2 * kb, v_blk=3 * kb,
                                nwin=(NSA_WINDOW - 1 + LANE - 1) // LANE,
                                window=NSA_WINDOW - 1, want_lse=False)
    o = nsa_combine(gate_logits, o_cmp, o_sel, o_win.reshape(B, S, hq))
    return out_proj(o, w_out, x, mod, gate_blk, hq, tail)


def dilated_mixer(hn, x, mod, gate_blk, w_in, w_out, cos2, sin2, tail=RESIDUAL):
    B, S, D = x.shape
    T = B * S
    H = DIL_HEADS
    hw = H * HEAD_DIM
    tm = _row_tile(T)
    outs, lses = [], []
    for g, (window, dil) in enumerate(DIL_PATTERNS):
        w = window // dil
        L = S // dil
        regroup = lambda t: t.reshape(t.shape[:-2] + (L, dil, t.shape[-1])).swapaxes(-3, -2)
        hn_g = regroup(hn).reshape(T, D)
        cos_g, sin_g = regroup(cos2).reshape(S, LANE), regroup(sin2).reshape(S, LANE)
        q, k, v = [proj_heads(hn_g, w_in[:, (3 * g + c) * hw:(3 * g + c + 1) * hw].astype(BF16),
                              cos_g, sin_g, [(kind,) * H], tm=tm)[0].reshape(B, dil, L, hw)
                   for c, kind in enumerate((ROPE_Q, ROPE, CAST))]
        o, lse = banded_attention(q, k, v, n_kv=H, R=1, nwin=(w + LANE - 1) // LANE, window=w,
                                  want_lse=True)
        outs.append(o)
        lses.append(lse)
    o = dil_combine(outs, lses)
    return out_proj(o, w_out, x, mod, gate_blk, hw, tail)


def rglru_mixer(hn, x, mod, gate_blk, w_in, conv_w, conv_b, w_gate, b_gate, lam, w_out,
                tail=RESIDUAL):
    B, S, D = x.shape
    T = B * S
    tm = _row_tile(T)
    proj = matmul(hn.reshape(T, D), w_in.astype(BF16), tm=tm, tn=D_RNN, tk=D)
    hy = rglru_scan(proj.reshape(B, S, 2 * D_RNN), conv_w, conv_b, _gate_band_weights(w_gate),
                    b_gate, lam)
    return out_proj(hy, w_out, x, mod, gate_blk, D_RNN, tail)


def mlp(hn, x, mod, gate_blk, w1, w2, tail=RESIDUAL):
    B, S, D = x.shape
    T = B * S
    h = matmul(hn.reshape(T, D), w1.astype(BF16), tm=_row_tile(T), tn=1024, tk=D, out_dtype=BF16,
               epilogue="relu2")
    return out_proj(h, w2, x, mod, gate_blk, 2048, tail)


def kernel(x, c, l0_w_ada, l0_b_ada, l0_norm1, l0_w_in, l0_cmp_pe, l0_cmp_w1, l0_cmp_w2, l0_w_out, l0_norm2, l0_w_ff1, l0_w_ff2, l1_w_ada, l1_b_ada, l1_norm1, l1_w_in, l1_w_out, l1_norm2, l1_w_ff1, l1_w_ff2, l2_w_ada, l2_b_ada, l2_norm1, l2_w_in, l2_conv_w, l2_conv_b, l2_w_gate, l2_b_gate, l2_lambda, l2_w_out, l2_norm2, l2_w_ff1, l2_w_ff2, l3_w_ada, l3_b_ada, l3_norm1, l3_w_in, l3_cmp_pe, l3_cmp_w1, l3_cmp_w2, l3_w_out, l3_norm2, l3_w_ff1, l3_w_ff2, norm_f):
    B, S, D = x.shape
    cos2, sin2 = rope_tables(S)
    layers = (
        (l0_w_ada, l0_b_ada, l0_norm1, l0_norm2, l0_w_ff1, l0_w_ff2,
         (l0_w_in, l0_cmp_pe, l0_cmp_w1, l0_cmp_w2, l0_w_out)),
        (l1_w_ada, l1_b_ada, l1_norm1, l1_norm2, l1_w_ff1, l1_w_ff2, (l1_w_in, l1_w_out)),
        (l2_w_ada, l2_b_ada, l2_norm1, l2_norm2, l2_w_ff1, l2_w_ff2,
         (l2_w_in, l2_conv_w, l2_conv_b, l2_w_gate, l2_b_gate, l2_lambda, l2_w_out)),
        (l3_w_ada, l3_b_ada, l3_norm1, l3_norm2, l3_w_ff1, l3_w_ff2,
         (l3_w_in, l3_cmp_pe, l3_cmp_w1, l3_cmp_w2, l3_w_out)),
    )
    mods = [adaln(c, l[0], l[1]).reshape(B, 1, 6 * D) for l in layers]
    hn = modulate(x, layers[0][2], mods[0], 0, 1)
    for li in range(DEPTH):
        _, _, _, n2, ff1, ff2, mix = layers[li]
        mod = mods[li]
        tail = dict(epilogue="residual_norm", norm_gain=n2, norm_mod=mod, shift_blk=3, scale_blk=4)
        kind = li % N_MIXERS
        if kind == 0:
            x, hn = nsa_mixer(hn, x, mod, 2, *mix, cos2, sin2, tail=tail)
        elif kind == 1:
            x, hn = dilated_mixer(hn, x, mod, 2, *mix, cos2, sin2, tail=tail)
        else:
            x, hn = rglru_mixer(hn, x, mod, 2, *mix, tail=tail)
        if li + 1 < DEPTH:
            tail = dict(epilogue="residual_norm", norm_gain=layers[li + 1][2],
                        norm_mod=mods[li + 1], shift_blk=0, scale_blk=1)
            x, hn = mlp(hn, x, mod, 5, ff1, ff2, tail=tail)
        else:
            return mlp(hn, x, mod, 5, ff1, ff2,
                       tail=dict(epilogue="residual_final", norm_gain=norm_f))
```

```python
import functools
import math

import jax
import jax.numpy as jnp
import numpy as np
from jax import lax
from jax.experimental import pallas as pl
from jax.experimental.pallas import tpu as pltpu

F32 = jnp.float32
BF16 = jnp.bfloat16

D_MODEL = 2048
DEPTH = 4
N_MIXERS = 3
HEAD_DIM = 128
ROPE_THETA = 10000.0
NORM_EPS = 1e-6

NSA_HEADS = D_MODEL // HEAD_DIM
NSA_KV_GROUPS = 4
NSA_REP = NSA_HEADS // NSA_KV_GROUPS
NSA_KV_WIDTH = NSA_KV_GROUPS * HEAD_DIM
CMP_BLOCK = 32
CMP_STRIDE = 16
CMP_HIDDEN = 4 * HEAD_DIM
SEL_BLOCK = 64
SEL_TOPK = 16
NSA_WINDOW = 512
FORCED_BONUS = 1e9

DIL_HEADS = D_MODEL // HEAD_DIM
DIL_PATTERNS = ((128, 1), (512, 4), (2048, 16))

D_RNN = 2688
RNN_BLOCKS = 16
RNN_BLOCK_DIM = D_RNN // RNN_BLOCKS
CONV_WIDTH = 4
LRU_C = 8.0

LANE = 128
LOG2E = math.log2(math.e)
QK_SCALE = LOG2E / math.sqrt(HEAD_DIM)
NEG_BIG = -1e30
SEL_OFF = -float(2 ** 30)
VMEM_LIMIT = 56 * 1024 * 1024

NT_DIMS = (((1,), (1,)), ((), ()))


def _cparams(sem):
    return pltpu.CompilerParams(dimension_semantics=sem, vmem_limit_bytes=VMEM_LIMIT)


def _adaln_kernel(c_ref, w_ref, b_ref, o_ref):
    w = w_ref[...]
    for b in range(c_ref.shape[0]):
        c = c_ref[b]
        cond = c * jax.nn.sigmoid(c)
        o_ref[b:b + 1, :] = jnp.sum(w * cond, axis=0, keepdims=True) + b_ref[...]


def adaln(c, w_ada, b_ada, tn=1024):
    B, D = c.shape
    N = w_ada.shape[1]
    return pl.pallas_call(
        _adaln_kernel,
        grid=(N // tn,),
        in_specs=[pl.BlockSpec((B, D, 1), lambda j: (0, 0, 0)),
                  pl.BlockSpec((D, tn), lambda j: (0, j)),
                  pl.BlockSpec((1, tn), lambda j: (0, j))],
        out_specs=pl.BlockSpec((B, tn), lambda j: (0, j)),
        out_shape=jax.ShapeDtypeStruct((B, N), F32),
        compiler_params=_cparams(("parallel",)),
        name="adaln",
    )(c.reshape(B, D, 1), w_ada, b_ada.reshape(1, N))


def _modulate_kernel(x_ref, gain_ref, sh_ref, sc_ref, o_ref):
    x = x_ref[...]
    ms = jnp.mean(x * x, axis=-1, keepdims=True)
    y = x * lax.rsqrt(ms + NORM_EPS) * gain_ref[...]
    o_ref[...] = (y * (1.0 + sc_ref[...]) + sh_ref[...]).astype(o_ref.dtype)


def modulate(x, gain, mod, shift_blk, scale_blk, ts=512):
    B, S, D = x.shape
    return pl.pallas_call(
        _modulate_kernel,
        grid=(B, S // ts),
        in_specs=[pl.BlockSpec((None, ts, D), lambda b, i: (b, i, 0)),
                  pl.BlockSpec((1, D), lambda b, i: (0, 0)),
                  pl.BlockSpec((None, 1, D), lambda b, i: (b, 0, shift_blk)),
                  pl.BlockSpec((None, 1, D), lambda b, i: (b, 0, scale_blk))],
        out_specs=pl.BlockSpec((None, ts, D), lambda b, i: (b, i, 0)),
        out_shape=jax.ShapeDtypeStruct((B, S, D), BF16),
        compiler_params=_cparams(("parallel", "parallel")),
        name="modulate",
    )(x, gain.reshape(1, D), mod, mod)


def _mm_kernel(*refs, nk, epilogue):
    refs = list(refs)
    a_ref, w_ref = refs[:2]
    del refs[:2]
    if epilogue.startswith("residual"):
        res_ref, gate_ref = refs[:2]
        del refs[:2]
    if epilogue == "residual_norm":
        gain_ref, sh_ref, sc_ref = refs[:3]
        del refs[:3]
    elif epilogue == "residual_final":
        gain_ref = refs.pop(0)
    o_ref = refs.pop(0)
    hn_ref = refs.pop(0) if epilogue == "residual_norm" else None
    rest = refs

    def finish(acc):
        if epilogue == "relu2":
            r = jnp.maximum(acc, 0.0)
            acc = r * r
        elif epilogue.startswith("residual"):
            acc = res_ref[...] + gate_ref[...] * acc
        if epilogue in ("residual_norm", "residual_final"):
            ms = jnp.mean(acc * acc, axis=-1, keepdims=True)
            y = acc * lax.rsqrt(ms + NORM_EPS) * gain_ref[...]
            if epilogue == "residual_final":
                o_ref[...] = y
                return
            hn_ref[...] = (y * (1.0 + sc_ref[...]) + sh_ref[...]).astype(hn_ref.dtype)
        o_ref[...] = acc.astype(o_ref.dtype)

    def part():
        return jnp.dot(a_ref[...], w_ref[...], preferred_element_type=F32)

    if nk == 1:
        finish(part())
        return
    acc_ref = rest[0]
    k = pl.program_id(2)

    @pl.when(k == 0)
    def _():
        acc_ref[...] = part()

    if nk > 2:
        @pl.when(jnp.logical_and(k > 0, k < nk - 1))
        def _():
            acc_ref[...] += part()

    @pl.when(k == nk - 1)
    def _():
        finish(acc_ref[...] + part())


def matmul(a, w, *, tm, tn, tk, out_dtype=F32, epilogue="none", res=None, mod=None,
           gate_blk=0, rows_per_batch=None, norm_gain=None, norm_mod=None, shift_blk=0,
           scale_blk=0):
    M, K = a.shape
    N = w.shape[1]
    nk = K // tk
    assert M % tm == 0 and N % tn == 0 and K % tk == 0
    in_specs = [pl.BlockSpec((tm, tk), lambda i, j, k: (i, k)),
                pl.BlockSpec((tk, tn), lambda i, j, k: (k, j))]
    args = [a, w]
    out_spec = pl.BlockSpec((tm, tn), lambda i, j, k: (i, j))
    out_specs, out_shape = out_spec, jax.ShapeDtypeStruct((M, N), out_dtype)
    if epilogue.startswith("residual"):
        assert rows_per_batch % tm == 0 and tn == D_MODEL == N
        batch = lambda i: i * tm // rows_per_batch
        chunk = lambda blk: pl.BlockSpec((None, 1, tn), lambda i, j, k: (batch(i), 0, blk))
        in_specs += [out_spec, chunk(gate_blk)]
        args += [res, mod]
        if epilogue != "residual":
            in_specs.append(pl.BlockSpec((1, tn), lambda i, j, k: (0, 0)))
            args.append(norm_gain.reshape(1, tn))
        if epilogue == "residual_norm":
            in_specs += [chunk(shift_blk), chunk(scale_blk)]
            args += [norm_mod, norm_mod]
            out_specs = [out_spec, out_spec]
            out_shape = [out_shape, jax.ShapeDtypeStruct((M, N), BF16)]
    scratch = [pltpu.VMEM((tm, tn), F32)] if nk > 1 else []
    return pl.pallas_call(
        functools.partial(_mm_kernel, nk=nk, epilogue=epilogue),
        grid=(M // tm, N // tn, nk),
        in_specs=in_specs,
        out_specs=out_specs,
        out_shape=out_shape,
        scratch_shapes=scratch,
        compiler_params=_cparams(("parallel", "parallel", "arbitrary")),
        name="mm_" + epilogue,
    )(*args)


def rope_tables(S):
    inv_freq = ROPE_THETA ** (-jnp.arange(0, HEAD_DIM, 2, dtype=F32) / HEAD_DIM)
    ang = jnp.arange(S, dtype=F32)[:, None] * inv_freq[None, :]
    cos, sin = jnp.cos(ang), jnp.sin(ang)
    return jnp.concatenate([cos, cos], axis=-1), jnp.concatenate([-sin, sin], axis=-1)


CAST, ROPE, ROPE_Q, SCALE_Q = 0, 1, 2, 3


def _proj_heads_kernel(a_ref, w_ref, c_ref, s_ref, *o_refs, kinds):
    if len(o_refs) > len(kinds):
        o_refs, wb_ref = o_refs[:-1], o_refs[-1]

        @pl.when(pl.program_id(0) == 0)
        def _():
            wb_ref[...] = w_ref[...].astype(BF16)
        w_ref = wb_ref
    acc = jnp.dot(a_ref[...], w_ref[...], preferred_element_type=F32)
    c = c_ref[...]
    s = s_ref[...]
    for o_ref, head_kinds in zip(o_refs, kinds):
        for h, kind in enumerate(head_kinds):
            sl = slice(h * LANE, (h + 1) * LANE)
            t = acc[:, sl]
            if kind in (ROPE, ROPE_Q):
                t = t * c + pltpu.roll(t, HEAD_DIM // 2, 1) * s
            if kind in (ROPE_Q, SCALE_Q):
                t = t * QK_SCALE
            o_ref[:, sl] = t.astype(o_ref.dtype)


def proj_heads(a, w, cos2, sin2, kinds, *, tm, col_blk=0):
    M, K = a.shape
    N = len(kinds[0]) * LANE
    nt = cos2.shape[0] // tm
    tab = pl.BlockSpec((tm, LANE), lambda i: (i % nt, 0))
    out_spec = pl.BlockSpec((tm, N), lambda i: (i, 0))
    cast = w.dtype != BF16
    return pl.pallas_call(
        functools.partial(_proj_heads_kernel, kinds=tuple(kinds)),
        grid=(M // tm,),
        in_specs=[pl.BlockSpec((tm, K), lambda i: (i, 0)),
                  pl.BlockSpec((K, N), lambda i: (0, col_blk), pipeline_mode=pl.Buffered(1)),
                  tab, tab],
        out_specs=[out_spec] * len(kinds),
        out_shape=[jax.ShapeDtypeStruct((M, N), BF16)] * len(kinds),
        scratch_shapes=[pltpu.VMEM((K, N), BF16)] if cast else [],
        compiler_params=_cparams(("arbitrary",)),
        name="proj_heads",
    )(a, w, cos2, sin2)


def _ones_column(n):
    return jnp.where(lax.broadcasted_iota(jnp.int32, (n, LANE), 1) == 0, 1.0, 0.0).astype(BF16)


def _softmax_pv(s, m, v, ones):
    p = jnp.exp2(s - m).astype(BF16)
    res = jnp.dot(p, jnp.concatenate([v, ones], axis=1), preferred_element_type=F32)
    return res[:, LANE:LANE + 1], res[:, :LANE]


def _banded_kernel(q_ref, k_ref, v_ref, o_ref, *lse_refs, R, hb, nwin, window, tq, qsub, nkeys):
    i = pl.program_id(3)
    L = k_ref.shape[0]
    rows = R * qsub
    rel = (lax.broadcasted_iota(jnp.int32, (rows, nkeys), 0) & (qsub - 1)) \
        - lax.broadcasted_iota(jnp.int32, (rows, nkeys), 1)
    ones = _ones_column(nkeys)
    units = [(h, j) for h in range(hb) for j in range(tq // qsub)]

    def scores(h, j):
        q0 = i * tq + j * qsub
        kstart = pl.multiple_of(jnp.clip(q0 - nwin * LANE, 0, L - nkeys), LANE)
        qj = q_ref[j * qsub:(j + 1) * qsub, h * R * LANE:(h + 1) * R * LANE]
        if R > 1:
            q = jnp.concatenate([qj[:, r * LANE:(r + 1) * LANE] for r in range(R)], axis=0)
        else:
            q = qj
        s = lax.dot_general(q, k_ref[pl.ds(kstart, nkeys), h * LANE:(h + 1) * LANE], NT_DIMS,
                            preferred_element_type=F32)
        return q0, kstart, s

    ahead = 2
    pending = [scores(*u) for u in units[:ahead]]
    for n, (h, j) in enumerate(units):
        if n + ahead < len(units):
            pending.append(scores(*units[n + ahead]))
        q0, kstart, s = pending[n]
        v = v_ref[pl.ds(kstart, nkeys), h * LANE:(h + 1) * LANE]
        diff = rel + (q0 - kstart)
        valid = lax.bitcast_convert_type(diff, jnp.uint32) <= jnp.uint32(window)
        s = jnp.where(valid, s, NEG_BIG)
        m = jnp.max(s, axis=-1, keepdims=True)
        l, o = _softmax_pv(s, m, v, ones)
        o = o / l
        for r in range(R):
            col = (h * R + r) * LANE
            o_ref[j * qsub:(j + 1) * qsub, col:col + LANE] = o[r * qsub:(r + 1) * qsub]
        if lse_refs:
            lse_refs[0][j * qsub:(j + 1) * qsub, h * LANE:(h + 1) * LANE] = jnp.broadcast_to(
                m + jnp.log(l) * LOG2E, (rows, LANE))


def banded_attention(q_arr, k_arr, v_arr, *, n_kv, R, nwin, window, want_lse, q_blk=0, k_blk=0,
                     v_blk=0):
    B, dil, L, _ = q_arr.shape
    tq = min(512, L)
    qsub = LANE
    nkeys = min(qsub + nwin * LANE, L)
    hb = 4 if R == 1 else 1
    assert L % tq == 0 and n_kv % hb == 0
    assert q_blk % (hb * R) == 0 and k_blk % hb == 0 and v_blk % hb == 0
    width = n_kv * R * LANE
    out_shape = [jax.ShapeDtypeStruct((B, dil, L, width), F32)]
    out_specs = [pl.BlockSpec((None, None, tq, hb * R * LANE), lambda b, r, h, i: (b, r, i, h))]
    if want_lse:
        out_shape.append(out_shape[0])
        out_specs.append(out_specs[0])
    kv_spec = lambda blk: pl.BlockSpec((None, None, L, hb * LANE),
                                       lambda b, r, h, i: (b, r, 0, blk // hb + h))
    return pl.pallas_call(
        functools.partial(_banded_kernel, R=R, hb=hb, nwin=nwin, window=window, tq=tq, qsub=qsub,
                          nkeys=nkeys),
        grid=(B, dil, n_kv // hb, L // tq),
        in_specs=[pl.BlockSpec((None, None, tq, hb * R * LANE),
                               lambda b, r, h, i: (b, r, i, q_blk // (hb * R) + h)),
                  kv_spec(k_blk), kv_spec(v_blk)],
        out_specs=out_specs,
        out_shape=out_shape,
        compiler_params=_cparams(("parallel", "parallel", "parallel", "arbitrary")),
        name="banded_attn",
    )(q_arr, k_arr, v_arr)


def _compress_kernel(x_ref, pe_ref, w1_ref, w2_ref, o_ref):
    n = x_ref.shape[0] // CMP_STRIDE
    a = jnp.zeros((n, CMP_HIDDEN), F32)
    b = jnp.zeros((n, CMP_HIDDEN), F32)
    for j in range(CMP_STRIDE):
        xj = x_ref[pl.ds(j, n, stride=CMP_STRIDE), :]
        top = (xj + pe_ref[j:j + 1, :]).astype(BF16)
        bot = (xj + pe_ref[CMP_STRIDE + j:CMP_STRIDE + j + 1, :]).astype(BF16)
        a = a + jnp.dot(top, w1_ref[j * HEAD_DIM:(j + 1) * HEAD_DIM, :],
                        preferred_element_type=F32)
        b = b + jnp.dot(bot, w1_ref[(CMP_STRIDE + j) * HEAD_DIM:(CMP_STRIDE + j + 1) * HEAD_DIM, :],
                        preferred_element_type=F32)
    hid = a + pltpu.roll(b, n - 1, 0)
    o_ref[...] = jnp.dot(jax.nn.gelu(hid).astype(BF16), w2_ref[...], preferred_element_type=F32)


def compress(kv, pe, w1, w2):
    B, S, _ = kv.shape
    G = NSA_KV_GROUPS
    NP = S // CMP_STRIDE
    return pl.pallas_call(
        _compress_kernel,
        grid=(2, B, G),
        in_specs=[pl.BlockSpec((None, S, HEAD_DIM), lambda t, b, g: (b, 0, t * G + g)),
                  pl.BlockSpec((None, CMP_BLOCK, HEAD_DIM), lambda t, b, g: (t, 0, 0)),
                  pl.BlockSpec((None, CMP_BLOCK * HEAD_DIM, CMP_HIDDEN), lambda t, b, g: (t, 0, 0)),
                  pl.BlockSpec((None, CMP_HIDDEN, HEAD_DIM), lambda t, b, g: (t, 0, 0))],
        out_specs=pl.BlockSpec((None, None, None, NP, HEAD_DIM), lambda t, b, g: (t, b, g, 0, 0)),
        out_shape=jax.ShapeDtypeStruct((2, B, G, NP, HEAD_DIM), F32),
        compiler_params=_cparams(("parallel", "parallel", "parallel")),
        name="nsa_compress",
    )(kv, pe, w1, w2)


def _cmp_attn_kernel(q_ref, kc_ref, vc_ref, ov_ref, o_ref, imp_ref, *, tq, qsub):
    i = pl.program_id(2)
    R = NSA_REP
    nsub = tq // qsub
    rows = R * qsub
    n_var = kc_ref.shape[0] // LANE

    def run(nk):
        kc = kc_ref[0:nk, :].astype(BF16)
        vc = vc_ref[0:nk, :].astype(BF16)
        ov = ov_ref[:, 0:nk]
        tloc = lax.broadcasted_iota(jnp.int32, (rows, nk), 0) & (qsub - 1)
        cmp_end = lax.broadcasted_iota(jnp.int32, (rows, nk), 1) * CMP_STRIDE + (CMP_BLOCK - 1)

        def scores(u):
            qu = q_ref[u * qsub:(u + 1) * qsub, :]
            q = jnp.concatenate([qu[:, r * LANE:(r + 1) * LANE] for r in range(R)], axis=0)
            return lax.dot_general(q, kc, NT_DIMS, preferred_element_type=F32)

        ahead = 2
        pending = [scores(u) for u in range(min(ahead, nsub))]
        for u in range(nsub):
            if u + ahead < nsub:
                pending.append(scores(u + ahead))
            valid = cmp_end <= tloc + (i * tq + u * qsub)
            s = jnp.where(valid, pending[u], NEG_BIG)
            m = jnp.max(s, axis=-1, keepdims=True)
            p = jnp.where(valid, jnp.exp2(s - m), 0.0)
            l = jnp.sum(p, axis=-1, keepdims=True)
            p = p / jnp.where(l > 0, l, 1.0)
            o = jnp.dot(p.astype(BF16), vc, preferred_element_type=F32)
            for r in range(R):
                o_ref[u * qsub:(u + 1) * qsub, r * LANE:(r + 1) * LANE] = \
                    o[r * qsub:(r + 1) * qsub]
            psum = p[0:qsub]
            for r in range(1, R):
                psum = psum + p[r * qsub:(r + 1) * qsub]
            p_hi = psum.astype(BF16)
            p_lo = (psum - p_hi.astype(F32)).astype(BF16)
            imp_ref[:, u * qsub:(u + 1) * qsub] = (
                lax.dot_general(ov, p_hi, NT_DIMS, preferred_element_type=F32)
                + lax.dot_general(ov, p_lo, NT_DIMS, preferred_element_type=F32))

    need = ((i + 1) * tq - CMP_BLOCK) // CMP_STRIDE + 1
    var = jnp.clip((need - 1) // LANE, 0, n_var - 1)
    for v in range(n_var):
        pl.when(var == v)(functools.partial(run, (v + 1) * LANE))


def cmp_attention(q, kvc, overlap_t, tq=512, qsub=128):
    B, S, _ = q.shape
    G = NSA_KV_GROUPS
    NP = kvc.shape[3]
    n_sel = overlap_t.shape[0]
    qw = NSA_REP * LANE
    return pl.pallas_call(
        functools.partial(_cmp_attn_kernel, tq=tq, qsub=qsub),
        grid=(B, G, S // tq),
        in_specs=[pl.BlockSpec((None, tq, qw), lambda b, g, i: (b, i, g)),
                  pl.BlockSpec((None, None, None, NP, LANE), lambda b, g, i: (0, b, g, 0, 0)),
                  pl.BlockSpec((None, None, None, NP, LANE), lambda b, g, i: (1, b, g, 0, 0)),
                  pl.BlockSpec((n_sel, NP), lambda b, g, i: (0, 0))],
        out_specs=[pl.BlockSpec((None, tq, qw), lambda b, g, i: (b, i, g)),
                   pl.BlockSpec((None, None, n_sel, tq), lambda b, g, i: (b, g, 0, i))],
        out_shape=[jax.ShapeDtypeStruct((B, S, NSA_HEADS * LANE), F32),
                   jax.ShapeDtypeStruct((B, G, n_sel, S), F32)],
        compiler_params=_cparams(("parallel", "parallel", "parallel")),
        name="nsa_cmp_attn",
    )(q, kvc, kvc, overlap_t)


def _topk_kernel(imp_ref, o_ref, *, tq):
    i = pl.program_id(2)
    imp = imp_ref[...]
    n_sel = imp.shape[0]
    blk = lax.broadcasted_iota(jnp.int32, imp.shape, 0)
    t = i * tq + lax.broadcasted_iota(jnp.int32, imp.shape, 1)
    cur = t // SEL_BLOCK
    avail = blk <= cur
    forced = jnp.where(blk == 0, 1.0, jnp.where(blk == cur, 1.0, jnp.where(blk == cur - 1, 1.0, 0.0)))
    score = jnp.where(avail, imp + FORCED_BONUS * forced, -jnp.inf)
    picked = jnp.zeros(imp.shape, F32)
    for _ in range(min(SEL_TOPK, n_sel)):
        mx = jnp.max(score, axis=0, keepdims=True)
        first = jnp.min(jnp.where(score == mx, blk, n_sel), axis=0, keepdims=True)
        hit = blk == first
        picked = jnp.where(hit, 1.0, picked)
        score = jnp.where(hit, -jnp.inf, score)
    feat = jnp.where(avail, jnp.where(picked > 0.0, 0.0, SEL_OFF), SEL_OFF)
    o_ref[...] = feat.T.astype(o_ref.dtype)


def topk_select(imp_t, tq=1024):
    B, G, n_sel, S = imp_t.shape
    tq = min(tq, S)
    return pl.pallas_call(
        functools.partial(_topk_kernel, tq=tq),
        grid=(B, G, S // tq),
        in_specs=[pl.BlockSpec((None, None, n_sel, tq), lambda b, g, i: (b, g, 0, i))],
        out_specs=pl.BlockSpec((None, None, tq, n_sel), lambda b, g, i: (b, g, i, 0)),
        out_shape=jax.ShapeDtypeStruct((B, G, S, n_sel), BF16),
        compiler_params=_cparams(("parallel", "parallel", "parallel")),
        name="nsa_topk",
    )(imp_t)


def _sel_attn_kernel(q_ref, mf_ref, k_ref, e_ref, vt_ref, o_ref, *, tq, qsub, tkv):
    i = pl.program_id(2)
    R = NSA_REP
    nsub = tq // qsub
    rows = R * qsub
    q0 = i * tq
    ntile = (q0 + tq - 1) // tkv + 1
    rel = (lax.broadcasted_iota(jnp.int32, (tkv, rows), 1) & (qsub - 1)) \
        - lax.broadcasted_iota(jnp.int32, (tkv, rows), 0)

    qa = []
    for u in range(nsub):
        qu = q_ref[u * qsub:(u + 1) * qsub, :]
        mf = mf_ref[u * qsub:(u + 1) * qsub, :]
        qa.append(jnp.concatenate(
            [jnp.concatenate([qu[:, r * LANE:(r + 1) * LANE], mf], axis=1) for r in range(R)],
            axis=0))

    def body(j, carry, masked):
        ks = pl.multiple_of(j * tkv, tkv)
        ka = jnp.concatenate([k_ref[pl.ds(ks, tkv), :], e_ref[pl.ds(ks, tkv), :]], axis=1)
        vt = vt_ref[:, pl.ds(ks, tkv)]
        out = []
        scores = lambda u: lax.dot_general(ka, qa[u], NT_DIMS, preferred_element_type=F32)
        ahead = 2
        sts = [scores(u) for u in range(min(ahead, nsub))]
        for u in range(nsub):
            m, l, acc = carry[u]
            if u + ahead < nsub:
                sts.append(scores(u + ahead))
            st = sts[u]
            if masked:
                st = jnp.where(rel >= ks - (q0 + u * qsub), st, NEG_BIG)
            m_new = jnp.maximum(m, jnp.max(st, axis=0, keepdims=True))
            a = jnp.exp2(m - m_new)
            res = jnp.dot(vt, jnp.exp2(st - m_new).astype(BF16), preferred_element_type=F32)
            out.append((m_new, a * l + res[LANE:LANE + 1, :], a * acc + res[:LANE, :]))
        return tuple(out)

    init = tuple((jnp.full((1, rows), NEG_BIG, F32), jnp.zeros((1, rows), F32),
                  jnp.zeros((LANE, rows), F32)) for _ in range(nsub))
    carry = lax.fori_loop(0, ntile - 1, lambda j, c: body(j, c, False), init)
    final = body(ntile - 1, carry, True)
    for u in range(nsub):
        _, l, acc = final[u]
        ot = acc / l
        for r in range(R):
            o_ref[u * qsub:(u + 1) * qsub, r * LANE:(r + 1) * LANE] = \
                ot[:, r * qsub:(r + 1) * qsub].T


def sel_attention(q_rot, mfeat, kvb, onehot, *, k_blk, v_blk, tq=1024, qsub=128, tkv=1024):
    B, S, _ = q_rot.shape
    G = NSA_KV_GROUPS
    n_sel = onehot.shape[1]
    qw = NSA_REP * LANE
    tkv = min(tkv, S)
    assert tkv % tq == 0 and S % tkv == 0
    v = kvb[:, :, v_blk * LANE:(v_blk + G) * LANE].reshape(B, S, G, LANE).transpose(0, 2, 3, 1)
    pad = jnp.zeros((B, G, 16, S), BF16).at[:, :, 0, :].set(1.0)
    vt = jnp.concatenate([v, pad], axis=2)
    return pl.pallas_call(
        functools.partial(_sel_attn_kernel, tq=tq, qsub=qsub, tkv=tkv),
        grid=(B, G, S // tq),
        in_specs=[pl.BlockSpec((None, tq, qw), lambda b, g, i: (b, i, g)),
                  pl.BlockSpec((None, None, tq, n_sel), lambda b, g, i: (b, g, i, 0)),
                  pl.BlockSpec((None, S, LANE), lambda b, g, i: (b, 0, k_blk + g)),
                  pl.BlockSpec((S, n_sel), lambda b, g, i: (0, 0)),
                  pl.BlockSpec((None, None, LANE + 16, S), lambda b, g, i: (b, g, 0, 0))],
        out_specs=pl.BlockSpec((None, tq, qw), lambda b, g, i: (b, i, g)),
        out_shape=jax.ShapeDtypeStruct((B, S, NSA_HEADS * LANE), F32),
        compiler_params=_cparams(("parallel", "parallel", "arbitrary")),
        name="nsa_sel_attn",
    )(q_rot, mfeat, kvb, onehot, vt)


def _nsa_combine_kernel(g_ref, oc_ref, os_ref, ow_ref, o_ref):
    gates = jax.nn.sigmoid(g_ref[...])
    for h in range(NSA_HEADS):
        sl = slice(h * LANE, (h + 1) * LANE)
        acc = gates[:, 3 * h:3 * h + 1] * oc_ref[:, sl]
        acc = acc + gates[:, 3 * h + 1:3 * h + 2] * os_ref[:, sl]
        acc = acc + gates[:, 3 * h + 2:3 * h + 3] * ow_ref[:, sl]
        o_ref[:, sl] = acc.astype(o_ref.dtype)


def nsa_combine(gate_logits, o_cmp, o_sel, o_win, ts=256):
    B, S, W = o_cmp.shape
    spec = pl.BlockSpec((None, ts, W), lambda b, i: (b, i, 0))
    return pl.pallas_call(
        _nsa_combine_kernel,
        grid=(B, S // ts),
        in_specs=[pl.BlockSpec((None, ts, LANE), lambda b, i: (b, i, 0)), spec, spec, spec],
        out_specs=spec,
        out_shape=jax.ShapeDtypeStruct((B, S, W), BF16),
        compiler_params=_cparams(("parallel", "parallel")),
        name="nsa_combine",
    )(gate_logits, o_cmp, o_sel, o_win)


def _dil_combine_kernel(*refs, dils):
    n = len(dils)
    o_refs, lse_refs, out_ref = refs[:n], refs[n:2 * n], refs[2 * n]
    nat_o, nat_l = refs[2 * n + 1:3 * n + 1], refs[3 * n + 1:]
    ts = out_ref.shape[0]
    for h in range(out_ref.shape[1] // LANE):
        sl = slice(h * LANE, (h + 1) * LANE)
        for g, dil in enumerate(dils):
            for r in range(dil):
                rows = pl.ds(r, ts // dil, stride=dil) if dil > 1 else slice(None)
                nat_o[g][rows, :] = o_refs[g][r, :, sl]
                nat_l[g][rows, :] = lse_refs[g][r, :, sl]
        lses = [r[...] for r in nat_l]
        mx = functools.reduce(jnp.maximum, lses)
        ws = [jnp.exp2(l - mx) for l in lses]
        den = functools.reduce(lambda a, b: a + b, ws)
        num = functools.reduce(lambda a, b: a + b, [w * r[...] for w, r in zip(ws, nat_o)])
        out_ref[:, sl] = (num / den).astype(out_ref.dtype)


def dil_combine(outs, lses, ts=128):
    dils = tuple(o.shape[1] for o in outs)
    B, W = outs[0].shape[0], outs[0].shape[-1]
    S = outs[0].shape[1] * outs[0].shape[2]
    specs = [pl.BlockSpec((None, d, ts // d, W), lambda b, i: (b, 0, i, 0)) for d in dils]
    return pl.pallas_call(
        functools.partial(_dil_combine_kernel, dils=dils),
        grid=(B, S // ts),
        in_specs=specs + specs,
        out_specs=pl.BlockSpec((None, ts, W), lambda b, i: (b, i, 0)),
        out_shape=jax.ShapeDtypeStruct((B, S, W), BF16),
        scratch_shapes=[pltpu.VMEM((ts, LANE), F32)] * (2 * len(dils)),
        compiler_params=_cparams(("parallel", "parallel")),
        name="dil_combine",
    )(*outs, *lses)


def _softplus(x):
    return jnp.maximum(x, 0.0) + jnp.log1p(jnp.exp(-jnp.abs(x)))


def _gate_band_starts():
    ntile = D_RNN // LANE
    starts = []
    for j in range(ntile):
        n_lo = (j * LANE) // RNN_BLOCK_DIM
        n_hi = (j * LANE + LANE - 1) // RNN_BLOCK_DIM
        lo = (n_lo * RNN_BLOCK_DIM) // LANE
        hi = -(-((n_hi + 1) * RNN_BLOCK_DIM) // LANE)
        assert hi - lo <= 4
        starts.append(min(lo, ntile - 4))
    return starts


def _rglru_kernel(y_ref, xr_ref, cw_ref, cb_ref, wg_ref, bg_ref, lam_ref, o_ref,
                  h_ref, tail_ref, *, ts, starts):
    i = pl.program_id(1)

    @pl.when(i == 0)
    def _():
        h_ref[...] = jnp.zeros_like(h_ref)
        tail_ref[...] = jnp.zeros_like(tail_ref)

    xr = xr_ref[...]
    ext = jnp.concatenate([tail_ref[...], xr], axis=0)
    x = cb_ref[...] + xr * cw_ref[CONV_WIDTH - 1:CONV_WIDTH, :]
    for d in range(1, CONV_WIDTH):
        shifted = pltpu.roll(ext, d, 0)[8:8 + ts]
        x = x + shifted * cw_ref[CONV_WIDTH - 1 - d:CONV_WIDTH - d, :]
    tail_ref[...] = xr[ts - 8:ts]

    xb = x.astype(BF16)
    gl = []
    for g in range(2):
        tiles = [jnp.dot(xb[:, a * LANE:(a + 4) * LANE], wg_ref[g, j],
                         preferred_element_type=F32) for j, a in enumerate(starts)]
        gl.append(jnp.concatenate(tiles, axis=1) + bg_ref[g:g + 1, :])
    r = jax.nn.sigmoid(gl[0])
    ig = jax.nn.sigmoid(gl[1])
    log_a = (-LRU_C) * r * _softplus(-lam_ref[...])
    a = jnp.exp(log_a)
    b = jnp.sqrt(-jnp.tanh(log_a) * (a * a + 1.0)) * (ig * x)

    row = lax.broadcasted_iota(jnp.int32, a.shape, 0)
    d = 1
    while d < ts:
        keep = row >= d
        b = b + a * jnp.where(keep, pltpu.roll(b, d, 0), 0.0)
        a = a * jnp.where(keep, pltpu.roll(a, d, 0), 1.0)
        d *= 2
    h = a * h_ref[0:1, :] + b
    h_ref[0:1, :] = h[ts - 1:ts, :]
    o_ref[...] = (h * jax.nn.gelu(y_ref[...])).astype(o_ref.dtype)


def rglru_scan(proj, conv_w, conv_b, wband, b_gate, lam, ts=128):
    B, S, _ = proj.shape
    C = D_RNN
    starts = _gate_band_starts()
    vec = lambda n: pl.BlockSpec((n, C), lambda b, i: (0, 0))
    return pl.pallas_call(
        functools.partial(_rglru_kernel, ts=ts, starts=starts),
        grid=(B, S // ts),
        in_specs=[pl.BlockSpec((None, ts, C), lambda b, i: (b, i, 0)),
                  pl.BlockSpec((None, ts, C), lambda b, i: (b, i, 1)),
                  vec(CONV_WIDTH), vec(1),
                  pl.BlockSpec(wband.shape, lambda b, i: (0, 0, 0, 0)),
                  vec(2), vec(1)],
        out_specs=pl.BlockSpec((None, ts, C), lambda b, i: (b, i, 0)),
        out_shape=jax.ShapeDtypeStruct((B, S, C), BF16),
        scratch_shapes=[pltpu.VMEM((8, C), F32), pltpu.VMEM((8, C), F32)],
        compiler_params=_cparams(("arbitrary", "arbitrary")),
        name="rglru_scan",
    )(proj, proj, conv_w, conv_b.reshape(1, C), wband, b_gate, lam.reshape(1, C))


def _gate_band_weights(w_gate):
    starts = _gate_band_starts()
    dense = jnp.stack([jax.scipy.linalg.block_diag(*[w_gate[g, n] for n in range(RNN_BLOCKS)])
                       for g in range(2)])
    tiles = [dense[:, a * LANE:(a + 4) * LANE, j * LANE:(j + 1) * LANE]
             for j, a in enumerate(starts)]
    return jnp.stack(tiles, axis=1).astype(BF16)


def _row_tile(T):
    return 512 if T % 512 == 0 else T


RESIDUAL = dict(epilogue="residual")


def out_proj(a, w, x, mod, gate_blk, tk, tail):
    B, S, D = x.shape
    T = B * S
    out = matmul(a.reshape(T, -1), w.astype(BF16), tm=_row_tile(T), tn=D, tk=tk,
                 res=x.reshape(T, D), mod=mod, gate_blk=gate_blk, rows_per_batch=S, **tail)
    if isinstance(out, (list, tuple)):
        return tuple(o.reshape(B, S, D) for o in out)
    return out.reshape(B, S, D)


def nsa_mixer(hn, x, mod, gate_blk, w_in, cmp_pe, cmp_w1, cmp_w2, w_out, cos2, sin2,
              tail=RESIDUAL):
    B, S, D = x.shape
    T = B * S
    G = NSA_KV_GROUPS
    hq = NSA_HEADS * HEAD_DIM
    kvw = NSA_KV_WIDTH
    main = hq + 6 * kvw
    tm = _row_tile(T)
    hn2 = hn.reshape(T, D)
    w_gate = jnp.pad(w_in[:, main:], ((0, 0), (0, LANE - 3 * NSA_HEADS))).astype(BF16)
    gate_logits = matmul(hn2, w_gate, tm=tm, tn=LANE, tk=D).reshape(B, S, LANE)

    q_cmp, q_rot = proj_heads(hn2, w_in[:, :hq].astype(BF16), cos2, sin2,
                              [(SCALE_Q,) * NSA_HEADS, (ROPE_Q,) * NSA_HEADS], tm=tm)
    g_rope, g_cast = (ROPE,) * G, (CAST,) * G
    (kvb,) = proj_heads(hn2, w_in[:, hq + 2 * kvw:main].astype(BF16), cos2, sin2,
                        [g_rope + g_cast + g_rope + g_cast], tm=tm)
    q_rot = q_rot.reshape(B, 1, S, hq)
    kvb = kvb.reshape(B, 1, S, 4 * kvw)

    npiece = S // CMP_STRIDE
    kv_cmp = matmul(hn2, w_in[:, hq:hq + 2 * kvw].astype(BF16), tm=tm, tn=2 * kvw, tk=D)
    kvc = compress(kv_cmp.reshape(B, S, 2 * kvw), cmp_pe, cmp_w1.astype(BF16),
                   cmp_w2.astype(BF16))

    n_sb = S // SEL_BLOCK
    cmp_start = np.arange(npiece) * CMP_STRIDE
    sel_start = np.arange(n_sb) * SEL_BLOCK
    overlap = ((cmp_start[:, None] < sel_start[None, :] + SEL_BLOCK)
               & (cmp_start[:, None] + CMP_BLOCK > sel_start[None, :]))
    overlap[npiece - 1] = False
    overlap_t = jnp.asarray(overlap.T, BF16)
    onehot = jnp.asarray(np.arange(S)[:, None] // SEL_BLOCK == np.arange(n_sb)[None, :], BF16)

    o_cmp, imp_t = cmp_attention(q_cmp.reshape(B, S, hq), kvc, overlap_t)
    mfeat = topk_select(imp_t)
    kb = kvw // LANE
    o_sel = sel_attention(q_rot.reshape(B, S, hq), mfeat, kvb.reshape(B, S, 4 * kvw),
                          onehot, k_blk=0, v_blk=kb)
    (o_win,) = banded_attention(q_rot, kvb, kvb, n_kv=G, R=NSA_REP, k_blk=2 * kb, v_blk=3 * kb,
                                nwin=(NSA_WINDOW - 1 + LANE - 1) // LANE,
                                window=NSA_WINDOW - 1, want_lse=False)
    o = nsa_combine(gate_logits, o_cmp, o_sel, o_win.reshape(B, S, hq))
    return out_proj(o, w_out, x, mod, gate_blk, hq, tail)


def dilated_mixer(hn, x, mod, gate_blk, w_in, w_out, cos2, sin2, tail=RESIDUAL):
    B, S, D = x.shape
    T = B * S
    H = DIL_HEADS
    hw = H * HEAD_DIM
    tm = _row_tile(T)
    outs, lses = [], []
    for g, (window, dil) in enumerate(DIL_PATTERNS):
        w = window // dil
        L = S // dil
        regroup = lambda t: t.reshape(t.shape[:-2] + (L, dil, t.shape[-1])).swapaxes(-3, -2)
        hn_g = regroup(hn).reshape(T, D)
        cos_g, sin_g = regroup(cos2).reshape(S, LANE), regroup(sin2).reshape(S, LANE)
        q, k, v = [proj_heads(hn_g, w_in, cos_g, sin_g, [(kind,) * H], tm=tm,
                              col_blk=3 * g + c)[0].reshape(B, dil, L, hw)
                   for c, kind in enumerate((ROPE_Q, ROPE, CAST))]
        o, lse = banded_attention(q, k, v, n_kv=H, R=1, nwin=(w + LANE - 1) // LANE, window=w,
                                  want_lse=True)
        outs.append(o)
        lses.append(lse)
    o = dil_combine(outs, lses)
    return out_proj(o, w_out, x, mod, gate_blk, hw, tail)


def rglru_mixer(hn, x, mod, gate_blk, w_in, conv_w, conv_b, w_gate, b_gate, lam, w_out,
                tail=RESIDUAL):
    B, S, D = x.shape
    T = B * S
    tm = _row_tile(T)
    proj = matmul(hn.reshape(T, D), w_in.astype(BF16), tm=tm, tn=D_RNN, tk=D)
    hy = rglru_scan(proj.reshape(B, S, 2 * D_RNN), conv_w, conv_b, _gate_band_weights(w_gate),
                    b_gate, lam)
    return out_proj(hy, w_out, x, mod, gate_blk, D_RNN, tail)


def mlp(hn, x, mod, gate_blk, w1, w2, tail=RESIDUAL):
    B, S, D = x.shape
    T = B * S
    h = matmul(hn.reshape(T, D), w1.astype(BF16), tm=2 * _row_tile(T) if T % 1024 == 0 else T,
               tn=2048, tk=D, out_dtype=BF16, epilogue="relu2")
    return out_proj(h, w2, x, mod, gate_blk, 2048, tail)


def kernel(x, c, l0_w_ada, l0_b_ada, l0_norm1, l0_w_in, l0_cmp_pe, l0_cmp_w1, l0_cmp_w2, l0_w_out, l0_norm2, l0_w_ff1, l0_w_ff2, l1_w_ada, l1_b_ada, l1_norm1, l1_w_in, l1_w_out, l1_norm2, l1_w_ff1, l1_w_ff2, l2_w_ada, l2_b_ada, l2_norm1, l2_w_in, l2_conv_w, l2_conv_b, l2_w_gate, l2_b_gate, l2_lambda, l2_w_out, l2_norm2, l2_w_ff1, l2_w_ff2, l3_w_ada, l3_b_ada, l3_norm1, l3_w_in, l3_cmp_pe, l3_cmp_w1, l3_cmp_w2, l3_w_out, l3_norm2, l3_w_ff1, l3_w_ff2, norm_f):
    B, S, D = x.shape
    cos2, sin2 = rope_tables(S)
    layers = (
        (l0_w_ada, l0_b_ada, l0_norm1, l0_norm2, l0_w_ff1, l0_w_ff2,
         (l0_w_in, l0_cmp_pe, l0_cmp_w1, l0_cmp_w2, l0_w_out)),
        (l1_w_ada, l1_b_ada, l1_norm1, l1_norm2, l1_w_ff1, l1_w_ff2, (l1_w_in, l1_w_out)),
        (l2_w_ada, l2_b_ada, l2_norm1, l2_norm2, l2_w_ff1, l2_w_ff2,
         (l2_w_in, l2_conv_w, l2_conv_b, l2_w_gate, l2_b_gate, l2_lambda, l2_w_out)),
        (l3_w_ada, l3_b_ada, l3_norm1, l3_norm2, l3_w_ff1, l3_w_ff2,
         (l3_w_in, l3_cmp_pe, l3_cmp_w1, l3_cmp_w2, l3_w_out)),
    )
    mods = [adaln(c, l[0], l[1]).reshape(B, 1, 6 * D) for l in layers]
    hn = modulate(x, layers[0][2], mods[0], 0, 1)
    for li in range(DEPTH):
        _, _, _, n2, ff1, ff2, mix = layers[li]
        mod = mods[li]
        tail = dict(epilogue="residual_norm", norm_gain=n2, norm_mod=mod, shift_blk=3, scale_blk=4)
        kind = li % N_MIXERS
        if kind == 0:
            x, hn = nsa_mixer(hn, x, mod, 2, *mix, cos2, sin2, tail=tail)
        elif kind == 1:
            x, hn = dilated_mixer(hn, x, mod, 2, *mix, cos2, sin2, tail=tail)
        else:
            x, hn = rglru_mixer(hn, x, mod, 2, *mix, tail=tail)
        if li + 1 < DEPTH:
            tail = dict(epilogue="residual_norm", norm_gain=layers[li + 1][2],
                        norm_mod=mods[li + 1], shift_blk=0, scale_blk=1)
            x, hn = mlp(hn, x, mod, 5, ff1, ff2, tail=tail)
        else:
            return mlp(hn, x, mod, 5, ff1, ff2,
                       tail=dict(epilogue="residual_final", norm_gain=norm_f))
```

```python
import functools
import math

import jax
import jax.numpy as jnp
import numpy as np
from jax import lax
from jax.experimental import pallas as pl
from jax.experimental.pallas import tpu as pltpu

F32 = jnp.float32
BF16 = jnp.bfloat16

D_MODEL = 2048
DEPTH = 4
N_MIXERS = 3
HEAD_DIM = 128
ROPE_THETA = 10000.0
NORM_EPS = 1e-6

NSA_HEADS = D_MODEL // HEAD_DIM
NSA_KV_GROUPS = 4
NSA_REP = NSA_HEADS // NSA_KV_GROUPS
NSA_KV_WIDTH = NSA_KV_GROUPS * HEAD_DIM
CMP_BLOCK = 32
CMP_STRIDE = 16
CMP_HIDDEN = 4 * HEAD_DIM
SEL_BLOCK = 64
SEL_TOPK = 16
NSA_WINDOW = 512
FORCED_BONUS = 1e9

DIL_HEADS = D_MODEL // HEAD_DIM
DIL_PATTERNS = ((128, 1), (512, 4), (2048, 16))

D_RNN = 2688
RNN_BLOCKS = 16
RNN_BLOCK_DIM = D_RNN // RNN_BLOCKS
CONV_WIDTH = 4
LRU_C = 8.0

LANE = 128
LOG2E = math.log2(math.e)
QK_SCALE = LOG2E / math.sqrt(HEAD_DIM)
NEG_BIG = -1e30
SEL_OFF = -float(2 ** 30)
VMEM_LIMIT = 56 * 1024 * 1024

NT_DIMS = (((1,), (1,)), ((), ()))


def _cparams(sem):
    return pltpu.CompilerParams(dimension_semantics=sem, vmem_limit_bytes=VMEM_LIMIT)


def _adaln_kernel(c_ref, w_ref, b_ref, o_ref):
    w = w_ref[...]
    for b in range(c_ref.shape[0]):
        c = c_ref[b]
        cond = c * jax.nn.sigmoid(c)
        o_ref[b:b + 1, :] = jnp.sum(w * cond, axis=0, keepdims=True) + b_ref[...]


def adaln(c, w_ada, b_ada, tn=1024):
    B, D = c.shape
    N = w_ada.shape[1]
    return pl.pallas_call(
        _adaln_kernel,
        grid=(N // tn,),
        in_specs=[pl.BlockSpec((B, D, 1), lambda j: (0, 0, 0)),
                  pl.BlockSpec((D, tn), lambda j: (0, j)),
                  pl.BlockSpec((1, tn), lambda j: (0, j))],
        out_specs=pl.BlockSpec((B, tn), lambda j: (0, j)),
        out_shape=jax.ShapeDtypeStruct((B, N), F32),
        compiler_params=_cparams(("parallel",)),
        name="adaln",
    )(c.reshape(B, D, 1), w_ada, b_ada.reshape(1, N))


def _modulate_kernel(x_ref, gain_ref, sh_ref, sc_ref, o_ref):
    x = x_ref[...]
    ms = jnp.mean(x * x, axis=-1, keepdims=True)
    y = x * lax.rsqrt(ms + NORM_EPS) * gain_ref[...]
    o_ref[...] = (y * (1.0 + sc_ref[...]) + sh_ref[...]).astype(o_ref.dtype)


def modulate(x, gain, mod, shift_blk, scale_blk, ts=512):
    B, S, D = x.shape
    return pl.pallas_call(
        _modulate_kernel,
        grid=(B, S // ts),
        in_specs=[pl.BlockSpec((None, ts, D), lambda b, i: (b, i, 0)),
                  pl.BlockSpec((1, D), lambda b, i: (0, 0)),
                  pl.BlockSpec((None, 1, D), lambda b, i: (b, 0, shift_blk)),
                  pl.BlockSpec((None, 1, D), lambda b, i: (b, 0, scale_blk))],
        out_specs=pl.BlockSpec((None, ts, D), lambda b, i: (b, i, 0)),
        out_shape=jax.ShapeDtypeStruct((B, S, D), BF16),
        compiler_params=_cparams(("parallel", "parallel")),
        name="modulate",
    )(x, gain.reshape(1, D), mod, mod)


def _mm_kernel(*refs, nk, epilogue):
    refs = list(refs)
    a_ref, w_ref = refs[:2]
    del refs[:2]
    if epilogue.startswith("residual"):
        res_ref, gate_ref = refs[:2]
        del refs[:2]
    if epilogue == "residual_norm":
        gain_ref, sh_ref, sc_ref = refs[:3]
        del refs[:3]
    elif epilogue == "residual_final":
        gain_ref = refs.pop(0)
    o_ref = refs.pop(0)
    hn_ref = refs.pop(0) if epilogue == "residual_norm" else None
    rest = refs

    def finish(acc):
        if epilogue == "relu2":
            r = jnp.maximum(acc, 0.0)
            acc = r * r
        elif epilogue.startswith("residual"):
            acc = res_ref[...] + gate_ref[...] * acc
        if epilogue in ("residual_norm", "residual_final"):
            ms = jnp.mean(acc * acc, axis=-1, keepdims=True)
            y = acc * lax.rsqrt(ms + NORM_EPS) * gain_ref[...]
            if epilogue == "residual_final":
                o_ref[...] = y
                return
            hn_ref[...] = (y * (1.0 + sc_ref[...]) + sh_ref[...]).astype(hn_ref.dtype)
        o_ref[...] = acc.astype(o_ref.dtype)

    def part():
        return jnp.dot(a_ref[...], w_ref[...], preferred_element_type=F32)

    if nk == 1:
        finish(part())
        return
    acc_ref = rest[0] if rest else o_ref
    k = pl.program_id(2)

    @pl.when(k == 0)
    def _():
        acc_ref[...] = part()

    if nk > 2:
        @pl.when(jnp.logical_and(k > 0, k < nk - 1))
        def _():
            acc_ref[...] += part()

    @pl.when(k == nk - 1)
    def _():
        finish(acc_ref[...] + part())


def matmul(a, w, *, tm, tn, tk, out_dtype=F32, epilogue="none", res=None, mod=None,
           gate_blk=0, rows_per_batch=None, norm_gain=None, norm_mod=None, shift_blk=0,
           scale_blk=0):
    M, K = a.shape
    N = w.shape[1]
    nk = K // tk
    assert M % tm == 0 and N % tn == 0 and K % tk == 0
    in_specs = [pl.BlockSpec((tm, tk), lambda i, j, k: (i, k)),
                pl.BlockSpec((tk, tn), lambda i, j, k: (k, j))]
    args = [a, w]
    out_spec = pl.BlockSpec((tm, tn), lambda i, j, k: (i, j))
    out_specs, out_shape = out_spec, jax.ShapeDtypeStruct((M, N), out_dtype)
    if epilogue.startswith("residual"):
        assert rows_per_batch % tm == 0 and tn == D_MODEL == N
        batch = lambda i: i * tm // rows_per_batch
        chunk = lambda blk: pl.BlockSpec((None, 1, tn), lambda i, j, k: (batch(i), 0, blk))
        in_specs += [out_spec, chunk(gate_blk)]
        args += [res, mod]
        if epilogue != "residual":
            in_specs.append(pl.BlockSpec((1, tn), lambda i, j, k: (0, 0)))
            args.append(norm_gain.reshape(1, tn))
        if epilogue == "residual_norm":
            in_specs += [chunk(shift_blk), chunk(scale_blk)]
            args += [norm_mod, norm_mod]
            out_specs = [out_spec, out_spec]
            out_shape = [out_shape, jax.ShapeDtypeStruct((M, N), BF16)]
    scratch = [pltpu.VMEM((tm, tn), F32)] if nk > 1 and out_dtype != F32 else []
    return pl.pallas_call(
        functools.partial(_mm_kernel, nk=nk, epilogue=epilogue),
        grid=(M // tm, N // tn, nk),
        in_specs=in_specs,
        out_specs=out_specs,
        out_shape=out_shape,
        scratch_shapes=scratch,
        compiler_params=_cparams(("parallel", "parallel", "arbitrary")),
        name="mm_" + epilogue,
    )(*args)


def rope_tables(S):
    inv_freq = ROPE_THETA ** (-jnp.arange(0, HEAD_DIM, 2, dtype=F32) / HEAD_DIM)
    ang = jnp.arange(S, dtype=F32)[:, None] * inv_freq[None, :]
    cos, sin = jnp.cos(ang), jnp.sin(ang)
    return jnp.concatenate([cos, cos], axis=-1), jnp.concatenate([-sin, sin], axis=-1)


CAST, ROPE, ROPE_Q, SCALE_Q = 0, 1, 2, 3


def _proj_heads_kernel(a_ref, w_ref, c_ref, s_ref, *o_refs, kinds):
    if len(o_refs) > len(kinds):
        o_refs, wb_ref = o_refs[:-1], o_refs[-1]

        @pl.when(pl.program_id(0) == 0)
        def _():
            wb_ref[...] = w_ref[...].astype(BF16)
        w_ref = wb_ref
    acc = jnp.dot(a_ref[...], w_ref[...], preferred_element_type=F32)
    c = c_ref[...]
    s = s_ref[...]
    for o_ref, head_kinds in zip(o_refs, kinds):
        for h, kind in enumerate(head_kinds):
            sl = slice(h * LANE, (h + 1) * LANE)
            t = acc[:, sl]
            if kind in (ROPE, ROPE_Q):
                t = t * c + pltpu.roll(t, HEAD_DIM // 2, 1) * s
            if kind in (ROPE_Q, SCALE_Q):
                t = t * QK_SCALE
            o_ref[:, sl] = t.astype(o_ref.dtype)


def proj_heads(a, w, cos2, sin2, kinds, *, tm, col_blk=0):
    M, K = a.shape
    N = len(kinds[0]) * LANE
    nt = cos2.shape[0] // tm
    tab = pl.BlockSpec((tm, LANE), lambda i: (i % nt, 0))
    out_spec = pl.BlockSpec((tm, N), lambda i: (i, 0))
    cast = w.dtype != BF16
    return pl.pallas_call(
        functools.partial(_proj_heads_kernel, kinds=tuple(kinds)),
        grid=(M // tm,),
        in_specs=[pl.BlockSpec((tm, K), lambda i: (i, 0)),
                  pl.BlockSpec((K, N), lambda i: (0, col_blk), pipeline_mode=pl.Buffered(1)),
                  tab, tab],
        out_specs=[out_spec] * len(kinds),
        out_shape=[jax.ShapeDtypeStruct((M, N), BF16)] * len(kinds),
        scratch_shapes=[pltpu.VMEM((K, N), BF16)] if cast else [],
        compiler_params=_cparams(("arbitrary",)),
        name="proj_heads",
    )(a, w, cos2, sin2)


def _ones_column(n):
    return jnp.where(lax.broadcasted_iota(jnp.int32, (n, LANE), 1) == 0, 1.0, 0.0).astype(BF16)


def _softmax_pv(s, m, v, ones):
    p = jnp.exp2(s - m).astype(BF16)
    res = jnp.dot(p, jnp.concatenate([v, ones], axis=1), preferred_element_type=F32)
    return res[:, LANE:LANE + 1], res[:, :LANE]


def _banded_kernel(q_ref, k_ref, v_ref, o_ref, *lse_refs, R, hb, nwin, window, tq, qsub, nkeys):
    i = pl.program_id(3)
    L = k_ref.shape[0]
    rows = R * qsub
    rel = (lax.broadcasted_iota(jnp.int32, (rows, nkeys), 0) & (qsub - 1)) \
        - lax.broadcasted_iota(jnp.int32, (rows, nkeys), 1)
    ones = _ones_column(nkeys)
    units = [(h, j) for j in range(tq // qsub) for h in range(hb)]
    lane = lax.broadcasted_iota(jnp.int32, (rows, LANE), 1)
    lse_tile = jnp.zeros((rows, LANE), F32)

    def scores(h, j):
        q0 = i * tq + j * qsub
        kstart = pl.multiple_of(jnp.clip(q0 - nwin * LANE, 0, L - nkeys), LANE)
        qj = q_ref[j * qsub:(j + 1) * qsub, h * R * LANE:(h + 1) * R * LANE]
        if R > 1:
            q = jnp.concatenate([qj[:, r * LANE:(r + 1) * LANE] for r in range(R)], axis=0)
        else:
            q = qj
        s = lax.dot_general(q, k_ref[pl.ds(kstart, nkeys), h * LANE:(h + 1) * LANE], NT_DIMS,
                            preferred_element_type=F32)
        return q0, kstart, s

    ahead = 2
    pending = [scores(*u) for u in units[:ahead]]
    for n, (h, j) in enumerate(units):
        if n + ahead < len(units):
            pending.append(scores(*units[n + ahead]))
        q0, kstart, s = pending[n]
        v = v_ref[pl.ds(kstart, nkeys), h * LANE:(h + 1) * LANE]
        diff = rel + (q0 - kstart)
        valid = lax.bitcast_convert_type(diff, jnp.uint32) <= jnp.uint32(window)
        s = jnp.where(valid, s, NEG_BIG)
        m = jnp.max(s, axis=-1, keepdims=True)
        l, o = _softmax_pv(s, m, v, ones)
        o = o / l
        for r in range(R):
            col = (h * R + r) * LANE
            o_ref[j * qsub:(j + 1) * qsub, col:col + LANE] = o[r * qsub:(r + 1) * qsub]
        if lse_refs:
            lse_tile = jnp.where(lane == h, m + jnp.log(l) * LOG2E, lse_tile)
            if h == hb - 1:
                lse_refs[0][j * qsub:(j + 1) * qsub, :] = lse_tile


def banded_attention(q_arr, k_arr, v_arr, *, n_kv, R, nwin, window, want_lse, q_blk=0, k_blk=0,
                     v_blk=0):
    B, dil, L, _ = q_arr.shape
    tq = min(512, L)
    qsub = LANE
    nkeys = min(qsub + nwin * LANE, L)
    hb = 4 if R == 1 else 1
    assert L % tq == 0 and n_kv % hb == 0
    assert q_blk % (hb * R) == 0 and k_blk % hb == 0 and v_blk % hb == 0
    width = n_kv * R * LANE
    out_shape = [jax.ShapeDtypeStruct((B, dil, L, width), F32)]
    out_specs = [pl.BlockSpec((None, None, tq, hb * R * LANE), lambda b, r, h, i: (b, r, i, h))]
    if want_lse:
        out_shape.append(jax.ShapeDtypeStruct((B, dil, L, n_kv // hb * LANE), F32))
        out_specs.append(pl.BlockSpec((None, None, tq, LANE), lambda b, r, h, i: (b, r, i, h)))
    kv_spec = lambda blk: pl.BlockSpec((None, None, L, hb * LANE),
                                       lambda b, r, h, i: (b, r, 0, blk // hb + h))
    return pl.pallas_call(
        functools.partial(_banded_kernel, R=R, hb=hb, nwin=nwin, window=window, tq=tq, qsub=qsub,
                          nkeys=nkeys),
        grid=(B, dil, n_kv // hb, L // tq),
        in_specs=[pl.BlockSpec((None, None, tq, hb * R * LANE),
                               lambda b, r, h, i: (b, r, i, q_blk // (hb * R) + h)),
                  kv_spec(k_blk), kv_spec(v_blk)],
        out_specs=out_specs,
        out_shape=out_shape,
        compiler_params=_cparams(("parallel", "parallel", "parallel", "arbitrary")),
        name="banded_attn",
    )(q_arr, k_arr, v_arr)


def _compress_kernel(x_ref, pe_ref, w1_ref, w2_ref, o_ref):
    n = x_ref.shape[0] // CMP_STRIDE
    a = jnp.zeros((n, CMP_HIDDEN), F32)
    b = jnp.zeros((n, CMP_HIDDEN), F32)
    for j in range(CMP_STRIDE):
        xj = x_ref[pl.ds(j, n, stride=CMP_STRIDE), :]
        top = (xj + pe_ref[j:j + 1, :]).astype(BF16)
        bot = (xj + pe_ref[CMP_STRIDE + j:CMP_STRIDE + j + 1, :]).astype(BF16)
        a = a + jnp.dot(top, w1_ref[j * HEAD_DIM:(j + 1) * HEAD_DIM, :],
                        preferred_element_type=F32)
        b = b + jnp.dot(bot, w1_ref[(CMP_STRIDE + j) * HEAD_DIM:(CMP_STRIDE + j + 1) * HEAD_DIM, :],
                        preferred_element_type=F32)
    hid = a + pltpu.roll(b, n - 1, 0)
    o_ref[...] = jnp.dot(jax.nn.gelu(hid).astype(BF16), w2_ref[...], preferred_element_type=F32)


def compress(kv, pe, w1, w2):
    B, S, _ = kv.shape
    G = NSA_KV_GROUPS
    NP = S // CMP_STRIDE
    return pl.pallas_call(
        _compress_kernel,
        grid=(2, B, G),
        in_specs=[pl.BlockSpec((None, S, HEAD_DIM), lambda t, b, g: (b, 0, t * G + g)),
                  pl.BlockSpec((None, CMP_BLOCK, HEAD_DIM), lambda t, b, g: (t, 0, 0)),
                  pl.BlockSpec((None, CMP_BLOCK * HEAD_DIM, CMP_HIDDEN), lambda t, b, g: (t, 0, 0)),
                  pl.BlockSpec((None, CMP_HIDDEN, HEAD_DIM), lambda t, b, g: (t, 0, 0))],
        out_specs=pl.BlockSpec((None, None, None, NP, HEAD_DIM), lambda t, b, g: (t, b, g, 0, 0)),
        out_shape=jax.ShapeDtypeStruct((2, B, G, NP, HEAD_DIM), F32),
        compiler_params=_cparams(("parallel", "parallel", "parallel")),
        name="nsa_compress",
    )(kv, pe, w1, w2)


def _cmp_attn_kernel(q_ref, kc_ref, vc_ref, ov_ref, o_ref, imp_ref, *, tq, qsub):
    i = pl.program_id(2)
    R = NSA_REP
    nsub = tq // qsub
    rows = R * qsub
    n_var = kc_ref.shape[0] // LANE

    def run(nk):
        kc = kc_ref[0:nk, :].astype(BF16)
        vc = vc_ref[0:nk, :].astype(BF16)
        ov = ov_ref[:, 0:nk]
        tloc = lax.broadcasted_iota(jnp.int32, (rows, nk), 0) & (qsub - 1)
        cmp_end = lax.broadcasted_iota(jnp.int32, (rows, nk), 1) * CMP_STRIDE + (CMP_BLOCK - 1)

        def scores(u):
            qu = q_ref[u * qsub:(u + 1) * qsub, :]
            q = jnp.concatenate([qu[:, r * LANE:(r + 1) * LANE] for r in range(R)], axis=0)
            return lax.dot_general(q, kc, NT_DIMS, preferred_element_type=F32)

        ahead = 2
        pending = [scores(u) for u in range(min(ahead, nsub))]
        for u in range(nsub):
            if u + ahead < nsub:
                pending.append(scores(u + ahead))
            valid = cmp_end <= tloc + (i * tq + u * qsub)
            s = jnp.where(valid, pending[u], NEG_BIG)
            m = jnp.max(s, axis=-1, keepdims=True)
            p = jnp.where(valid, jnp.exp2(s - m), 0.0)
            l = jnp.sum(p, axis=-1, keepdims=True)
            p = p / jnp.where(l > 0, l, 1.0)
            o = jnp.dot(p.astype(BF16), vc, preferred_element_type=F32)
            for r in range(R):
                o_ref[u * qsub:(u + 1) * qsub, r * LANE:(r + 1) * LANE] = \
                    o[r * qsub:(r + 1) * qsub]
            psum = p[0:qsub]
            for r in range(1, R):
                psum = psum + p[r * qsub:(r + 1) * qsub]
            p_hi = psum.astype(BF16)
            p_lo = (psum - p_hi.astype(F32)).astype(BF16)
            imp_ref[:, u * qsub:(u + 1) * qsub] = (
                lax.dot_general(ov, p_hi, NT_DIMS, preferred_element_type=F32)
                + lax.dot_general(ov, p_lo, NT_DIMS, preferred_element_type=F32))

    need = ((i + 1) * tq - CMP_BLOCK) // CMP_STRIDE + 1
    var = jnp.clip((need - 1) // LANE, 0, n_var - 1)
    for v in range(n_var):
        pl.when(var == v)(functools.partial(run, (v + 1) * LANE))


def cmp_attention(q, kvc, overlap_t, tq=512, qsub=128):
    B, S, _ = q.shape
    G = NSA_KV_GROUPS
    NP = kvc.shape[3]
    n_sel = overlap_t.shape[0]
    qw = NSA_REP * LANE
    return pl.pallas_call(
        functools.partial(_cmp_attn_kernel, tq=tq, qsub=qsub),
        grid=(B, G, S // tq),
        in_specs=[pl.BlockSpec((None, tq, qw), lambda b, g, i: (b, i, g)),
                  pl.BlockSpec((None, None, None, NP, LANE), lambda b, g, i: (0, b, g, 0, 0)),
                  pl.BlockSpec((None, None, None, NP, LANE), lambda b, g, i: (1, b, g, 0, 0)),
                  pl.BlockSpec((n_sel, NP), lambda b, g, i: (0, 0))],
        out_specs=[pl.BlockSpec((None, tq, qw), lambda b, g, i: (b, i, g)),
                   pl.BlockSpec((None, None, n_sel, tq), lambda b, g, i: (b, g, 0, i))],
        out_shape=[jax.ShapeDtypeStruct((B, S, NSA_HEADS * LANE), F32),
                   jax.ShapeDtypeStruct((B, G, n_sel, S), F32)],
        compiler_params=_cparams(("parallel", "parallel", "parallel")),
        name="nsa_cmp_attn",
    )(q, kvc, kvc, overlap_t)


def _topk_kernel(imp_ref, o_ref, *, tq):
    i = pl.program_id(2)
    imp = imp_ref[...]
    n_sel = imp.shape[0]
    blk = lax.broadcasted_iota(jnp.int32, imp.shape, 0)
    t = i * tq + lax.broadcasted_iota(jnp.int32, imp.shape, 1)
    cur = t // SEL_BLOCK
    avail = blk <= cur
    forced = jnp.where(blk == 0, 1.0, jnp.where(blk == cur, 1.0, jnp.where(blk == cur - 1, 1.0, 0.0)))
    score = jnp.where(avail, imp + FORCED_BONUS * forced, -jnp.inf)
    picked = jnp.zeros(imp.shape, F32)
    for _ in range(min(SEL_TOPK, n_sel)):
        mx = jnp.max(score, axis=0, keepdims=True)
        first = jnp.min(jnp.where(score == mx, blk, n_sel), axis=0, keepdims=True)
        hit = blk == first
        picked = jnp.where(hit, 1.0, picked)
        score = jnp.where(hit, -jnp.inf, score)
    feat = jnp.where(avail, jnp.where(picked > 0.0, 0.0, SEL_OFF), SEL_OFF)
    o_ref[...] = feat.T.astype(o_ref.dtype)


def topk_select(imp_t, tq=1024):
    B, G, n_sel, S = imp_t.shape
    tq = min(tq, S)
    return pl.pallas_call(
        functools.partial(_topk_kernel, tq=tq),
        grid=(B, G, S // tq),
        in_specs=[pl.BlockSpec((None, None, n_sel, tq), lambda b, g, i: (b, g, 0, i))],
        out_specs=pl.BlockSpec((None, None, tq, n_sel), lambda b, g, i: (b, g, i, 0)),
        out_shape=jax.ShapeDtypeStruct((B, G, S, n_sel), BF16),
        compiler_params=_cparams(("parallel", "parallel", "parallel")),
        name="nsa_topk",
    )(imp_t)


def _sel_attn_kernel(q_ref, mf_ref, k_ref, e_ref, vt_ref, o_ref, *, tq, qsub, tkv):
    i = pl.program_id(2)
    R = NSA_REP
    nsub = tq // qsub
    rows = R * qsub
    q0 = i * tq
    ntile = (q0 + tq - 1) // tkv + 1
    rel = (lax.broadcasted_iota(jnp.int32, (tkv, rows), 1) & (qsub - 1)) \
        - lax.broadcasted_iota(jnp.int32, (tkv, rows), 0)

    qa = []
    for u in range(nsub):
        qu = q_ref[u * qsub:(u + 1) * qsub, :]
        mf = mf_ref[u * qsub:(u + 1) * qsub, :]
        qa.append(jnp.concatenate(
            [jnp.concatenate([qu[:, r * LANE:(r + 1) * LANE], mf], axis=1) for r in range(R)],
            axis=0))

    def body(j, carry, masked):
        ks = pl.multiple_of(j * tkv, tkv)
        ka = jnp.concatenate([k_ref[pl.ds(ks, tkv), :], e_ref[pl.ds(ks, tkv), :]], axis=1)
        vt = vt_ref[:, pl.ds(ks, tkv)]
        out = []
        nkeys = lambda u: (u + 1) * qsub if masked else tkv
        scores = lambda u: lax.dot_general(ka[:nkeys(u)], qa[u], NT_DIMS,
                                           preferred_element_type=F32)
        ahead = 2
        sts = [scores(u) for u in range(min(ahead, nsub))]
        for u in range(nsub):
            m, l, acc = carry[u]
            if u + ahead < nsub:
                sts.append(scores(u + ahead))
            st = sts[u]
            if masked:
                st = jnp.where(rel[:nkeys(u)] >= -u * qsub, st, NEG_BIG)
            m_new = jnp.maximum(m, jnp.max(st, axis=0, keepdims=True))
            a = jnp.exp2(m - m_new)
            res = jnp.dot(vt[:, :nkeys(u)], jnp.exp2(st - m_new).astype(BF16),
                          preferred_element_type=F32)
            out.append((m_new, a * l + res[LANE:LANE + 1, :], a * acc + res[:LANE, :]))
        return tuple(out)

    init = tuple((jnp.full((1, rows), NEG_BIG, F32), jnp.zeros((1, rows), F32),
                  jnp.zeros((LANE, rows), F32)) for _ in range(nsub))
    carry = lax.fori_loop(0, ntile - 1, lambda j, c: body(j, c, False), init)
    final = body(ntile - 1, carry, True)
    for u in range(nsub):
        _, l, acc = final[u]
        ot = acc / l
        for r in range(R):
            o_ref[u * qsub:(u + 1) * qsub, r * LANE:(r + 1) * LANE] = \
                ot[:, r * qsub:(r + 1) * qsub].T


def sel_attention(q_rot, mfeat, kvb, onehot, *, k_blk, v_blk, tq=1024, qsub=128, tkv=1024):
    B, S, _ = q_rot.shape
    G = NSA_KV_GROUPS
    n_sel = onehot.shape[1]
    qw = NSA_REP * LANE
    tkv = min(tkv, S)
    tq = min(tq, S)
    assert tkv == tq and S % tkv == 0
    v = kvb[:, :, v_blk * LANE:(v_blk + G) * LANE].reshape(B, S, G, LANE).transpose(0, 2, 3, 1)
    pad = jnp.zeros((B, G, 16, S), BF16).at[:, :, 0, :].set(1.0)
    vt = jnp.concatenate([v, pad], axis=2)
    return pl.pallas_call(
        functools.partial(_sel_attn_kernel, tq=tq, qsub=qsub, tkv=tkv),
        grid=(B, G, S // tq),
        in_specs=[pl.BlockSpec((None, tq, qw), lambda b, g, i: (b, i, g)),
                  pl.BlockSpec((None, None, tq, n_sel), lambda b, g, i: (b, g, i, 0)),
                  pl.BlockSpec((None, S, LANE), lambda b, g, i: (b, 0, k_blk + g)),
                  pl.BlockSpec((S, n_sel), lambda b, g, i: (0, 0)),
                  pl.BlockSpec((None, None, LANE + 16, S), lambda b, g, i: (b, g, 0, 0))],
        out_specs=pl.BlockSpec((None, tq, qw), lambda b, g, i: (b, i, g)),
        out_shape=jax.ShapeDtypeStruct((B, S, NSA_HEADS * LANE), F32),
        compiler_params=_cparams(("parallel", "parallel", "arbitrary")),
        name="nsa_sel_attn",
    )(q_rot, mfeat, kvb, onehot, vt)


def _nsa_combine_kernel(g_ref, oc_ref, os_ref, ow_ref, o_ref):
    gates = jax.nn.sigmoid(g_ref[...])
    for h in range(NSA_HEADS):
        sl = slice(h * LANE, (h + 1) * LANE)
        acc = gates[:, 3 * h:3 * h + 1] * oc_ref[:, sl]
        acc = acc + gates[:, 3 * h + 1:3 * h + 2] * os_ref[:, sl]
        acc = acc + gates[:, 3 * h + 2:3 * h + 3] * ow_ref[:, sl]
        o_ref[:, sl] = acc.astype(o_ref.dtype)


def nsa_combine(gate_logits, o_cmp, o_sel, o_win, ts=256):
    B, S, W = o_cmp.shape
    spec = pl.BlockSpec((None, ts, W), lambda b, i: (b, i, 0))
    return pl.pallas_call(
        _nsa_combine_kernel,
        grid=(B, S // ts),
        in_specs=[pl.BlockSpec((None, ts, LANE), lambda b, i: (b, i, 0)), spec, spec, spec],
        out_specs=spec,
        out_shape=jax.ShapeDtypeStruct((B, S, W), BF16),
        compiler_params=_cparams(("parallel", "parallel")),
        name="nsa_combine",
    )(gate_logits, o_cmp, o_sel, o_win)


def _dil_combine_kernel(*refs, dils, hb):
    n = len(dils)
    o_refs, lse_refs, out_ref = refs[:n], refs[n:2 * n], refs[2 * n]
    nat_o, nat_l = refs[2 * n + 1:3 * n + 1], refs[3 * n + 1:]
    ts = out_ref.shape[0]
    for h in range(out_ref.shape[1] // LANE):
        sl = slice(h * LANE, (h + 1) * LANE)
        lane = (h // hb) * LANE + h % hb
        for g, dil in enumerate(dils):
            for r in range(dil):
                rows = pl.ds(r, ts // dil, stride=dil) if dil > 1 else slice(None)
                nat_o[g][rows, :] = o_refs[g][r, :, sl]
                nat_l[g][rows, :] = jnp.broadcast_to(lse_refs[g][r, :, lane:lane + 1],
                                                     (ts // dil, LANE))
        lses = [r[...] for r in nat_l]
        mx = functools.reduce(jnp.maximum, lses)
        ws = [jnp.exp2(l - mx) for l in lses]
        den = functools.reduce(lambda a, b: a + b, ws)
        num = functools.reduce(lambda a, b: a + b, [w * r[...] for w, r in zip(ws, nat_o)])
        out_ref[:, sl] = (num / den).astype(out_ref.dtype)


def dil_combine(outs, lses, ts=128):
    dils = tuple(o.shape[1] for o in outs)
    B, W = outs[0].shape[0], outs[0].shape[-1]
    S = outs[0].shape[1] * outs[0].shape[2]
    WL = lses[0].shape[-1]
    hb = (W // LANE) // (WL // LANE)
    spec = lambda w: [pl.BlockSpec((None, d, ts // d, w), lambda b, i: (b, 0, i, 0)) for d in dils]
    return pl.pallas_call(
        functools.partial(_dil_combine_kernel, dils=dils, hb=hb),
        grid=(B, S // ts),
        in_specs=spec(W) + spec(WL),
        out_specs=pl.BlockSpec((None, ts, W), lambda b, i: (b, i, 0)),
        out_shape=jax.ShapeDtypeStruct((B, S, W), BF16),
        scratch_shapes=[pltpu.VMEM((ts, LANE), F32)] * (2 * len(dils)),
        compiler_params=_cparams(("parallel", "parallel")),
        name="dil_combine",
    )(*outs, *lses)


def _softplus(x):
    return jnp.maximum(x, 0.0) + jnp.log1p(jnp.exp(-jnp.abs(x)))


def _gate_band_starts():
    ntile = D_RNN // LANE
    starts = []
    for j in range(ntile):
        n_lo = (j * LANE) // RNN_BLOCK_DIM
        n_hi = (j * LANE + LANE - 1) // RNN_BLOCK_DIM
        lo = (n_lo * RNN_BLOCK_DIM) // LANE
        hi = -(-((n_hi + 1) * RNN_BLOCK_DIM) // LANE)
        assert hi - lo <= 4
        starts.append(min(lo, ntile - 4))
    return starts


def _rglru_kernel(y_ref, xr_ref, cw_ref, cb_ref, wg_ref, bg_ref, lam_ref, o_ref,
                  h_ref, tail_ref, *, ts, starts):
    i = pl.program_id(1)

    @pl.when(i == 0)
    def _():
        h_ref[...] = jnp.zeros_like(h_ref)
        tail_ref[...] = jnp.zeros_like(tail_ref)

    xr = xr_ref[...]
    ext = jnp.concatenate([tail_ref[...], xr], axis=0)
    x = cb_ref[...] + xr * cw_ref[CONV_WIDTH - 1:CONV_WIDTH, :]
    for d in range(1, CONV_WIDTH):
        shifted = pltpu.roll(ext, d, 0)[8:8 + ts]
        x = x + shifted * cw_ref[CONV_WIDTH - 1 - d:CONV_WIDTH - d, :]
    tail_ref[...] = xr[ts - 8:ts]

    xb = x.astype(BF16)
    gl = []
    for g in range(2):
        tiles = [jnp.dot(xb[:, a * LANE:(a + 4) * LANE], wg_ref[g, j],
                         preferred_element_type=F32) for j, a in enumerate(starts)]
        gl.append(jnp.concatenate(tiles, axis=1) + bg_ref[g:g + 1, :])
    r = jax.nn.sigmoid(gl[0])
    ig = jax.nn.sigmoid(gl[1])
    log_a = (-LRU_C) * r * _softplus(-lam_ref[...])
    a = jnp.exp(log_a)
    b = jnp.sqrt(-jnp.tanh(log_a) * (a * a + 1.0)) * (ig * x)

    row = lax.broadcasted_iota(jnp.int32, a.shape, 0)
    d = 1
    while d < ts:
        keep = row >= d
        b = b + a * jnp.where(keep, pltpu.roll(b, d, 0), 0.0)
        a = a * jnp.where(keep, pltpu.roll(a, d, 0), 1.0)
        d *= 2
    h = a * h_ref[0:1, :] + b
    h_ref[0:1, :] = h[ts - 1:ts, :]
    o_ref[...] = (h * jax.nn.gelu(y_ref[...])).astype(o_ref.dtype)


def rglru_scan(proj, conv_w, conv_b, wband, b_gate, lam, ts=128):
    B, S, _ = proj.shape
    C = D_RNN
    starts = _gate_band_starts()
    vec = lambda n: pl.BlockSpec((n, C), lambda b, i: (0, 0))
    return pl.pallas_call(
        functools.partial(_rglru_kernel, ts=ts, starts=starts),
        grid=(B, S // ts),
        in_specs=[pl.BlockSpec((None, ts, C), lambda b, i: (b, i, 0)),
                  pl.BlockSpec((None, ts, C), lambda b, i: (b, i, 1)),
                  vec(CONV_WIDTH), vec(1),
                  pl.BlockSpec(wband.shape, lambda b, i: (0, 0, 0, 0)),
                  vec(2), vec(1)],
        out_specs=pl.BlockSpec((None, ts, C), lambda b, i: (b, i, 0)),
        out_shape=jax.ShapeDtypeStruct((B, S, C), BF16),
        scratch_shapes=[pltpu.VMEM((8, C), F32), pltpu.VMEM((8, C), F32)],
        compiler_params=_cparams(("arbitrary", "arbitrary")),
        name="rglru_scan",
    )(proj, proj, conv_w, conv_b.reshape(1, C), wband, b_gate, lam.reshape(1, C))


def _gate_band_weights(w_gate):
    starts = _gate_band_starts()
    dense = jnp.stack([jax.scipy.linalg.block_diag(*[w_gate[g, n] for n in range(RNN_BLOCKS)])
                       for g in range(2)])
    tiles = [dense[:, a * LANE:(a + 4) * LANE, j * LANE:(j + 1) * LANE]
             for j, a in enumerate(starts)]
    return jnp.stack(tiles, axis=1).astype(BF16)


def _row_tile(T):
    return 512 if T % 512 == 0 else T


RESIDUAL = dict(epilogue="residual")


def out_proj(a, w, x, mod, gate_blk, tk, tail, tm=None):
    B, S, D = x.shape
    T = B * S
    out = matmul(a.reshape(T, -1), w.astype(BF16), tm=tm or _row_tile(T), tn=D, tk=tk,
                 res=x.reshape(T, D), mod=mod, gate_blk=gate_blk, rows_per_batch=S, **tail)
    if isinstance(out, (list, tuple)):
        return tuple(o.reshape(B, S, D) for o in out)
    return out.reshape(B, S, D)


def nsa_mixer(hn, x, mod, gate_blk, w_in, cmp_pe, cmp_w1, cmp_w2, w_out, cos2, sin2,
              tail=RESIDUAL):
    B, S, D = x.shape
    T = B * S
    G = NSA_KV_GROUPS
    hq = NSA_HEADS * HEAD_DIM
    kvw = NSA_KV_WIDTH
    main = hq + 6 * kvw
    tm = _row_tile(T)
    hn2 = hn.reshape(T, D)
    w_gate = jnp.pad(w_in[:, main:], ((0, 0), (0, LANE - 3 * NSA_HEADS))).astype(BF16)
    gate_logits = matmul(hn2, w_gate, tm=tm, tn=LANE, tk=D).reshape(B, S, LANE)

    q_cmp, q_rot = proj_heads(hn2, w_in[:, :hq].astype(BF16), cos2, sin2,
                              [(SCALE_Q,) * NSA_HEADS, (ROPE_Q,) * NSA_HEADS], tm=tm)
    g_rope, g_cast = (ROPE,) * G, (CAST,) * G
    (kvb,) = proj_heads(hn2, w_in[:, hq + 2 * kvw:main].astype(BF16), cos2, sin2,
                        [g_rope + g_cast + g_rope + g_cast], tm=tm)
    q_rot = q_rot.reshape(B, 1, S, hq)
    kvb = kvb.reshape(B, 1, S, 4 * kvw)

    npiece = S // CMP_STRIDE
    kv_cmp = matmul(hn2, w_in[:, hq:hq + 2 * kvw].astype(BF16), tm=tm, tn=2 * kvw, tk=D)
    kvc = compress(kv_cmp.reshape(B, S, 2 * kvw), cmp_pe, cmp_w1.astype(BF16),
                   cmp_w2.astype(BF16))

    n_sb = S // SEL_BLOCK
    cmp_start = np.arange(npiece) * CMP_STRIDE
    sel_start = np.arange(n_sb) * SEL_BLOCK
    overlap = ((cmp_start[:, None] < sel_start[None, :] + SEL_BLOCK)
               & (cmp_start[:, None] + CMP_BLOCK > sel_start[None, :]))
    overlap[npiece - 1] = False
    overlap_t = jnp.asarray(overlap.T, BF16)
    onehot = jnp.asarray(np.arange(S)[:, None] // SEL_BLOCK == np.arange(n_sb)[None, :], BF16)

    o_cmp, imp_t = cmp_attention(q_cmp.reshape(B, S, hq), kvc, overlap_t)
    mfeat = topk_select(imp_t)
    kb = kvw // LANE
    o_sel = sel_attention(q_rot.reshape(B, S, hq), mfeat, kvb.reshape(B, S, 4 * kvw),
                          onehot, k_blk=0, v_blk=kb)
    (o_win,) = banded_attention(q_rot, kvb, kvb, n_kv=G, R=NSA_REP, k_blk=2 * kb, v_blk=3 * kb,
                                nwin=(NSA_WINDOW - 1 + LANE - 1) // LANE,
                                window=NSA_WINDOW - 1, want_lse=False)
    o = nsa_combine(gate_logits, o_cmp, o_sel, o_win.reshape(B, S, hq))
    return out_proj(o, w_out, x, mod, gate_blk, hq, tail)


def dilated_mixer(hn, x, mod, gate_blk, w_in, w_out, cos2, sin2, tail=RESIDUAL):
    B, S, D = x.shape
    T = B * S
    H = DIL_HEADS
    hw = H * HEAD_DIM
    tm = _row_tile(T)
    outs, lses = [], []
    for g, (window, dil) in enumerate(DIL_PATTERNS):
        w = window // dil
        L = S // dil
        regroup = lambda t: t.reshape(t.shape[:-2] + (L, dil, t.shape[-1])).swapaxes(-3, -2)
        hn_g = regroup(hn).reshape(T, D)
        cos_g, sin_g = regroup(cos2).reshape(S, LANE), regroup(sin2).reshape(S, LANE)
        q, k, v = [proj_heads(hn_g, w_in, cos_g, sin_g, [(kind,) * H], tm=tm,
                              col_blk=3 * g + c)[0].reshape(B, dil, L, hw)
                   for c, kind in enumerate((ROPE_Q, ROPE, CAST))]
        o, lse = banded_attention(q, k, v, n_kv=H, R=1, nwin=(w + LANE - 1) // LANE, window=w,
                                  want_lse=True)
        outs.append(o)
        lses.append(lse)
    o = dil_combine(outs, lses)
    return out_proj(o, w_out, x, mod, gate_blk, hw, tail)


def rglru_mixer(hn, x, mod, gate_blk, w_in, conv_w, conv_b, w_gate, b_gate, lam, w_out,
                tail=RESIDUAL):
    B, S, D = x.shape
    T = B * S
    tm = _row_tile(T)
    proj = matmul(hn.reshape(T, D), w_in.astype(BF16), tm=tm, tn=D_RNN, tk=D)
    hy = rglru_scan(proj.reshape(B, S, 2 * D_RNN), conv_w, conv_b, _gate_band_weights(w_gate),
                    b_gate, lam)
    return out_proj(hy, w_out, x, mod, gate_blk, D_RNN, tail)


def mlp(hn, x, mod, gate_blk, w1, w2, tail=RESIDUAL):
    B, S, D = x.shape
    T = B * S
    h = matmul(hn.reshape(T, D), w1.astype(BF16), tm=2 * _row_tile(T) if T % 1024 == 0 else T,
               tn=2048, tk=D, out_dtype=BF16, epilogue="relu2")
    return out_proj(h, w2, x, mod, gate_blk, 1024, tail, tm=1024 if T % 1024 == 0 else None)


def kernel(x, c, l0_w_ada, l0_b_ada, l0_norm1, l0_w_in, l0_cmp_pe, l0_cmp_w1, l0_cmp_w2, l0_w_out, l0_norm2, l0_w_ff1, l0_w_ff2, l1_w_ada, l1_b_ada, l1_norm1, l1_w_in, l1_w_out, l1_norm2, l1_w_ff1, l1_w_ff2, l2_w_ada, l2_b_ada, l2_norm1, l2_w_in, l2_conv_w, l2_conv_b, l2_w_gate, l2_b_gate, l2_lambda, l2_w_out, l2_norm2, l2_w_ff1, l2_w_ff2, l3_w_ada, l3_b_ada, l3_norm1, l3_w_in, l3_cmp_pe, l3_cmp_w1, l3_cmp_w2, l3_w_out, l3_norm2, l3_w_ff1, l3_w_ff2, norm_f):
    B, S, D = x.shape
    cos2, sin2 = rope_tables(S)
    layers = (
        (l0_w_ada, l0_b_ada, l0_norm1, l0_norm2, l0_w_ff1, l0_w_ff2,
         (l0_w_in, l0_cmp_pe, l0_cmp_w1, l0_cmp_w2, l0_w_out)),
        (l1_w_ada, l1_b_ada, l1_norm1, l1_norm2, l1_w_ff1, l1_w_ff2, (l1_w_in, l1_w_out)),
        (l2_w_ada, l2_b_ada, l2_norm1, l2_norm2, l2_w_ff1, l2_w_ff2,
         (l2_w_in, l2_conv_w, l2_conv_b, l2_w_gate, l2_b_gate, l2_lambda, l2_w_out)),
        (l3_w_ada, l3_b_ada, l3_norm1, l3_norm2, l3_w_ff1, l3_w_ff2,
         (l3_w_in, l3_cmp_pe, l3_cmp_w1, l3_cmp_w2, l3_w_out)),
    )
    mods = [adaln(c, l[0], l[1]).reshape(B, 1, 6 * D) for l in layers]
    hn = modulate(x, layers[0][2], mods[0], 0, 1)
    for li in range(DEPTH):
        _, _, _, n2, ff1, ff2, mix = layers[li]
        mod = mods[li]
        tail = dict(epilogue="residual_norm", norm_gain=n2, norm_mod=mod, shift_blk=3, scale_blk=4)
        kind = li % N_MIXERS
        if kind == 0:
            x, hn = nsa_mixer(hn, x, mod, 2, *mix, cos2, sin2, tail=tail)
        elif kind == 1:
            x, hn = dilated_mixer(hn, x, mod, 2, *mix, cos2, sin2, tail=tail)
        else:
            x, hn = rglru_mixer(hn, x, mod, 2, *mix, tail=tail)
        if li + 1 < DEPTH:
            tail = dict(epilogue="residual_norm", norm_gain=layers[li + 1][2],
                        norm_mod=mods[li + 1], shift_blk=0, scale_blk=1)
            x, hn = mlp(hn, x, mod, 5, ff1, ff2, tail=tail)
        else:
            return mlp(hn, x, mod, 5, ff1, ff2,
                       tail=dict(epilogue="residual_final", norm_gain=norm_f))
```

```python
import functools
import math

import jax
import jax.numpy as jnp
import numpy as np
from jax import lax
from jax.experimental import pallas as pl
from jax.experimental.pallas import tpu as pltpu

F32 = jnp.float32
BF16 = jnp.bfloat16

D_MODEL = 2048
DEPTH = 4
N_MIXERS = 3
HEAD_DIM = 128
ROPE_THETA = 10000.0
NORM_EPS = 1e-6

NSA_HEADS = D_MODEL // HEAD_DIM
NSA_KV_GROUPS = 4
NSA_REP = NSA_HEADS // NSA_KV_GROUPS
NSA_KV_WIDTH = NSA_KV_GROUPS * HEAD_DIM
CMP_BLOCK = 32
CMP_STRIDE = 16
CMP_HIDDEN = 4 * HEAD_DIM
SEL_BLOCK = 64
SEL_TOPK = 16
NSA_WINDOW = 512
FORCED_BONUS = 1e9

DIL_HEADS = D_MODEL // HEAD_DIM
DIL_PATTERNS = ((128, 1), (512, 4), (2048, 16))

D_RNN = 2688
RNN_BLOCKS = 16
RNN_BLOCK_DIM = D_RNN // RNN_BLOCKS
CONV_WIDTH = 4
LRU_C = 8.0

LANE = 128
LOG2E = math.log2(math.e)
QK_SCALE = LOG2E / math.sqrt(HEAD_DIM)
NEG_BIG = -1e30
SEL_OFF = -float(2 ** 30)
VMEM_LIMIT = 56 * 1024 * 1024

NT_DIMS = (((1,), (1,)), ((), ()))


def _cparams(sem):
    return pltpu.CompilerParams(dimension_semantics=sem, vmem_limit_bytes=VMEM_LIMIT)


def _adaln_kernel(c_ref, w_ref, b_ref, o_ref):
    w = w_ref[...]
    for b in range(c_ref.shape[0]):
        c = c_ref[b]
        cond = c * jax.nn.sigmoid(c)
        o_ref[b:b + 1, :] = jnp.sum(w * cond, axis=0, keepdims=True) + b_ref[...]


def adaln(c, w_ada, b_ada, tn=1024):
    B, D = c.shape
    N = w_ada.shape[1]
    return pl.pallas_call(
        _adaln_kernel,
        grid=(N // tn,),
        in_specs=[pl.BlockSpec((B, D, 1), lambda j: (0, 0, 0)),
                  pl.BlockSpec((D, tn), lambda j: (0, j)),
                  pl.BlockSpec((1, tn), lambda j: (0, j))],
        out_specs=pl.BlockSpec((B, tn), lambda j: (0, j)),
        out_shape=jax.ShapeDtypeStruct((B, N), F32),
        compiler_params=_cparams(("parallel",)),
        name="adaln",
    )(c.reshape(B, D, 1), w_ada, b_ada.reshape(1, N))


def _modulate_kernel(x_ref, gain_ref, sh_ref, sc_ref, o_ref):
    x = x_ref[...]
    ms = jnp.mean(x * x, axis=-1, keepdims=True)
    y = x * lax.rsqrt(ms + NORM_EPS) * gain_ref[...]
    o_ref[...] = (y * (1.0 + sc_ref[...]) + sh_ref[...]).astype(o_ref.dtype)


def modulate(x, gain, mod, shift_blk, scale_blk, ts=512):
    B, S, D = x.shape
    return pl.pallas_call(
        _modulate_kernel,
        grid=(B, S // ts),
        in_specs=[pl.BlockSpec((None, ts, D), lambda b, i: (b, i, 0)),
                  pl.BlockSpec((1, D), lambda b, i: (0, 0)),
                  pl.BlockSpec((None, 1, D), lambda b, i: (b, 0, shift_blk)),
                  pl.BlockSpec((None, 1, D), lambda b, i: (b, 0, scale_blk))],
        out_specs=pl.BlockSpec((None, ts, D), lambda b, i: (b, i, 0)),
        out_shape=jax.ShapeDtypeStruct((B, S, D), BF16),
        compiler_params=_cparams(("parallel", "parallel")),
        name="modulate",
    )(x, gain.reshape(1, D), mod, mod)


def _mm_kernel(*refs, nk, epilogue):
    refs = list(refs)
    a_ref, w_ref = refs[:2]
    del refs[:2]
    if epilogue.startswith("residual"):
        res_ref, gate_ref = refs[:2]
        del refs[:2]
    if epilogue == "residual_norm":
        gain_ref, sh_ref, sc_ref = refs[:3]
        del refs[:3]
    elif epilogue == "residual_final":
        gain_ref = refs.pop(0)
    o_ref = refs.pop(0)
    hn_ref = refs.pop(0) if epilogue == "residual_norm" else None
    rest = refs

    def finish(acc):
        if epilogue == "relu2":
            r = jnp.maximum(acc, 0.0)
            acc = r * r
        elif epilogue.startswith("residual"):
            acc = res_ref[...] + gate_ref[...] * acc
        if epilogue in ("residual_norm", "residual_final"):
            ms = jnp.mean(acc * acc, axis=-1, keepdims=True)
            y = acc * lax.rsqrt(ms + NORM_EPS) * gain_ref[...]
            if epilogue == "residual_final":
                o_ref[...] = y
                return
            hn_ref[...] = (y * (1.0 + sc_ref[...]) + sh_ref[...]).astype(hn_ref.dtype)
        o_ref[...] = acc.astype(o_ref.dtype)

    def part():
        return jnp.dot(a_ref[...], w_ref[...], preferred_element_type=F32)

    if nk == 1:
        finish(part())
        return
    acc_ref = rest[0] if rest else o_ref
    k = pl.program_id(2)

    @pl.when(k == 0)
    def _():
        acc_ref[...] = part()

    if nk > 2:
        @pl.when(jnp.logical_and(k > 0, k < nk - 1))
        def _():
            acc_ref[...] += part()

    @pl.when(k == nk - 1)
    def _():
        finish(acc_ref[...] + part())


def matmul(a, w, *, tm, tn, tk, out_dtype=F32, epilogue="none", res=None, mod=None,
           gate_blk=0, rows_per_batch=None, norm_gain=None, norm_mod=None, shift_blk=0,
           scale_blk=0):
    M, K = a.shape
    N = w.shape[1]
    nk = K // tk
    assert M % tm == 0 and N % tn == 0 and K % tk == 0
    in_specs = [pl.BlockSpec((tm, tk), lambda i, j, k: (i, k)),
                pl.BlockSpec((tk, tn), lambda i, j, k: (k, j))]
    args = [a, w]
    out_spec = pl.BlockSpec((tm, tn), lambda i, j, k: (i, j))
    out_specs, out_shape = out_spec, jax.ShapeDtypeStruct((M, N), out_dtype)
    if epilogue.startswith("residual"):
        assert rows_per_batch % tm == 0 and tn == D_MODEL == N
        batch = lambda i: i * tm // rows_per_batch
        chunk = lambda blk: pl.BlockSpec((None, 1, tn), lambda i, j, k: (batch(i), 0, blk))
        in_specs += [out_spec, chunk(gate_blk)]
        args += [res, mod]
        if epilogue != "residual":
            in_specs.append(pl.BlockSpec((1, tn), lambda i, j, k: (0, 0)))
            args.append(norm_gain.reshape(1, tn))
        if epilogue == "residual_norm":
            in_specs += [chunk(shift_blk), chunk(scale_blk)]
            args += [norm_mod, norm_mod]
            out_specs = [out_spec, out_spec]
            out_shape = [out_shape, jax.ShapeDtypeStruct((M, N), BF16)]
    scratch = [pltpu.VMEM((tm, tn), F32)] if nk > 1 and out_dtype != F32 else []
    return pl.pallas_call(
        functools.partial(_mm_kernel, nk=nk, epilogue=epilogue),
        grid=(M // tm, N // tn, nk),
        in_specs=in_specs,
        out_specs=out_specs,
        out_shape=out_shape,
        scratch_shapes=scratch,
        compiler_params=_cparams(("parallel", "parallel", "arbitrary")),
        name="mm_" + epilogue,
    )(*args)


def rope_tables(S):
    inv_freq = ROPE_THETA ** (-jnp.arange(0, HEAD_DIM, 2, dtype=F32) / HEAD_DIM)
    ang = jnp.arange(S, dtype=F32)[:, None] * inv_freq[None, :]
    cos, sin = jnp.cos(ang), jnp.sin(ang)
    return jnp.concatenate([cos, cos], axis=-1), jnp.concatenate([-sin, sin], axis=-1)


CAST, ROPE, ROPE_Q, SCALE_Q = 0, 1, 2, 3


def _proj_heads_kernel(a_ref, w_ref, c_ref, s_ref, *o_refs, kinds):
    if len(o_refs) > len(kinds):
        o_refs, wb_ref = o_refs[:-1], o_refs[-1]

        @pl.when(pl.program_id(0) == 0)
        def _():
            wb_ref[...] = w_ref[...].astype(BF16)
        w_ref = wb_ref
    acc = jnp.dot(a_ref[...], w_ref[...], preferred_element_type=F32)
    c = c_ref[...]
    s = s_ref[...]
    for o_ref, head_kinds in zip(o_refs, kinds):
        for h, kind in enumerate(head_kinds):
            sl = slice(h * LANE, (h + 1) * LANE)
            t = acc[:, sl]
            if kind in (ROPE, ROPE_Q):
                t = t * c + pltpu.roll(t, HEAD_DIM // 2, 1) * s
            if kind in (ROPE_Q, SCALE_Q):
                t = t * QK_SCALE
            o_ref[:, sl] = t.astype(o_ref.dtype)


def proj_heads(a, w, cos2, sin2, kinds, *, tm, col_blk=0):
    M, K = a.shape
    N = len(kinds[0]) * LANE
    nt = cos2.shape[0] // tm
    tab = pl.BlockSpec((tm, LANE), lambda i: (i % nt, 0))
    out_spec = pl.BlockSpec((tm, N), lambda i: (i, 0))
    cast = w.dtype != BF16
    return pl.pallas_call(
        functools.partial(_proj_heads_kernel, kinds=tuple(kinds)),
        grid=(M // tm,),
        in_specs=[pl.BlockSpec((tm, K), lambda i: (i, 0)),
                  pl.BlockSpec((K, N), lambda i: (0, col_blk), pipeline_mode=pl.Buffered(1)),
                  tab, tab],
        out_specs=[out_spec] * len(kinds),
        out_shape=[jax.ShapeDtypeStruct((M, N), BF16)] * len(kinds),
        scratch_shapes=[pltpu.VMEM((K, N), BF16)] if cast else [],
        compiler_params=_cparams(("arbitrary",)),
        name="proj_heads",
    )(a, w, cos2, sin2)


def _ones_column(n):
    return jnp.where(lax.broadcasted_iota(jnp.int32, (n, LANE), 1) == 0, 1.0, 0.0).astype(BF16)


def _softmax_pv(s, m, v, ones):
    p = jnp.exp2(s - m).astype(BF16)
    res = jnp.dot(p, jnp.concatenate([v, ones], axis=1), preferred_element_type=F32)
    return res[:, LANE:LANE + 1], res[:, :LANE]


def _banded_kernel(q_ref, k_ref, v_ref, o_ref, *lse_refs, R, hb, nwin, window, tq, qsub, nkeys):
    i = pl.program_id(3)
    L = k_ref.shape[0]
    rows = R * qsub
    rel = (lax.broadcasted_iota(jnp.int32, (rows, nkeys), 0) & (qsub - 1)) \
        - lax.broadcasted_iota(jnp.int32, (rows, nkeys), 1)
    ones = _ones_column(nkeys)
    units = [(h, j) for j in range(tq // qsub) for h in range(hb)]
    lane = lax.broadcasted_iota(jnp.int32, (rows, LANE), 1)
    lse_tile = jnp.zeros((rows, LANE), F32)

    def scores(h, j):
        q0 = i * tq + j * qsub
        kstart = pl.multiple_of(jnp.clip(q0 - nwin * LANE, 0, L - nkeys), LANE)
        qj = q_ref[j * qsub:(j + 1) * qsub, h * R * LANE:(h + 1) * R * LANE]
        if R > 1:
            q = jnp.concatenate([qj[:, r * LANE:(r + 1) * LANE] for r in range(R)], axis=0)
        else:
            q = qj
        s = lax.dot_general(q, k_ref[pl.ds(kstart, nkeys), h * LANE:(h + 1) * LANE], NT_DIMS,
                            preferred_element_type=F32)
        return q0, kstart, s

    ahead = 2
    pending = [scores(*u) for u in units[:ahead]]
    for n, (h, j) in enumerate(units):
        if n + ahead < len(units):
            pending.append(scores(*units[n + ahead]))
        q0, kstart, s = pending[n]
        v = v_ref[pl.ds(kstart, nkeys), h * LANE:(h + 1) * LANE]
        diff = rel + (q0 - kstart)
        valid = lax.bitcast_convert_type(diff, jnp.uint32) <= jnp.uint32(window)
        s = jnp.where(valid, s, NEG_BIG)
        m = jnp.max(s, axis=-1, keepdims=True)
        l, o = _softmax_pv(s, m, v, ones)
        o = o / l
        for r in range(R):
            col = (h * R + r) * LANE
            o_ref[j * qsub:(j + 1) * qsub, col:col + LANE] = o[r * qsub:(r + 1) * qsub]
        if lse_refs:
            lse_tile = jnp.where(lane == h, m + jnp.log(l) * LOG2E, lse_tile)
            if h == hb - 1:
                lse_refs[0][j * qsub:(j + 1) * qsub, :] = lse_tile


def banded_attention(q_arr, k_arr, v_arr, *, n_kv, R, nwin, window, want_lse, q_blk=0, k_blk=0,
                     v_blk=0):
    B, dil, L, _ = q_arr.shape
    tq = min(512, L)
    qsub = LANE
    nkeys = min(qsub + nwin * LANE, L)
    hb = 4 if R == 1 else 1
    assert L % tq == 0 and n_kv % hb == 0
    assert q_blk % (hb * R) == 0 and k_blk % hb == 0 and v_blk % hb == 0
    width = n_kv * R * LANE
    out_shape = [jax.ShapeDtypeStruct((B, dil, L, width), F32)]
    out_specs = [pl.BlockSpec((None, None, tq, hb * R * LANE), lambda b, r, h, i: (b, r, i, h))]
    if want_lse:
        out_shape.append(jax.ShapeDtypeStruct((B, dil, L, n_kv // hb * LANE), F32))
        out_specs.append(pl.BlockSpec((None, None, tq, LANE), lambda b, r, h, i: (b, r, i, h)))
    kv_spec = lambda blk: pl.BlockSpec((None, None, L, hb * LANE),
                                       lambda b, r, h, i: (b, r, 0, blk // hb + h))
    return pl.pallas_call(
        functools.partial(_banded_kernel, R=R, hb=hb, nwin=nwin, window=window, tq=tq, qsub=qsub,
                          nkeys=nkeys),
        grid=(B, dil, n_kv // hb, L // tq),
        in_specs=[pl.BlockSpec((None, None, tq, hb * R * LANE),
                               lambda b, r, h, i: (b, r, i, q_blk // (hb * R) + h)),
                  kv_spec(k_blk), kv_spec(v_blk)],
        out_specs=out_specs,
        out_shape=out_shape,
        compiler_params=_cparams(("parallel", "parallel", "parallel", "arbitrary")),
        name="banded_attn",
    )(q_arr, k_arr, v_arr)


def _compress_kernel(x_ref, pe_ref, w1_ref, w2_ref, o_ref):
    n = x_ref.shape[0] // CMP_STRIDE
    a = jnp.zeros((n, CMP_HIDDEN), F32)
    b = jnp.zeros((n, CMP_HIDDEN), F32)
    for j in range(CMP_STRIDE):
        xj = x_ref[pl.ds(j, n, stride=CMP_STRIDE), :]
        top = (xj + pe_ref[j:j + 1, :]).astype(BF16)
        bot = (xj + pe_ref[CMP_STRIDE + j:CMP_STRIDE + j + 1, :]).astype(BF16)
        a = a + jnp.dot(top, w1_ref[j * HEAD_DIM:(j + 1) * HEAD_DIM, :],
                        preferred_element_type=F32)
        b = b + jnp.dot(bot, w1_ref[(CMP_STRIDE + j) * HEAD_DIM:(CMP_STRIDE + j + 1) * HEAD_DIM, :],
                        preferred_element_type=F32)
    hid = a + pltpu.roll(b, n - 1, 0)
    o_ref[...] = jnp.dot(jax.nn.gelu(hid).astype(BF16), w2_ref[...], preferred_element_type=F32)


def compress(kv, pe, w1, w2):
    B, S, _ = kv.shape
    G = NSA_KV_GROUPS
    NP = S // CMP_STRIDE
    return pl.pallas_call(
        _compress_kernel,
        grid=(2, B, G),
        in_specs=[pl.BlockSpec((None, S, HEAD_DIM), lambda t, b, g: (b, 0, t * G + g)),
                  pl.BlockSpec((None, CMP_BLOCK, HEAD_DIM), lambda t, b, g: (t, 0, 0)),
                  pl.BlockSpec((None, CMP_BLOCK * HEAD_DIM, CMP_HIDDEN), lambda t, b, g: (t, 0, 0)),
                  pl.BlockSpec((None, CMP_HIDDEN, HEAD_DIM), lambda t, b, g: (t, 0, 0))],
        out_specs=pl.BlockSpec((None, None, None, NP, HEAD_DIM), lambda t, b, g: (t, b, g, 0, 0)),
        out_shape=jax.ShapeDtypeStruct((2, B, G, NP, HEAD_DIM), F32),
        compiler_params=_cparams(("parallel", "parallel", "parallel")),
        name="nsa_compress",
    )(kv, pe, w1, w2)


def _cmp_attn_kernel(q_ref, kc_ref, vc_ref, ov_ref, o_ref, imp_ref, *, tq, qsub):
    i = pl.program_id(2)
    R = NSA_REP
    nsub = tq // qsub
    rows = R * qsub
    n_var = kc_ref.shape[0] // LANE

    def run(nk):
        kc = kc_ref[0:nk, :].astype(BF16)
        vc = vc_ref[0:nk, :].astype(BF16)
        ov = ov_ref[:, 0:nk]
        tloc = lax.broadcasted_iota(jnp.int32, (rows, nk), 0) & (qsub - 1)
        cmp_end = lax.broadcasted_iota(jnp.int32, (rows, nk), 1) * CMP_STRIDE + (CMP_BLOCK - 1)

        def scores(u):
            qu = q_ref[u * qsub:(u + 1) * qsub, :]
            q = jnp.concatenate([qu[:, r * LANE:(r + 1) * LANE] for r in range(R)], axis=0)
            return lax.dot_general(q, kc, NT_DIMS, preferred_element_type=F32)

        ahead = 2
        pending = [scores(u) for u in range(min(ahead, nsub))]
        for u in range(nsub):
            if u + ahead < nsub:
                pending.append(scores(u + ahead))
            valid = cmp_end <= tloc + (i * tq + u * qsub)
            s = jnp.where(valid, pending[u], NEG_BIG)
            m = jnp.max(s, axis=-1, keepdims=True)
            p = jnp.where(valid, jnp.exp2(s - m), 0.0)
            l = jnp.sum(p, axis=-1, keepdims=True)
            p = p / jnp.where(l > 0, l, 1.0)
            o = jnp.dot(p.astype(BF16), vc, preferred_element_type=F32)
            for r in range(R):
                o_ref[u * qsub:(u + 1) * qsub, r * LANE:(r + 1) * LANE] = \
                    o[r * qsub:(r + 1) * qsub]
            psum = p[0:qsub]
            for r in range(1, R):
                psum = psum + p[r * qsub:(r + 1) * qsub]
            p_hi = psum.astype(BF16)
            p_lo = (psum - p_hi.astype(F32)).astype(BF16)
            imp_ref[:, u * qsub:(u + 1) * qsub] = (
                lax.dot_general(ov, p_hi, NT_DIMS, preferred_element_type=F32)
                + lax.dot_general(ov, p_lo, NT_DIMS, preferred_element_type=F32))

    need = ((i + 1) * tq - CMP_BLOCK) // CMP_STRIDE + 1
    var = jnp.clip((need - 1) // LANE, 0, n_var - 1)
    for v in range(n_var):
        pl.when(var == v)(functools.partial(run, (v + 1) * LANE))


def cmp_attention(q, kvc, overlap_t, tq=512, qsub=128):
    B, S, _ = q.shape
    G = NSA_KV_GROUPS
    NP = kvc.shape[3]
    n_sel = overlap_t.shape[0]
    qw = NSA_REP * LANE
    return pl.pallas_call(
        functools.partial(_cmp_attn_kernel, tq=tq, qsub=qsub),
        grid=(B, G, S // tq),
        in_specs=[pl.BlockSpec((None, tq, qw), lambda b, g, i: (b, i, g)),
                  pl.BlockSpec((None, None, None, NP, LANE), lambda b, g, i: (0, b, g, 0, 0)),
                  pl.BlockSpec((None, None, None, NP, LANE), lambda b, g, i: (1, b, g, 0, 0)),
                  pl.BlockSpec((n_sel, NP), lambda b, g, i: (0, 0))],
        out_specs=[pl.BlockSpec((None, tq, qw), lambda b, g, i: (b, i, g)),
                   pl.BlockSpec((None, None, n_sel, tq), lambda b, g, i: (b, g, 0, i))],
        out_shape=[jax.ShapeDtypeStruct((B, S, NSA_HEADS * LANE), F32),
                   jax.ShapeDtypeStruct((B, G, n_sel, S), F32)],
        compiler_params=_cparams(("parallel", "parallel", "parallel")),
        name="nsa_cmp_attn",
    )(q, kvc, kvc, overlap_t)


def _topk_kernel(imp_ref, o_ref, *, tq):
    i = pl.program_id(2)
    imp = imp_ref[...]
    n_sel = imp.shape[0]
    blk = lax.broadcasted_iota(jnp.int32, imp.shape, 0)
    t = i * tq + lax.broadcasted_iota(jnp.int32, imp.shape, 1)
    cur = t // SEL_BLOCK
    avail = blk <= cur
    forced = jnp.where(blk == 0, 1.0, jnp.where(blk == cur, 1.0, jnp.where(blk == cur - 1, 1.0, 0.0)))
    score = jnp.where(avail, imp + FORCED_BONUS * forced, -jnp.inf)
    picked = jnp.zeros(imp.shape, F32)
    for _ in range(min(SEL_TOPK, n_sel)):
        mx = jnp.max(score, axis=0, keepdims=True)
        first = jnp.min(jnp.where(score == mx, blk, n_sel), axis=0, keepdims=True)
        hit = blk == first
        picked = jnp.where(hit, 1.0, picked)
        score = jnp.where(hit, -jnp.inf, score)
    feat = jnp.where(avail, jnp.where(picked > 0.0, 0.0, SEL_OFF), SEL_OFF)
    o_ref[...] = feat.T.astype(o_ref.dtype)


def topk_select(imp_t, tq=1024):
    B, G, n_sel, S = imp_t.shape
    tq = min(tq, S)
    return pl.pallas_call(
        functools.partial(_topk_kernel, tq=tq),
        grid=(B, G, S // tq),
        in_specs=[pl.BlockSpec((None, None, n_sel, tq), lambda b, g, i: (b, g, 0, i))],
        out_specs=pl.BlockSpec((None, None, tq, n_sel), lambda b, g, i: (b, g, i, 0)),
        out_shape=jax.ShapeDtypeStruct((B, G, S, n_sel), BF16),
        compiler_params=_cparams(("parallel", "parallel", "parallel")),
        name="nsa_topk",
    )(imp_t)


def _sel_attn_kernel(q_ref, mf_ref, k_ref, e_ref, vt_ref, o_ref, *, tq, qsub, tkv):
    i = pl.program_id(2)
    R = NSA_REP
    nsub = tq // qsub
    rows = R * qsub
    q0 = i * tq
    ntile = (q0 + tq - 1) // tkv + 1
    rel = (lax.broadcasted_iota(jnp.int32, (tkv, rows), 1) & (qsub - 1)) \
        - lax.broadcasted_iota(jnp.int32, (tkv, rows), 0)

    qa = []
    for u in range(nsub):
        qu = q_ref[u * qsub:(u + 1) * qsub, :]
        mf = mf_ref[u * qsub:(u + 1) * qsub, :]
        qa.append(jnp.concatenate(
            [jnp.concatenate([qu[:, r * LANE:(r + 1) * LANE], mf], axis=1) for r in range(R)],
            axis=0))

    def body(j, carry, masked):
        ks = pl.multiple_of(j * tkv, tkv)
        ka = jnp.concatenate([k_ref[pl.ds(ks, tkv), :], e_ref[pl.ds(ks, tkv), :]], axis=1)
        vt = vt_ref[:, pl.ds(ks, tkv)]
        out = []
        nkeys = lambda u: (u + 1) * qsub if masked else tkv
        scores = lambda u: lax.dot_general(ka[:nkeys(u)], qa[u], NT_DIMS,
                                           preferred_element_type=F32)
        ahead = 2
        sts = [scores(u) for u in range(min(ahead, nsub))]
        for u in range(nsub):
            m, l, acc = carry[u]
            if u + ahead < nsub:
                sts.append(scores(u + ahead))
            st = sts[u]
            if masked:
                st = jnp.where(rel[:nkeys(u)] >= -u * qsub, st, NEG_BIG)
            m_new = jnp.maximum(m, jnp.max(st, axis=0, keepdims=True))
            a = jnp.exp2(m - m_new)
            res = jnp.dot(vt[:, :nkeys(u)], jnp.exp2(st - m_new).astype(BF16),
                          preferred_element_type=F32)
            out.append((m_new, a * l + res[LANE:LANE + 1, :], a * acc + res[:LANE, :]))
        return tuple(out)

    init = tuple((jnp.full((1, rows), NEG_BIG, F32), jnp.zeros((1, rows), F32),
                  jnp.zeros((LANE, rows), F32)) for _ in range(nsub))
    carry = lax.fori_loop(0, ntile - 1, lambda j, c: body(j, c, False), init)
    final = body(ntile - 1, carry, True)
    for u in range(nsub):
        _, l, acc = final[u]
        ot = acc / l
        for r in range(R):
            o_ref[u * qsub:(u + 1) * qsub, r * LANE:(r + 1) * LANE] = \
                ot[:, r * qsub:(r + 1) * qsub].T


def sel_attention(q_rot, mfeat, kvb, onehot, *, k_blk, v_blk, tq=1024, qsub=128, tkv=1024):
    B, S, _ = q_rot.shape
    G = NSA_KV_GROUPS
    n_sel = onehot.shape[1]
    qw = NSA_REP * LANE
    tkv = min(tkv, S)
    tq = min(tq, S)
    assert tkv == tq and S % tkv == 0
    v = kvb[:, :, v_blk * LANE:(v_blk + G) * LANE].reshape(B, S, G, LANE).transpose(0, 2, 3, 1)
    pad = jnp.zeros((B, G, 16, S), BF16).at[:, :, 0, :].set(1.0)
    vt = jnp.concatenate([v, pad], axis=2)
    return pl.pallas_call(
        functools.partial(_sel_attn_kernel, tq=tq, qsub=qsub, tkv=tkv),
        grid=(B, G, S // tq),
        in_specs=[pl.BlockSpec((None, tq, qw), lambda b, g, i: (b, i, g)),
                  pl.BlockSpec((None, None, tq, n_sel), lambda b, g, i: (b, g, i, 0)),
                  pl.BlockSpec((None, S, LANE), lambda b, g, i: (b, 0, k_blk + g)),
                  pl.BlockSpec((S, n_sel), lambda b, g, i: (0, 0)),
                  pl.BlockSpec((None, None, LANE + 16, S), lambda b, g, i: (b, g, 0, 0))],
        out_specs=pl.BlockSpec((None, tq, qw), lambda b, g, i: (b, i, g)),
        out_shape=jax.ShapeDtypeStruct((B, S, NSA_HEADS * LANE), F32),
        compiler_params=_cparams(("parallel", "parallel", "arbitrary")),
        name="nsa_sel_attn",
    )(q_rot, mfeat, kvb, onehot, vt)


def _nsa_combine_kernel(g_ref, oc_ref, os_ref, ow_ref, o_ref):
    gates = jax.nn.sigmoid(g_ref[...])
    for h in range(NSA_HEADS):
        sl = slice(h * LANE, (h + 1) * LANE)
        acc = gates[:, 3 * h:3 * h + 1] * oc_ref[:, sl]
        acc = acc + gates[:, 3 * h + 1:3 * h + 2] * os_ref[:, sl]
        acc = acc + gates[:, 3 * h + 2:3 * h + 3] * ow_ref[:, sl]
        o_ref[:, sl] = acc.astype(o_ref.dtype)


def nsa_combine(gate_logits, o_cmp, o_sel, o_win, ts=256):
    B, S, W = o_cmp.shape
    spec = pl.BlockSpec((None, ts, W), lambda b, i: (b, i, 0))
    return pl.pallas_call(
        _nsa_combine_kernel,
        grid=(B, S // ts),
        in_specs=[pl.BlockSpec((None, ts, LANE), lambda b, i: (b, i, 0)), spec, spec, spec],
        out_specs=spec,
        out_shape=jax.ShapeDtypeStruct((B, S, W), BF16),
        compiler_params=_cparams(("parallel", "parallel")),
        name="nsa_combine",
    )(gate_logits, o_cmp, o_sel, o_win)


def _dil_combine_kernel(*refs, dils, hb):
    n = len(dils)
    o_refs, lse_refs, out_ref = refs[:n], refs[n:2 * n], refs[2 * n]
    nat_o, nat_l = refs[2 * n + 1:3 * n + 1], refs[3 * n + 1:]
    ts = out_ref.shape[0]
    for h in range(out_ref.shape[1] // LANE):
        sl = slice(h * LANE, (h + 1) * LANE)
        lane = (h // hb) * LANE + h % hb
        for g, dil in enumerate(dils):
            for r in range(dil):
                rows = pl.ds(r, ts // dil, stride=dil) if dil > 1 else slice(None)
                nat_o[g][rows, :] = o_refs[g][r, :, sl]
                nat_l[g][rows, :] = jnp.broadcast_to(lse_refs[g][r, :, lane:lane + 1],
                                                     (ts // dil, LANE))
        lses = [r[...] for r in nat_l]
        mx = functools.reduce(jnp.maximum, lses)
        ws = [jnp.exp2(l - mx) for l in lses]
        den = functools.reduce(lambda a, b: a + b, ws)
        num = functools.reduce(lambda a, b: a + b, [w * r[...] for w, r in zip(ws, nat_o)])
        out_ref[:, sl] = (num / den).astype(out_ref.dtype)


def dil_combine(outs, lses, ts=128):
    dils = tuple(o.shape[1] for o in outs)
    B, W = outs[0].shape[0], outs[0].shape[-1]
    S = outs[0].shape[1] * outs[0].shape[2]
    WL = lses[0].shape[-1]
    hb = (W // LANE) // (WL // LANE)
    spec = lambda w: [pl.BlockSpec((None, d, ts // d, w), lambda b, i: (b, 0, i, 0)) for d in dils]
    return pl.pallas_call(
        functools.partial(_dil_combine_kernel, dils=dils, hb=hb),
        grid=(B, S // ts),
        in_specs=spec(W) + spec(WL),
        out_specs=pl.BlockSpec((None, ts, W), lambda b, i: (b, i, 0)),
        out_shape=jax.ShapeDtypeStruct((B, S, W), BF16),
        scratch_shapes=[pltpu.VMEM((ts, LANE), F32)] * (2 * len(dils)),
        compiler_params=_cparams(("parallel", "parallel")),
        name="dil_combine",
    )(*outs, *lses)


def _softplus(x):
    return jnp.maximum(x, 0.0) + jnp.log1p(jnp.exp(-jnp.abs(x)))


def _gate_band_starts():
    ntile = D_RNN // LANE
    starts = []
    for j in range(ntile):
        n_lo = (j * LANE) // RNN_BLOCK_DIM
        n_hi = (j * LANE + LANE - 1) // RNN_BLOCK_DIM
        lo = (n_lo * RNN_BLOCK_DIM) // LANE
        hi = -(-((n_hi + 1) * RNN_BLOCK_DIM) // LANE)
        assert hi - lo <= 4
        starts.append(min(lo, ntile - 4))
    return starts


def _rglru_kernel(y_ref, xr_ref, cw_ref, cb_ref, wg_ref, bg_ref, lam_ref, o_ref,
                  h_ref, tail_ref, *, ts, starts):
    i = pl.program_id(1)

    @pl.when(i == 0)
    def _():
        h_ref[...] = jnp.zeros_like(h_ref)
        tail_ref[...] = jnp.zeros_like(tail_ref)

    xr = xr_ref[...]
    ext = jnp.concatenate([tail_ref[...], xr], axis=0)
    x = cb_ref[...] + xr * cw_ref[CONV_WIDTH - 1:CONV_WIDTH, :]
    for d in range(1, CONV_WIDTH):
        shifted = pltpu.roll(ext, d, 0)[8:8 + ts]
        x = x + shifted * cw_ref[CONV_WIDTH - 1 - d:CONV_WIDTH - d, :]
    tail_ref[...] = xr[ts - 8:ts]

    xb = x.astype(BF16)
    gl = []
    for g in range(2):
        tiles = [jnp.dot(xb[:, a * LANE:(a + 4) * LANE], wg_ref[g, j],
                         preferred_element_type=F32) for j, a in enumerate(starts)]
        gl.append(jnp.concatenate(tiles, axis=1) + bg_ref[g:g + 1, :])
    r = jax.nn.sigmoid(gl[0])
    ig = jax.nn.sigmoid(gl[1])
    log_a = (-LRU_C) * r * _softplus(-lam_ref[...])
    a = jnp.exp(log_a)
    z = -jnp.tanh(log_a) * (a * a + 1.0)
    b = jnp.where(z > 0.0, z * lax.rsqrt(z), 0.0) * (ig * x)

    row = lax.broadcasted_iota(jnp.int32, a.shape, 0)
    d = 1
    while d < ts:
        if d < 8:
            keep = row >= d
            b = b + a * jnp.where(keep, pltpu.roll(b, d, 0), 0.0)
            a = a * jnp.where(keep, pltpu.roll(a, d, 0), 1.0)
        else:
            b = jnp.concatenate([b[:d], b[d:] + a[d:] * b[:ts - d]], axis=0)
            a = jnp.concatenate([a[:d], a[d:] * a[:ts - d]], axis=0)
        d *= 2
    h = a * h_ref[0:1, :] + b
    h_ref[0:1, :] = h[ts - 1:ts, :]
    o_ref[...] = (h * jax.nn.gelu(y_ref[...])).astype(o_ref.dtype)


def rglru_scan(proj, conv_w, conv_b, wband, b_gate, lam, ts=128):
    B, S, _ = proj.shape
    C = D_RNN
    starts = _gate_band_starts()
    vec = lambda n: pl.BlockSpec((n, C), lambda b, i: (0, 0))
    return pl.pallas_call(
        functools.partial(_rglru_kernel, ts=ts, starts=starts),
        grid=(B, S // ts),
        in_specs=[pl.BlockSpec((None, ts, C), lambda b, i: (b, i, 0)),
                  pl.BlockSpec((None, ts, C), lambda b, i: (b, i, 1)),
                  vec(CONV_WIDTH), vec(1),
                  pl.BlockSpec(wband.shape, lambda b, i: (0, 0, 0, 0)),
                  vec(2), vec(1)],
        out_specs=pl.BlockSpec((None, ts, C), lambda b, i: (b, i, 0)),
        out_shape=jax.ShapeDtypeStruct((B, S, C), BF16),
        scratch_shapes=[pltpu.VMEM((8, C), F32), pltpu.VMEM((8, C), F32)],
        compiler_params=_cparams(("arbitrary", "arbitrary")),
        name="rglru_scan",
    )(proj, proj, conv_w, conv_b.reshape(1, C), wband, b_gate, lam.reshape(1, C))


def _gate_band_weights(w_gate):
    starts = _gate_band_starts()
    dense = jnp.stack([jax.scipy.linalg.block_diag(*[w_gate[g, n] for n in range(RNN_BLOCKS)])
                       for g in range(2)])
    tiles = [dense[:, a * LANE:(a + 4) * LANE, j * LANE:(j + 1) * LANE]
             for j, a in enumerate(starts)]
    return jnp.stack(tiles, axis=1).astype(BF16)


def _row_tile(T):
    return 512 if T % 512 == 0 else T


def _proj_tile(S):
    return 1024 if S % 1024 == 0 else _row_tile(S)


RESIDUAL = dict(epilogue="residual")


def out_proj(a, w, x, mod, gate_blk, tk, tail, tm=None):
    B, S, D = x.shape
    T = B * S
    out = matmul(a.reshape(T, -1), w.astype(BF16), tm=tm or _row_tile(T), tn=D, tk=tk,
                 res=x.reshape(T, D), mod=mod, gate_blk=gate_blk, rows_per_batch=S, **tail)
    if isinstance(out, (list, tuple)):
        return tuple(o.reshape(B, S, D) for o in out)
    return out.reshape(B, S, D)


def nsa_mixer(hn, x, mod, gate_blk, w_in, cmp_pe, cmp_w1, cmp_w2, w_out, cos2, sin2,
              tail=RESIDUAL):
    B, S, D = x.shape
    T = B * S
    G = NSA_KV_GROUPS
    hq = NSA_HEADS * HEAD_DIM
    kvw = NSA_KV_WIDTH
    main = hq + 6 * kvw
    tm = _proj_tile(S)
    hn2 = hn.reshape(T, D)
    w_gate = jnp.pad(w_in[:, main:], ((0, 0), (0, LANE - 3 * NSA_HEADS))).astype(BF16)
    gate_logits = matmul(hn2, w_gate, tm=tm, tn=LANE, tk=D).reshape(B, S, LANE)

    q_cmp, q_rot = proj_heads(hn2, w_in[:, :hq].astype(BF16), cos2, sin2,
                              [(SCALE_Q,) * NSA_HEADS, (ROPE_Q,) * NSA_HEADS], tm=tm)
    g_rope, g_cast = (ROPE,) * G, (CAST,) * G
    (kvb,) = proj_heads(hn2, w_in[:, hq + 2 * kvw:main].astype(BF16), cos2, sin2,
                        [g_rope + g_cast + g_rope + g_cast], tm=tm)
    q_rot = q_rot.reshape(B, 1, S, hq)
    kvb = kvb.reshape(B, 1, S, 4 * kvw)

    npiece = S // CMP_STRIDE
    kv_cmp = matmul(hn2, w_in[:, hq:hq + 2 * kvw].astype(BF16), tm=tm, tn=2 * kvw, tk=D)
    kvc = compress(kv_cmp.reshape(B, S, 2 * kvw), cmp_pe, cmp_w1.astype(BF16),
                   cmp_w2.astype(BF16))

    n_sb = S // SEL_BLOCK
    cmp_start = np.arange(npiece) * CMP_STRIDE
    sel_start = np.arange(n_sb) * SEL_BLOCK
    overlap = ((cmp_start[:, None] < sel_start[None, :] + SEL_BLOCK)
               & (cmp_start[:, None] + CMP_BLOCK > sel_start[None, :]))
    overlap[npiece - 1] = False
    overlap_t = jnp.asarray(overlap.T, BF16)
    onehot = jnp.asarray(np.arange(S)[:, None] // SEL_BLOCK == np.arange(n_sb)[None, :], BF16)

    o_cmp, imp_t = cmp_attention(q_cmp.reshape(B, S, hq), kvc, overlap_t)
    mfeat = topk_select(imp_t)
    kb = kvw // LANE
    o_sel = sel_attention(q_rot.reshape(B, S, hq), mfeat, kvb.reshape(B, S, 4 * kvw),
                          onehot, k_blk=0, v_blk=kb)
    (o_win,) = banded_attention(q_rot, kvb, kvb, n_kv=G, R=NSA_REP, k_blk=2 * kb, v_blk=3 * kb,
                                nwin=(NSA_WINDOW - 1 + LANE - 1) // LANE,
                                window=NSA_WINDOW - 1, want_lse=False)
    o = nsa_combine(gate_logits, o_cmp, o_sel, o_win.reshape(B, S, hq))
    return out_proj(o, w_out, x, mod, gate_blk, hq, tail)


def dilated_mixer(hn, x, mod, gate_blk, w_in, w_out, cos2, sin2, tail=RESIDUAL):
    B, S, D = x.shape
    T = B * S
    H = DIL_HEADS
    hw = H * HEAD_DIM
    tm = _proj_tile(S)
    outs, lses = [], []
    for g, (window, dil) in enumerate(DIL_PATTERNS):
        w = window // dil
        L = S // dil
        regroup = lambda t: t.reshape(t.shape[:-2] + (L, dil, t.shape[-1])).swapaxes(-3, -2)
        hn_g = regroup(hn).reshape(T, D)
        cos_g, sin_g = regroup(cos2).reshape(S, LANE), regroup(sin2).reshape(S, LANE)
        q, k, v = [proj_heads(hn_g, w_in, cos_g, sin_g, [(kind,) * H], tm=tm,
                              col_blk=3 * g + c)[0].reshape(B, dil, L, hw)
                   for c, kind in enumerate((ROPE_Q, ROPE, CAST))]
        o, lse = banded_attention(q, k, v, n_kv=H, R=1, nwin=(w + LANE - 1) // LANE, window=w,
                                  want_lse=True)
        outs.append(o)
        lses.append(lse)
    o = dil_combine(outs, lses)
    return out_proj(o, w_out, x, mod, gate_blk, hw, tail)


def rglru_mixer(hn, x, mod, gate_blk, w_in, conv_w, conv_b, w_gate, b_gate, lam, w_out,
                tail=RESIDUAL):
    B, S, D = x.shape
    T = B * S
    tm = _row_tile(T)
    proj = matmul(hn.reshape(T, D), w_in.astype(BF16), tm=tm, tn=D_RNN, tk=D)
    hy = rglru_scan(proj.reshape(B, S, 2 * D_RNN), conv_w, conv_b, _gate_band_weights(w_gate),
                    b_gate, lam)
    return out_proj(hy, w_out, x, mod, gate_blk, D_RNN, tail)


def mlp(hn, x, mod, gate_blk, w1, w2, tail=RESIDUAL):
    B, S, D = x.shape
    T = B * S
    h = matmul(hn.reshape(T, D), w1.astype(BF16), tm=2 * _row_tile(T) if T % 1024 == 0 else T,
               tn=2048, tk=D, out_dtype=BF16, epilogue="relu2")
    return out_proj(h, w2, x, mod, gate_blk, 1024, tail, tm=1024 if T % 1024 == 0 else None)


def kernel(x, c, l0_w_ada, l0_b_ada, l0_norm1, l0_w_in, l0_cmp_pe, l0_cmp_w1, l0_cmp_w2, l0_w_out, l0_norm2, l0_w_ff1, l0_w_ff2, l1_w_ada, l1_b_ada, l1_norm1, l1_w_in, l1_w_out, l1_norm2, l1_w_ff1, l1_w_ff2, l2_w_ada, l2_b_ada, l2_norm1, l2_w_in, l2_conv_w, l2_conv_b, l2_w_gate, l2_b_gate, l2_lambda, l2_w_out, l2_norm2, l2_w_ff1, l2_w_ff2, l3_w_ada, l3_b_ada, l3_norm1, l3_w_in, l3_cmp_pe, l3_cmp_w1, l3_cmp_w2, l3_w_out, l3_norm2, l3_w_ff1, l3_w_ff2, norm_f):
    B, S, D = x.shape
    cos2, sin2 = rope_tables(S)
    layers = (
        (l0_w_ada, l0_b_ada, l0_norm1, l0_norm2, l0_w_ff1, l0_w_ff2,
         (l0_w_in, l0_cmp_pe, l0_cmp_w1, l0_cmp_w2, l0_w_out)),
        (l1_w_ada, l1_b_ada, l1_norm1, l1_norm2, l1_w_ff1, l1_w_ff2, (l1_w_in, l1_w_out)),
        (l2_w_ada, l2_b_ada, l2_norm1, l2_norm2, l2_w_ff1, l2_w_ff2,
         (l2_w_in, l2_conv_w, l2_conv_b, l2_w_gate, l2_b_gate, l2_lambda, l2_w_out)),
        (l3_w_ada, l3_b_ada, l3_norm1, l3_norm2, l3_w_ff1, l3_w_ff2,
         (l3_w_in, l3_cmp_pe, l3_cmp_w1, l3_cmp_w2, l3_w_out)),
    )
    mods = [adaln(c, l[0], l[1]).reshape(B, 1, 6 * D) for l in layers]
    hn = modulate(x, layers[0][2], mods[0], 0, 1)
    for li in range(DEPTH):
        _, _, _, n2, ff1, ff2, mix = layers[li]
        mod = mods[li]
        tail = dict(epilogue="residual_norm", norm_gain=n2, norm_mod=mod, shift_blk=3, scale_blk=4)
        kind = li % N_MIXERS
        if kind == 0:
            x, hn = nsa_mixer(hn, x, mod, 2, *mix, cos2, sin2, tail=tail)
        elif kind == 1:
            x, hn = dilated_mixer(hn, x, mod, 2, *mix, cos2, sin2, tail=tail)
        else:
            x, hn = rglru_mixer(hn, x, mod, 2, *mix, tail=tail)
        if li + 1 < DEPTH:
            tail = dict(epilogue="residual_norm", norm_gain=layers[li + 1][2],
                        norm_mod=mods[li + 1], shift_blk=0, scale_blk=1)
            x, hn = mlp(hn, x, mod, 5, ff1, ff2, tail=tail)
        else:
            return mlp(hn, x, mod, 5, ff1, ff2,
                       tail=dict(epilogue="residual_final", norm_gain=norm_f))
```

```python
import functools
import math

import jax
import jax.numpy as jnp
import numpy as np
from jax import lax
from jax.experimental import pallas as pl
from jax.experimental.pallas import tpu as pltpu

F32 = jnp.float32
BF16 = jnp.bfloat16

D_MODEL = 2048
DEPTH = 4
N_MIXERS = 3
HEAD_DIM = 128
ROPE_THETA = 10000.0
NORM_EPS = 1e-6

NSA_HEADS = D_MODEL // HEAD_DIM
NSA_KV_GROUPS = 4
NSA_REP = NSA_HEADS // NSA_KV_GROUPS
NSA_KV_WIDTH = NSA_KV_GROUPS * HEAD_DIM
CMP_BLOCK = 32
CMP_STRIDE = 16
CMP_HIDDEN = 4 * HEAD_DIM
SEL_BLOCK = 64
SEL_TOPK = 16
NSA_WINDOW = 512
FORCED_BONUS = 1e9

DIL_HEADS = D_MODEL // HEAD_DIM
DIL_PATTERNS = ((128, 1), (512, 4), (2048, 16))

D_RNN = 2688
RNN_BLOCKS = 16
RNN_BLOCK_DIM = D_RNN // RNN_BLOCKS
CONV_WIDTH = 4
LRU_C = 8.0

LANE = 128
LOG2E = math.log2(math.e)
QK_SCALE = LOG2E / math.sqrt(HEAD_DIM)
NEG_BIG = -1e30
SEL_OFF = -float(2 ** 30)
VMEM_LIMIT = 56 * 1024 * 1024

NT_DIMS = (((1,), (1,)), ((), ()))


def _cparams(sem):
    return pltpu.CompilerParams(dimension_semantics=sem, vmem_limit_bytes=VMEM_LIMIT)


def _adaln_kernel(c_ref, w_ref, b_ref, o_ref):
    w = w_ref[...]
    for b in range(c_ref.shape[0]):
        c = c_ref[b]
        cond = c * jax.nn.sigmoid(c)
        o_ref[b:b + 1, :] = jnp.sum(w * cond, axis=0, keepdims=True) + b_ref[...]


def adaln(c, w_ada, b_ada, tn=1024):
    B, D = c.shape
    N = w_ada.shape[1]
    return pl.pallas_call(
        _adaln_kernel,
        grid=(N // tn,),
        in_specs=[pl.BlockSpec((B, D, 1), lambda j: (0, 0, 0)),
                  pl.BlockSpec((D, tn), lambda j: (0, j)),
                  pl.BlockSpec((1, tn), lambda j: (0, j))],
        out_specs=pl.BlockSpec((B, tn), lambda j: (0, j)),
        out_shape=jax.ShapeDtypeStruct((B, N), F32),
        compiler_params=_cparams(("parallel",)),
        name="adaln",
    )(c.reshape(B, D, 1), w_ada, b_ada.reshape(1, N))


def _modulate_kernel(x_ref, gain_ref, sh_ref, sc_ref, o_ref):
    x = x_ref[...]
    ms = jnp.mean(x * x, axis=-1, keepdims=True)
    y = x * lax.rsqrt(ms + NORM_EPS) * gain_ref[...]
    o_ref[...] = (y * (1.0 + sc_ref[...]) + sh_ref[...]).astype(o_ref.dtype)


def modulate(x, gain, mod, shift_blk, scale_blk, ts=512):
    B, S, D = x.shape
    return pl.pallas_call(
        _modulate_kernel,
        grid=(B, S // ts),
        in_specs=[pl.BlockSpec((None, ts, D), lambda b, i: (b, i, 0)),
                  pl.BlockSpec((1, D), lambda b, i: (0, 0)),
                  pl.BlockSpec((None, 1, D), lambda b, i: (b, 0, shift_blk)),
                  pl.BlockSpec((None, 1, D), lambda b, i: (b, 0, scale_blk))],
        out_specs=pl.BlockSpec((None, ts, D), lambda b, i: (b, i, 0)),
        out_shape=jax.ShapeDtypeStruct((B, S, D), BF16),
        compiler_params=_cparams(("parallel", "parallel")),
        name="modulate",
    )(x, gain.reshape(1, D), mod, mod)


def _mm_kernel(*refs, nk, epilogue):
    refs = list(refs)
    a_ref, w_ref = refs[:2]
    del refs[:2]
    if epilogue.startswith("residual"):
        res_ref, gate_ref = refs[:2]
        del refs[:2]
    if epilogue == "residual_norm":
        gain_ref, sh_ref, sc_ref = refs[:3]
        del refs[:3]
    elif epilogue == "residual_final":
        gain_ref = refs.pop(0)
    o_ref = refs.pop(0)
    hn_ref = refs.pop(0) if epilogue == "residual_norm" else None
    rest = refs

    def finish(acc):
        if epilogue == "relu2":
            r = jnp.maximum(acc, 0.0)
            acc = r * r
        elif epilogue.startswith("residual"):
            acc = res_ref[...] + gate_ref[...] * acc
        if epilogue in ("residual_norm", "residual_final"):
            ms = jnp.mean(acc * acc, axis=-1, keepdims=True)
            y = acc * lax.rsqrt(ms + NORM_EPS) * gain_ref[...]
            if epilogue == "residual_final":
                o_ref[...] = y
                return
            hn_ref[...] = (y * (1.0 + sc_ref[...]) + sh_ref[...]).astype(hn_ref.dtype)
        o_ref[...] = acc.astype(o_ref.dtype)

    def part():
        return jnp.dot(a_ref[...], w_ref[...], preferred_element_type=F32)

    if nk == 1:
        finish(part())
        return
    acc_ref = rest[0] if rest else o_ref
    k = pl.program_id(2)

    @pl.when(k == 0)
    def _():
        acc_ref[...] = part()

    if nk > 2:
        @pl.when(jnp.logical_and(k > 0, k < nk - 1))
        def _():
            acc_ref[...] += part()

    @pl.when(k == nk - 1)
    def _():
        finish(acc_ref[...] + part())


def matmul(a, w, *, tm, tn, tk, out_dtype=F32, epilogue="none", res=None, mod=None,
           gate_blk=0, rows_per_batch=None, norm_gain=None, norm_mod=None, shift_blk=0,
           scale_blk=0):
    M, K = a.shape
    N = w.shape[1]
    nk = K // tk
    assert M % tm == 0 and N % tn == 0 and K % tk == 0
    in_specs = [pl.BlockSpec((tm, tk), lambda i, j, k: (i, k)),
                pl.BlockSpec((tk, tn), lambda i, j, k: (k, j))]
    args = [a, w]
    out_spec = pl.BlockSpec((tm, tn), lambda i, j, k: (i, j))
    out_specs, out_shape = out_spec, jax.ShapeDtypeStruct((M, N), out_dtype)
    if epilogue.startswith("residual"):
        assert rows_per_batch % tm == 0 and tn == D_MODEL == N
        batch = lambda i: i * tm // rows_per_batch
        chunk = lambda blk: pl.BlockSpec((None, 1, tn), lambda i, j, k: (batch(i), 0, blk))
        in_specs += [out_spec, chunk(gate_blk)]
        args += [res, mod]
        if epilogue != "residual":
            in_specs.append(pl.BlockSpec((1, tn), lambda i, j, k: (0, 0)))
            args.append(norm_gain.reshape(1, tn))
        if epilogue == "residual_norm":
            in_specs += [chunk(shift_blk), chunk(scale_blk)]
            args += [norm_mod, norm_mod]
            out_specs = [out_spec, out_spec]
            out_shape = [out_shape, jax.ShapeDtypeStruct((M, N), BF16)]
    scratch = [pltpu.VMEM((tm, tn), F32)] if nk > 1 and out_dtype != F32 else []
    return pl.pallas_call(
        functools.partial(_mm_kernel, nk=nk, epilogue=epilogue),
        grid=(M // tm, N // tn, nk),
        in_specs=in_specs,
        out_specs=out_specs,
        out_shape=out_shape,
        scratch_shapes=scratch,
        compiler_params=_cparams(("parallel", "parallel", "arbitrary")),
        name="mm_" + epilogue,
    )(*args)


def rope_tables(S):
    inv_freq = ROPE_THETA ** (-jnp.arange(0, HEAD_DIM, 2, dtype=F32) / HEAD_DIM)
    ang = jnp.arange(S, dtype=F32)[:, None] * inv_freq[None, :]
    cos, sin = jnp.cos(ang), jnp.sin(ang)
    return jnp.concatenate([cos, cos], axis=-1), jnp.concatenate([-sin, sin], axis=-1)


CAST, ROPE, ROPE_Q, SCALE_Q = 0, 1, 2, 3


def _proj_heads_kernel(a_ref, w_ref, c_ref, s_ref, *o_refs, kinds):
    if len(o_refs) > len(kinds):
        o_refs, wb_ref = o_refs[:-1], o_refs[-1]

        @pl.when(pl.program_id(0) == 0)
        def _():
            wb_ref[...] = w_ref[...].astype(BF16)
        w_ref = wb_ref
    acc = jnp.dot(a_ref[...], w_ref[...], preferred_element_type=F32)
    c = c_ref[...]
    s = s_ref[...]
    for o_ref, head_kinds in zip(o_refs, kinds):
        for h, kind in enumerate(head_kinds):
            sl = slice(h * LANE, (h + 1) * LANE)
            t = acc[:, sl]
            if kind in (ROPE, ROPE_Q):
                t = t * c + pltpu.roll(t, HEAD_DIM // 2, 1) * s
            if kind in (ROPE_Q, SCALE_Q):
                t = t * QK_SCALE
            o_ref[:, sl] = t.astype(o_ref.dtype)


def proj_heads(a, w, cos2, sin2, kinds, *, tm, col_blk=0):
    M, K = a.shape
    N = len(kinds[0]) * LANE
    nt = cos2.shape[0] // tm
    tab = pl.BlockSpec((tm, LANE), lambda i: (i % nt, 0))
    out_spec = pl.BlockSpec((tm, N), lambda i: (i, 0))
    cast = w.dtype != BF16
    return pl.pallas_call(
        functools.partial(_proj_heads_kernel, kinds=tuple(kinds)),
        grid=(M // tm,),
        in_specs=[pl.BlockSpec((tm, K), lambda i: (i, 0)),
                  pl.BlockSpec((K, N), lambda i: (0, col_blk), pipeline_mode=pl.Buffered(1)),
                  tab, tab],
        out_specs=[out_spec] * len(kinds),
        out_shape=[jax.ShapeDtypeStruct((M, N), BF16)] * len(kinds),
        scratch_shapes=[pltpu.VMEM((K, N), BF16)] if cast else [],
        compiler_params=_cparams(("arbitrary",)),
        name="proj_heads",
    )(a, w, cos2, sin2)


def _ones_column(n):
    return jnp.where(lax.broadcasted_iota(jnp.int32, (n, LANE), 1) == 0, 1.0, 0.0).astype(BF16)


def _softmax_pv(s, m, v, ones):
    p = jnp.exp2(s - m).astype(BF16)
    res = jnp.dot(p, jnp.concatenate([v, ones], axis=1), preferred_element_type=F32)
    return res[:, LANE:LANE + 1], res[:, :LANE]


def _banded_kernel(q_ref, k_ref, v_ref, o_ref, *lse_refs, R, hb, nwin, window, tq, qsub, nkeys):
    i = pl.program_id(3)
    L = k_ref.shape[0]
    rows = R * qsub
    rel = (lax.broadcasted_iota(jnp.int32, (rows, nkeys), 0) & (qsub - 1)) \
        - lax.broadcasted_iota(jnp.int32, (rows, nkeys), 1)
    ones = _ones_column(nkeys)
    units = [(h, j) for j in range(tq // qsub) for h in range(hb)]
    lane = lax.broadcasted_iota(jnp.int32, (rows, LANE), 1)
    lse_tile = jnp.zeros((rows, LANE), F32)

    def scores(h, j):
        q0 = i * tq + j * qsub
        kstart = pl.multiple_of(jnp.clip(q0 - nwin * LANE, 0, L - nkeys), LANE)
        qj = q_ref[j * qsub:(j + 1) * qsub, h * R * LANE:(h + 1) * R * LANE]
        if R > 1:
            q = jnp.concatenate([qj[:, r * LANE:(r + 1) * LANE] for r in range(R)], axis=0)
        else:
            q = qj
        s = lax.dot_general(q, k_ref[pl.ds(kstart, nkeys), h * LANE:(h + 1) * LANE], NT_DIMS,
                            preferred_element_type=F32)
        return q0, kstart, s

    ahead = 2
    pending = [scores(*u) for u in units[:ahead]]
    for n, (h, j) in enumerate(units):
        if n + ahead < len(units):
            pending.append(scores(*units[n + ahead]))
        q0, kstart, s = pending[n]
        v = v_ref[pl.ds(kstart, nkeys), h * LANE:(h + 1) * LANE]
        diff = rel + (q0 - kstart)
        valid = lax.bitcast_convert_type(diff, jnp.uint32) <= jnp.uint32(window)
        s = jnp.where(valid, s, NEG_BIG)
        m = jnp.max(s, axis=-1, keepdims=True)
        l, o = _softmax_pv(s, m, v, ones)
        o = o / l
        for r in range(R):
            col = (h * R + r) * LANE
            o_ref[j * qsub:(j + 1) * qsub, col:col + LANE] = o[r * qsub:(r + 1) * qsub]
        if lse_refs:
            lse_tile = jnp.where(lane == h, m + jnp.log(l) * LOG2E, lse_tile)
            if h == hb - 1:
                lse_refs[0][j * qsub:(j + 1) * qsub, :] = lse_tile


def banded_attention(q_arr, k_arr, v_arr, *, n_kv, R, nwin, window, want_lse, q_blk=0, k_blk=0,
                     v_blk=0):
    B, dil, L, _ = q_arr.shape
    tq = min(1024 if R > 1 else 512, L)
    qsub = LANE
    nkeys = min(qsub + nwin * LANE, L)
    hb = 4 if R == 1 else 1
    assert L % tq == 0 and n_kv % hb == 0
    assert q_blk % (hb * R) == 0 and k_blk % hb == 0 and v_blk % hb == 0
    width = n_kv * R * LANE
    out_shape = [jax.ShapeDtypeStruct((B, dil, L, width), F32)]
    out_specs = [pl.BlockSpec((None, None, tq, hb * R * LANE), lambda b, r, h, i: (b, r, i, h))]
    if want_lse:
        out_shape.append(jax.ShapeDtypeStruct((B, dil, L, n_kv // hb * LANE), F32))
        out_specs.append(pl.BlockSpec((None, None, tq, LANE), lambda b, r, h, i: (b, r, i, h)))
    kv_spec = lambda blk: pl.BlockSpec((None, None, L, hb * LANE),
                                       lambda b, r, h, i: (b, r, 0, blk // hb + h))
    return pl.pallas_call(
        functools.partial(_banded_kernel, R=R, hb=hb, nwin=nwin, window=window, tq=tq, qsub=qsub,
                          nkeys=nkeys),
        grid=(B, dil, n_kv // hb, L // tq),
        in_specs=[pl.BlockSpec((None, None, tq, hb * R * LANE),
                               lambda b, r, h, i: (b, r, i, q_blk // (hb * R) + h)),
                  kv_spec(k_blk), kv_spec(v_blk)],
        out_specs=out_specs,
        out_shape=out_shape,
        compiler_params=_cparams(("parallel", "parallel", "parallel", "arbitrary")),
        name="banded_attn",
    )(q_arr, k_arr, v_arr)


def _compress_kernel(x_ref, pe_ref, w1_ref, w2_ref, o_ref):
    n = x_ref.shape[0] // CMP_STRIDE
    a = jnp.zeros((n, CMP_HIDDEN), F32)
    b = jnp.zeros((n, CMP_HIDDEN), F32)
    for j in range(CMP_STRIDE):
        xj = x_ref[pl.ds(j, n, stride=CMP_STRIDE), :]
        top = (xj + pe_ref[j:j + 1, :]).astype(BF16)
        bot = (xj + pe_ref[CMP_STRIDE + j:CMP_STRIDE + j + 1, :]).astype(BF16)
        a = a + jnp.dot(top, w1_ref[j * HEAD_DIM:(j + 1) * HEAD_DIM, :],
                        preferred_element_type=F32)
        b = b + jnp.dot(bot, w1_ref[(CMP_STRIDE + j) * HEAD_DIM:(CMP_STRIDE + j + 1) * HEAD_DIM, :],
                        preferred_element_type=F32)
    hid = a + pltpu.roll(b, n - 1, 0)
    o_ref[...] = jnp.dot(jax.nn.gelu(hid).astype(BF16), w2_ref[...], preferred_element_type=F32)


def compress(kv, pe, w1, w2):
    B, S, _ = kv.shape
    G = NSA_KV_GROUPS
    NP = S // CMP_STRIDE
    return pl.pallas_call(
        _compress_kernel,
        grid=(2, B, G),
        in_specs=[pl.BlockSpec((None, S, HEAD_DIM), lambda t, b, g: (b, 0, t * G + g)),
                  pl.BlockSpec((None, CMP_BLOCK, HEAD_DIM), lambda t, b, g: (t, 0, 0)),
                  pl.BlockSpec((None, CMP_BLOCK * HEAD_DIM, CMP_HIDDEN), lambda t, b, g: (t, 0, 0)),
                  pl.BlockSpec((None, CMP_HIDDEN, HEAD_DIM), lambda t, b, g: (t, 0, 0))],
        out_specs=pl.BlockSpec((None, None, None, NP, HEAD_DIM), lambda t, b, g: (t, b, g, 0, 0)),
        out_shape=jax.ShapeDtypeStruct((2, B, G, NP, HEAD_DIM), F32),
        compiler_params=_cparams(("parallel", "parallel", "parallel")),
        name="nsa_compress",
    )(kv, pe, w1, w2)


def _cmp_attn_kernel(q_ref, kc_ref, vc_ref, ov_ref, o_ref, imp_ref, *, tq, qsub):
    i = pl.program_id(2)
    R = NSA_REP
    nsub = tq // qsub
    rows = R * qsub
    n_var = kc_ref.shape[0] // LANE

    def run(nk):
        kc = kc_ref[0:nk, :].astype(BF16)
        vc = vc_ref[0:nk, :].astype(BF16)
        ov = ov_ref[:, 0:nk]
        tloc = lax.broadcasted_iota(jnp.int32, (rows, nk), 0) & (qsub - 1)
        cmp_end = lax.broadcasted_iota(jnp.int32, (rows, nk), 1) * CMP_STRIDE + (CMP_BLOCK - 1)

        def scores(u):
            qu = q_ref[u * qsub:(u + 1) * qsub, :]
            q = jnp.concatenate([qu[:, r * LANE:(r + 1) * LANE] for r in range(R)], axis=0)
            return lax.dot_general(q, kc, NT_DIMS, preferred_element_type=F32)

        ahead = 2
        pending = [scores(u) for u in range(min(ahead, nsub))]
        for u in range(nsub):
            if u + ahead < nsub:
                pending.append(scores(u + ahead))
            valid = cmp_end <= tloc + (i * tq + u * qsub)
            s = jnp.where(valid, pending[u], NEG_BIG)
            m = jnp.max(s, axis=-1, keepdims=True)
            p = jnp.where(valid, jnp.exp2(s - m), 0.0)
            l = jnp.sum(p, axis=-1, keepdims=True)
            p = p / jnp.where(l > 0, l, 1.0)
            o = jnp.dot(p.astype(BF16), vc, preferred_element_type=F32)
            for r in range(R):
                o_ref[u * qsub:(u + 1) * qsub, r * LANE:(r + 1) * LANE] = \
                    o[r * qsub:(r + 1) * qsub]
            psum = p[0:qsub]
            for r in range(1, R):
                psum = psum + p[r * qsub:(r + 1) * qsub]
            p_hi = psum.astype(BF16)
            p_lo = (psum - p_hi.astype(F32)).astype(BF16)
            imp_ref[:, u * qsub:(u + 1) * qsub] = (
                lax.dot_general(ov, p_hi, NT_DIMS, preferred_element_type=F32)
                + lax.dot_general(ov, p_lo, NT_DIMS, preferred_element_type=F32))

    need = ((i + 1) * tq - CMP_BLOCK) // CMP_STRIDE + 1
    var = jnp.clip((need - 1) // LANE, 0, n_var - 1)
    for v in range(n_var):
        pl.when(var == v)(functools.partial(run, (v + 1) * LANE))


def cmp_attention(q, kvc, overlap_t, tq=1024, qsub=128):
    B, S, _ = q.shape
    G = NSA_KV_GROUPS
    NP = kvc.shape[3]
    n_sel = overlap_t.shape[0]
    qw = NSA_REP * LANE
    return pl.pallas_call(
        functools.partial(_cmp_attn_kernel, tq=tq, qsub=qsub),
        grid=(B, G, S // tq),
        in_specs=[pl.BlockSpec((None, tq, qw), lambda b, g, i: (b, i, g)),
                  pl.BlockSpec((None, None, None, NP, LANE), lambda b, g, i: (0, b, g, 0, 0)),
                  pl.BlockSpec((None, None, None, NP, LANE), lambda b, g, i: (1, b, g, 0, 0)),
                  pl.BlockSpec((n_sel, NP), lambda b, g, i: (0, 0))],
        out_specs=[pl.BlockSpec((None, tq, qw), lambda b, g, i: (b, i, g)),
                   pl.BlockSpec((None, None, n_sel, tq), lambda b, g, i: (b, g, 0, i))],
        out_shape=[jax.ShapeDtypeStruct((B, S, NSA_HEADS * LANE), F32),
                   jax.ShapeDtypeStruct((B, G, n_sel, S), F32)],
        compiler_params=_cparams(("parallel", "parallel", "parallel")),
        name="nsa_cmp_attn",
    )(q, kvc, kvc, overlap_t)


def _topk_kernel(imp_ref, o_ref, *, tq):
    i = pl.program_id(2)
    imp = imp_ref[...]
    n_sel = imp.shape[0]
    blk = lax.broadcasted_iota(jnp.int32, imp.shape, 0)
    t = i * tq + lax.broadcasted_iota(jnp.int32, imp.shape, 1)
    cur = t // SEL_BLOCK
    avail = blk <= cur
    forced = jnp.where(blk == 0, 1.0, jnp.where(blk == cur, 1.0, jnp.where(blk == cur - 1, 1.0, 0.0)))
    score = jnp.where(avail, imp + FORCED_BONUS * forced, -jnp.inf)
    picked = jnp.zeros(imp.shape, F32)
    for _ in range(min(SEL_TOPK, n_sel)):
        mx = jnp.max(score, axis=0, keepdims=True)
        first = jnp.min(jnp.where(score == mx, blk, n_sel), axis=0, keepdims=True)
        hit = blk == first
        picked = jnp.where(hit, 1.0, picked)
        score = jnp.where(hit, -jnp.inf, score)
    feat = jnp.where(avail, jnp.where(picked > 0.0, 0.0, SEL_OFF), SEL_OFF)
    o_ref[...] = feat.T.astype(o_ref.dtype)


def topk_select(imp_t, tq=1024):
    B, G, n_sel, S = imp_t.shape
    tq = min(tq, S)
    return pl.pallas_call(
        functools.partial(_topk_kernel, tq=tq),
        grid=(B, G, S // tq),
        in_specs=[pl.BlockSpec((None, None, n_sel, tq), lambda b, g, i: (b, g, 0, i))],
        out_specs=pl.BlockSpec((None, None, tq, n_sel), lambda b, g, i: (b, g, i, 0)),
        out_shape=jax.ShapeDtypeStruct((B, G, S, n_sel), BF16),
        compiler_params=_cparams(("parallel", "parallel", "parallel")),
        name="nsa_topk",
    )(imp_t)


def _sel_attn_kernel(q_ref, mf_ref, k_ref, e_ref, vt_ref, o_ref, *, tq, qsub, tkv):
    i = pl.program_id(2)
    R = NSA_REP
    nsub = tq // qsub
    rows = R * qsub
    q0 = i * tq
    ntile = (q0 + tq - 1) // tkv + 1
    rel = (lax.broadcasted_iota(jnp.int32, (tkv, rows), 1) & (qsub - 1)) \
        - lax.broadcasted_iota(jnp.int32, (tkv, rows), 0)

    qa = []
    for u in range(nsub):
        qu = q_ref[u * qsub:(u + 1) * qsub, :]
        mf = mf_ref[u * qsub:(u + 1) * qsub, :]
        qa.append(jnp.concatenate(
            [jnp.concatenate([qu[:, r * LANE:(r + 1) * LANE], mf], axis=1) for r in range(R)],
            axis=0))

    def body(j, carry, masked):
        ks = pl.multiple_of(j * tkv, tkv)
        ka = jnp.concatenate([k_ref[pl.ds(ks, tkv), :], e_ref[pl.ds(ks, tkv), :]], axis=1)
        vt = vt_ref[:, pl.ds(ks, tkv)]
        out = []
        nkeys = lambda u: (u + 1) * qsub if masked else tkv
        scores = lambda u: lax.dot_general(ka[:nkeys(u)], qa[u], NT_DIMS,
                                           preferred_element_type=F32)
        ahead = 2
        sts = [scores(u) for u in range(min(ahead, nsub))]
        for u in range(nsub):
            m, l, acc = carry[u]
            if u + ahead < nsub:
                sts.append(scores(u + ahead))
            st = sts[u]
            if masked:
                st = jnp.where(rel[:nkeys(u)] >= -u * qsub, st, NEG_BIG)
            m_new = jnp.maximum(m, jnp.max(st, axis=0, keepdims=True))
            a = jnp.exp2(m - m_new)
            res = jnp.dot(vt[:, :nkeys(u)], jnp.exp2(st - m_new).astype(BF16),
                          preferred_element_type=F32)
            out.append((m_new, a * l + res[LANE:LANE + 1, :], a * acc + res[:LANE, :]))
        return tuple(out)

    init = tuple((jnp.full((1, rows), NEG_BIG, F32), jnp.zeros((1, rows), F32),
                  jnp.zeros((LANE, rows), F32)) for _ in range(nsub))
    carry = lax.fori_loop(0, ntile - 1, lambda j, c: body(j, c, False), init)
    final = body(ntile - 1, carry, True)
    for u in range(nsub):
        _, l, acc = final[u]
        ot = acc / l
        for r in range(R):
            o_ref[u * qsub:(u + 1) * qsub, r * LANE:(r + 1) * LANE] = \
                ot[:, r * qsub:(r + 1) * qsub].T


def sel_attention(q_rot, mfeat, kvb, onehot, *, k_blk, v_blk, tq=1024, qsub=128, tkv=1024):
    B, S, _ = q_rot.shape
    G = NSA_KV_GROUPS
    n_sel = onehot.shape[1]
    qw = NSA_REP * LANE
    tkv = min(tkv, S)
    tq = min(tq, S)
    assert tkv == tq and S % tkv == 0
    v = kvb[:, :, v_blk * LANE:(v_blk + G) * LANE].reshape(B, S, G, LANE).transpose(0, 2, 3, 1)
    pad = jnp.zeros((B, G, 16, S), BF16).at[:, :, 0, :].set(1.0)
    vt = jnp.concatenate([v, pad], axis=2)
    return pl.pallas_call(
        functools.partial(_sel_attn_kernel, tq=tq, qsub=qsub, tkv=tkv),
        grid=(B, G, S // tq),
        in_specs=[pl.BlockSpec((None, tq, qw), lambda b, g, i: (b, i, g)),
                  pl.BlockSpec((None, None, tq, n_sel), lambda b, g, i: (b, g, i, 0)),
                  pl.BlockSpec((None, S, LANE), lambda b, g, i: (b, 0, k_blk + g)),
                  pl.BlockSpec((S, n_sel), lambda b, g, i: (0, 0)),
                  pl.BlockSpec((None, None, LANE + 16, S), lambda b, g, i: (b, g, 0, 0))],
        out_specs=pl.BlockSpec((None, tq, qw), lambda b, g, i: (b, i, g)),
        out_shape=jax.ShapeDtypeStruct((B, S, NSA_HEADS * LANE), F32),
        compiler_params=_cparams(("parallel", "parallel", "arbitrary")),
        name="nsa_sel_attn",
    )(q_rot, mfeat, kvb, onehot, vt)


def _nsa_combine_kernel(g_ref, oc_ref, os_ref, ow_ref, o_ref):
    gates = jax.nn.sigmoid(g_ref[...])
    for h in range(NSA_HEADS):
        sl = slice(h * LANE, (h + 1) * LANE)
        acc = gates[:, 3 * h:3 * h + 1] * oc_ref[:, sl]
        acc = acc + gates[:, 3 * h + 1:3 * h + 2] * os_ref[:, sl]
        acc = acc + gates[:, 3 * h + 2:3 * h + 3] * ow_ref[:, sl]
        o_ref[:, sl] = acc.astype(o_ref.dtype)


def nsa_combine(gate_logits, o_cmp, o_sel, o_win, ts=512):
    B, S, W = o_cmp.shape
    spec = pl.BlockSpec((None, ts, W), lambda b, i: (b, i, 0))
    return pl.pallas_call(
        _nsa_combine_kernel,
        grid=(B, S // ts),
        in_specs=[pl.BlockSpec((None, ts, LANE), lambda b, i: (b, i, 0)), spec, spec, spec],
        out_specs=spec,
        out_shape=jax.ShapeDtypeStruct((B, S, W), BF16),
        compiler_params=_cparams(("parallel", "parallel")),
        name="nsa_combine",
    )(gate_logits, o_cmp, o_sel, o_win)


def _dil_combine_kernel(*refs, dils, hb):
    n = len(dils)
    o_refs, lse_refs, out_ref = refs[:n], refs[n:2 * n], refs[2 * n]
    nat_o, nat_l = refs[2 * n + 1:3 * n + 1], refs[3 * n + 1:]
    ts = out_ref.shape[0]
    for h in range(out_ref.shape[1] // LANE):
        sl = slice(h * LANE, (h + 1) * LANE)
        lane = (h // hb) * LANE + h % hb
        for g, dil in enumerate(dils):
            for r in range(dil):
                rows = pl.ds(r, ts // dil, stride=dil) if dil > 1 else slice(None)
                nat_o[g][rows, :] = o_refs[g][r, :, sl]
                nat_l[g][rows, :] = jnp.broadcast_to(lse_refs[g][r, :, lane:lane + 1],
                                                     (ts // dil, LANE))
        lses = [r[...] for r in nat_l]
        mx = functools.reduce(jnp.maximum, lses)
        ws = [jnp.exp2(l - mx) for l in lses]
        den = functools.reduce(lambda a, b: a + b, ws)
        num = functools.reduce(lambda a, b: a + b, [w * r[...] for w, r in zip(ws, nat_o)])
        out_ref[:, sl] = (num / den).astype(out_ref.dtype)


def dil_combine(outs, lses, ts=256):
    dils = tuple(o.shape[1] for o in outs)
    B, W = outs[0].shape[0], outs[0].shape[-1]
    S = outs[0].shape[1] * outs[0].shape[2]
    WL = lses[0].shape[-1]
    hb = (W // LANE) // (WL // LANE)
    spec = lambda w: [pl.BlockSpec((None, d, ts // d, w), lambda b, i: (b, 0, i, 0)) for d in dils]
    return pl.pallas_call(
        functools.partial(_dil_combine_kernel, dils=dils, hb=hb),
        grid=(B, S // ts),
        in_specs=spec(W) + spec(WL),
        out_specs=pl.BlockSpec((None, ts, W), lambda b, i: (b, i, 0)),
        out_shape=jax.ShapeDtypeStruct((B, S, W), BF16),
        scratch_shapes=[pltpu.VMEM((ts, LANE), F32)] * (2 * len(dils)),
        compiler_params=_cparams(("parallel", "parallel")),
        name="dil_combine",
    )(*outs, *lses)


def _softplus(x):
    return jnp.maximum(x, 0.0) + jnp.log1p(jnp.exp(-jnp.abs(x)))


def _gate_band_starts():
    ntile = D_RNN // LANE
    starts = []
    for j in range(ntile):
        n_lo = (j * LANE) // RNN_BLOCK_DIM
        n_hi = (j * LANE + LANE - 1) // RNN_BLOCK_DIM
        lo = (n_lo * RNN_BLOCK_DIM) // LANE
        hi = -(-((n_hi + 1) * RNN_BLOCK_DIM) // LANE)
        assert hi - lo <= 4
        starts.append(min(lo, ntile - 4))
    return starts


def _rglru_kernel(y_ref, xr_ref, cw_ref, cb_ref, wg_ref, bg_ref, lam_ref, o_ref,
                  h_ref, tail_ref, *, ts, starts):
    i = pl.program_id(1)

    @pl.when(i == 0)
    def _():
        h_ref[...] = jnp.zeros_like(h_ref)
        tail_ref[...] = jnp.zeros_like(tail_ref)

    xr = xr_ref[...]
    ext = jnp.concatenate([tail_ref[...], xr], axis=0)
    x = cb_ref[...] + xr * cw_ref[CONV_WIDTH - 1:CONV_WIDTH, :]
    for d in range(1, CONV_WIDTH):
        shifted = pltpu.roll(ext, d, 0)[8:8 + ts]
        x = x + shifted * cw_ref[CONV_WIDTH - 1 - d:CONV_WIDTH - d, :]
    tail_ref[...] = xr[ts - 8:ts]

    xb = x.astype(BF16)
    gl = []
    for g in range(2):
        tiles = [jnp.dot(xb[:, a * LANE:(a + 4) * LANE], wg_ref[g, j],
                         preferred_element_type=F32) for j, a in enumerate(starts)]
        gl.append(jnp.concatenate(tiles, axis=1) + bg_ref[g:g + 1, :])
    r = jax.nn.sigmoid(gl[0])
    ig = jax.nn.sigmoid(gl[1])
    log_a = (-LRU_C) * r * _softplus(-lam_ref[...])
    a = jnp.exp(log_a)
    z = -jnp.tanh(log_a) * (a * a + 1.0)
    b = jnp.where(z > 0.0, z * lax.rsqrt(z), 0.0) * (ig * x)

    row = lax.broadcasted_iota(jnp.int32, a.shape, 0)
    d = 1
    while d < ts:
        if d < 8:
            keep = row >= d
            b = b + a * jnp.where(keep, pltpu.roll(b, d, 0), 0.0)
            a = a * jnp.where(keep, pltpu.roll(a, d, 0), 1.0)
        else:
            b = jnp.concatenate([b[:d], b[d:] + a[d:] * b[:ts - d]], axis=0)
            a = jnp.concatenate([a[:d], a[d:] * a[:ts - d]], axis=0)
        d *= 2
    h = a * h_ref[0:1, :] + b
    h_ref[0:1, :] = h[ts - 1:ts, :]
    o_ref[...] = (h * jax.nn.gelu(y_ref[...])).astype(o_ref.dtype)


def rglru_scan(proj, conv_w, conv_b, wband, b_gate, lam, ts=128):
    B, S, _ = proj.shape
    C = D_RNN
    starts = _gate_band_starts()
    vec = lambda n: pl.BlockSpec((n, C), lambda b, i: (0, 0))
    return pl.pallas_call(
        functools.partial(_rglru_kernel, ts=ts, starts=starts),
        grid=(B, S // ts),
        in_specs=[pl.BlockSpec((None, ts, C), lambda b, i: (b, i, 0)),
                  pl.BlockSpec((None, ts, C), lambda b, i: (b, i, 1)),
                  vec(CONV_WIDTH), vec(1),
                  pl.BlockSpec(wband.shape, lambda b, i: (0, 0, 0, 0)),
                  vec(2), vec(1)],
        out_specs=pl.BlockSpec((None, ts, C), lambda b, i: (b, i, 0)),
        out_shape=jax.ShapeDtypeStruct((B, S, C), BF16),
        scratch_shapes=[pltpu.VMEM((8, C), F32), pltpu.VMEM((8, C), F32)],
        compiler_params=_cparams(("arbitrary", "arbitrary")),
        name="rglru_scan",
    )(proj, proj, conv_w, conv_b.reshape(1, C), wband, b_gate, lam.reshape(1, C))


def _gate_band_weights(w_gate):
    starts = _gate_band_starts()
    dense = jnp.stack([jax.scipy.linalg.block_diag(*[w_gate[g, n] for n in range(RNN_BLOCKS)])
                       for g in range(2)])
    tiles = [dense[:, a * LANE:(a + 4) * LANE, j * LANE:(j + 1) * LANE]
             for j, a in enumerate(starts)]
    return jnp.stack(tiles, axis=1).astype(BF16)


def _row_tile(T):
    return 512 if T % 512 == 0 else T


def _proj_tile(S):
    return 1024 if S % 1024 == 0 else _row_tile(S)


RESIDUAL = dict(epilogue="residual")


def out_proj(a, w, x, mod, gate_blk, tk, tail, tm=None):
    B, S, D = x.shape
    T = B * S
    out = matmul(a.reshape(T, -1), w.astype(BF16), tm=tm or _row_tile(T), tn=D, tk=tk,
                 res=x.reshape(T, D), mod=mod, gate_blk=gate_blk, rows_per_batch=S, **tail)
    if isinstance(out, (list, tuple)):
        return tuple(o.reshape(B, S, D) for o in out)
    return out.reshape(B, S, D)


def nsa_mixer(hn, x, mod, gate_blk, w_in, cmp_pe, cmp_w1, cmp_w2, w_out, cos2, sin2,
              tail=RESIDUAL):
    B, S, D = x.shape
    T = B * S
    G = NSA_KV_GROUPS
    hq = NSA_HEADS * HEAD_DIM
    kvw = NSA_KV_WIDTH
    main = hq + 6 * kvw
    tm = _proj_tile(S)
    hn2 = hn.reshape(T, D)
    w_gate = jnp.pad(w_in[:, main:], ((0, 0), (0, LANE - 3 * NSA_HEADS))).astype(BF16)
    gate_logits = matmul(hn2, w_gate, tm=tm, tn=LANE, tk=D).reshape(B, S, LANE)

    q_cmp, q_rot = proj_heads(hn2, w_in[:, :hq].astype(BF16), cos2, sin2,
                              [(SCALE_Q,) * NSA_HEADS, (ROPE_Q,) * NSA_HEADS], tm=tm)
    g_rope, g_cast = (ROPE,) * G, (CAST,) * G
    (kvb,) = proj_heads(hn2, w_in[:, hq + 2 * kvw:main].astype(BF16), cos2, sin2,
                        [g_rope + g_cast + g_rope + g_cast], tm=tm)
    q_rot = q_rot.reshape(B, 1, S, hq)
    kvb = kvb.reshape(B, 1, S, 4 * kvw)

    npiece = S // CMP_STRIDE
    kv_cmp = matmul(hn2, w_in[:, hq:hq + 2 * kvw].astype(BF16), tm=tm, tn=2 * kvw, tk=D)
    kvc = compress(kv_cmp.reshape(B, S, 2 * kvw), cmp_pe, cmp_w1.astype(BF16),
                   cmp_w2.astype(BF16))

    n_sb = S // SEL_BLOCK
    cmp_start = np.arange(npiece) * CMP_STRIDE
    sel_start = np.arange(n_sb) * SEL_BLOCK
    overlap = ((cmp_start[:, None] < sel_start[None, :] + SEL_BLOCK)
               & (cmp_start[:, None] + CMP_BLOCK > sel_start[None, :]))
    overlap[npiece - 1] = False
    overlap_t = jnp.asarray(overlap.T, BF16)
    onehot = jnp.asarray(np.arange(S)[:, None] // SEL_BLOCK == np.arange(n_sb)[None, :], BF16)

    o_cmp, imp_t = cmp_attention(q_cmp.reshape(B, S, hq), kvc, overlap_t)
    mfeat = topk_select(imp_t)
    kb = kvw // LANE
    o_sel = sel_attention(q_rot.reshape(B, S, hq), mfeat, kvb.reshape(B, S, 4 * kvw),
                          onehot, k_blk=0, v_blk=kb)
    (o_win,) = banded_attention(q_rot, kvb, kvb, n_kv=G, R=NSA_REP, k_blk=2 * kb, v_blk=3 * kb,
                                nwin=(NSA_WINDOW - 1 + LANE - 1) // LANE,
                                window=NSA_WINDOW - 1, want_lse=False)
    o = nsa_combine(gate_logits, o_cmp, o_sel, o_win.reshape(B, S, hq))
    return out_proj(o, w_out, x, mod, gate_blk, hq, tail)


def dilated_mixer(hn, x, mod, gate_blk, w_in, w_out, cos2, sin2, tail=RESIDUAL):
    B, S, D = x.shape
    T = B * S
    H = DIL_HEADS
    hw = H * HEAD_DIM
    tm = _proj_tile(S)
    outs, lses = [], []
    for g, (window, dil) in enumerate(DIL_PATTERNS):
        w = window // dil
        L = S // dil
        regroup = lambda t: t.reshape(t.shape[:-2] + (L, dil, t.shape[-1])).swapaxes(-3, -2)
        hn_g = regroup(hn).reshape(T, D)
        cos_g, sin_g = regroup(cos2).reshape(S, LANE), regroup(sin2).reshape(S, LANE)
        q, k, v = [proj_heads(hn_g, w_in, cos_g, sin_g, [(kind,) * H], tm=tm,
                              col_blk=3 * g + c)[0].reshape(B, dil, L, hw)
                   for c, kind in enumerate((ROPE_Q, ROPE, CAST))]
        o, lse = banded_attention(q, k, v, n_kv=H, R=1, nwin=(w + LANE - 1) // LANE, window=w,
                                  want_lse=True)
        outs.append(o)
        lses.append(lse)
    o = dil_combine(outs, lses)
    return out_proj(o, w_out, x, mod, gate_blk, hw, tail)


def rglru_mixer(hn, x, mod, gate_blk, w_in, conv_w, conv_b, w_gate, b_gate, lam, w_out,
                tail=RESIDUAL):
    B, S, D = x.shape
    T = B * S
    tm = _row_tile(T)
    proj = matmul(hn.reshape(T, D), w_in.astype(BF16), tm=tm, tn=D_RNN, tk=D)
    hy = rglru_scan(proj.reshape(B, S, 2 * D_RNN), conv_w, conv_b, _gate_band_weights(w_gate),
                    b_gate, lam)
    return out_proj(hy, w_out, x, mod, gate_blk, D_RNN, tail)


def mlp(hn, x, mod, gate_blk, w1, w2, tail=RESIDUAL):
    B, S, D = x.shape
    T = B * S
    h = matmul(hn.reshape(T, D), w1.astype(BF16), tm=2 * _row_tile(T) if T % 1024 == 0 else T,
               tn=2048, tk=D, out_dtype=BF16, epilogue="relu2")
    return out_proj(h, w2, x, mod, gate_blk, 1024, tail, tm=1024 if T % 1024 == 0 else None)


def kernel(x, c, l0_w_ada, l0_b_ada, l0_norm1, l0_w_in, l0_cmp_pe, l0_cmp_w1, l0_cmp_w2, l0_w_out, l0_norm2, l0_w_ff1, l0_w_ff2, l1_w_ada, l1_b_ada, l1_norm1, l1_w_in, l1_w_out, l1_norm2, l1_w_ff1, l1_w_ff2, l2_w_ada, l2_b_ada, l2_norm1, l2_w_in, l2_conv_w, l2_conv_b, l2_w_gate, l2_b_gate, l2_lambda, l2_w_out, l2_norm2, l2_w_ff1, l2_w_ff2, l3_w_ada, l3_b_ada, l3_norm1, l3_w_in, l3_cmp_pe, l3_cmp_w1, l3_cmp_w2, l3_w_out, l3_norm2, l3_w_ff1, l3_w_ff2, norm_f):
    B, S, D = x.shape
    cos2, sin2 = rope_tables(S)
    layers = (
        (l0_w_ada, l0_b_ada, l0_norm1, l0_norm2, l0_w_ff1, l0_w_ff2,
         (l0_w_in, l0_cmp_pe, l0_cmp_w1, l0_cmp_w2, l0_w_out)),
        (l1_w_ada, l1_b_ada, l1_norm1, l1_norm2, l1_w_ff1, l1_w_ff2, (l1_w_in, l1_w_out)),
        (l2_w_ada, l2_b_ada, l2_norm1, l2_norm2, l2_w_ff1, l2_w_ff2,
         (l2_w_in, l2_conv_w, l2_conv_b, l2_w_gate, l2_b_gate, l2_lambda, l2_w_out)),
        (l3_w_ada, l3_b_ada, l3_norm1, l3_norm2, l3_w_ff1, l3_w_ff2,
         (l3_w_in, l3_cmp_pe, l3_cmp_w1, l3_cmp_w2, l3_w_out)),
    )
    mods = [adaln(c, l[0], l[1]).reshape(B, 1, 6 * D) for l in layers]
    hn = modulate(x, layers[0][2], mods[0], 0, 1)
    for li in range(DEPTH):
        _, _, _, n2, ff1, ff2, mix = layers[li]
        mod = mods[li]
        tail = dict(epilogue="residual_norm", norm_gain=n2, norm_mod=mod, shift_blk=3, scale_blk=4)
        kind = li % N_MIXERS
        if kind == 0:
            x, hn = nsa_mixer(hn, x, mod, 2, *mix, cos2, sin2, tail=tail)
        elif kind == 1:
            x, hn = dilated_mixer(hn, x, mod, 2, *mix, cos2, sin2, tail=tail)
        else:
            x, hn = rglru_mixer(hn, x, mod, 2, *mix, tail=tail)
        if li + 1 < DEPTH:
            tail = dict(epilogue="residual_norm", norm_gain=layers[li + 1][2],
                        norm_mod=mods[li + 1], shift_blk=0, scale_blk=1)
            x, hn = mlp(hn, x, mod, 5, ff1, ff2, tail=tail)
        else:
            return mlp(hn, x, mod, 5, ff1, ff2,
                       tail=dict(epilogue="residual_final", norm_gain=norm_f))
```

```python
import functools
import math

import jax
import jax.numpy as jnp
import numpy as np
from jax import lax
from jax.experimental import pallas as pl
from jax.experimental.pallas import tpu as pltpu

F32 = jnp.float32
BF16 = jnp.bfloat16

D_MODEL = 2048
DEPTH = 4
N_MIXERS = 3
HEAD_DIM = 128
ROPE_THETA = 10000.0
NORM_EPS = 1e-6

NSA_HEADS = D_MODEL // HEAD_DIM
NSA_KV_GROUPS = 4
NSA_REP = NSA_HEADS // NSA_KV_GROUPS
NSA_KV_WIDTH = NSA_KV_GROUPS * HEAD_DIM
CMP_BLOCK = 32
CMP_STRIDE = 16
CMP_HIDDEN = 4 * HEAD_DIM
SEL_BLOCK = 64
SEL_TOPK = 16
NSA_WINDOW = 512
FORCED_BONUS = 1e9

DIL_HEADS = D_MODEL // HEAD_DIM
DIL_PATTERNS = ((128, 1), (512, 4), (2048, 16))

D_RNN = 2688
RNN_BLOCKS = 16
RNN_BLOCK_DIM = D_RNN // RNN_BLOCKS
CONV_WIDTH = 4
LRU_C = 8.0

LANE = 128
LOG2E = math.log2(math.e)
QK_SCALE = LOG2E / math.sqrt(HEAD_DIM)
NEG_BIG = -1e30
SEL_OFF = -float(2 ** 30)
VMEM_LIMIT = 56 * 1024 * 1024

NT_DIMS = (((1,), (1,)), ((), ()))


def _cparams(sem):
    return pltpu.CompilerParams(dimension_semantics=sem, vmem_limit_bytes=VMEM_LIMIT)


def _adaln_kernel(c_ref, w_ref, b_ref, o_ref):
    w = w_ref[...]
    for b in range(c_ref.shape[0]):
        c = c_ref[b]
        cond = c * jax.nn.sigmoid(c)
        o_ref[b:b + 1, :] = jnp.sum(w * cond, axis=0, keepdims=True) + b_ref[...]


def adaln(c, w_ada, b_ada, tn=1024):
    B, D = c.shape
    N = w_ada.shape[1]
    return pl.pallas_call(
        _adaln_kernel,
        grid=(N // tn,),
        in_specs=[pl.BlockSpec((B, D, 1), lambda j: (0, 0, 0)),
                  pl.BlockSpec((D, tn), lambda j: (0, j)),
                  pl.BlockSpec((1, tn), lambda j: (0, j))],
        out_specs=pl.BlockSpec((B, tn), lambda j: (0, j)),
        out_shape=jax.ShapeDtypeStruct((B, N), F32),
        compiler_params=_cparams(("parallel",)),
        name="adaln",
    )(c.reshape(B, D, 1), w_ada, b_ada.reshape(1, N))


def _modulate_kernel(x_ref, gain_ref, sh_ref, sc_ref, o_ref):
    x = x_ref[...]
    ms = jnp.mean(x * x, axis=-1, keepdims=True)
    y = x * lax.rsqrt(ms + NORM_EPS) * gain_ref[...]
    o_ref[...] = (y * (1.0 + sc_ref[...]) + sh_ref[...]).astype(o_ref.dtype)


def modulate(x, gain, mod, shift_blk, scale_blk, ts=512):
    B, S, D = x.shape
    return pl.pallas_call(
        _modulate_kernel,
        grid=(B, S // ts),
        in_specs=[pl.BlockSpec((None, ts, D), lambda b, i: (b, i, 0)),
                  pl.BlockSpec((1, D), lambda b, i: (0, 0)),
                  pl.BlockSpec((None, 1, D), lambda b, i: (b, 0, shift_blk)),
                  pl.BlockSpec((None, 1, D), lambda b, i: (b, 0, scale_blk))],
        out_specs=pl.BlockSpec((None, ts, D), lambda b, i: (b, i, 0)),
        out_shape=jax.ShapeDtypeStruct((B, S, D), BF16),
        compiler_params=_cparams(("parallel", "parallel")),
        name="modulate",
    )(x, gain.reshape(1, D), mod, mod)


def _mm_kernel(*refs, nk, epilogue):
    refs = list(refs)
    a_ref, w_ref = refs[:2]
    del refs[:2]
    if epilogue.startswith("residual"):
        res_ref, gate_ref = refs[:2]
        del refs[:2]
    if epilogue == "residual_norm":
        gain_ref, sh_ref, sc_ref = refs[:3]
        del refs[:3]
    elif epilogue == "residual_final":
        gain_ref = refs.pop(0)
    o_ref = refs.pop(0)
    hn_ref = refs.pop(0) if epilogue == "residual_norm" else None
    rest = refs

    def finish(acc):
        if epilogue == "relu2":
            r = jnp.maximum(acc, 0.0)
            acc = r * r
        elif epilogue.startswith("residual"):
            acc = res_ref[...] + gate_ref[...] * acc
        if epilogue in ("residual_norm", "residual_final"):
            ms = jnp.mean(acc * acc, axis=-1, keepdims=True)
            y = acc * lax.rsqrt(ms + NORM_EPS) * gain_ref[...]
            if epilogue == "residual_final":
                o_ref[...] = y
                return
            hn_ref[...] = (y * (1.0 + sc_ref[...]) + sh_ref[...]).astype(hn_ref.dtype)
        o_ref[...] = acc.astype(o_ref.dtype)

    def part():
        return jnp.dot(a_ref[...], w_ref[...], preferred_element_type=F32)

    if nk == 1:
        finish(part())
        return
    acc_ref = rest[0] if rest else o_ref
    k = pl.program_id(2)

    @pl.when(k == 0)
    def _():
        acc_ref[...] = part()

    if nk > 2:
        @pl.when(jnp.logical_and(k > 0, k < nk - 1))
        def _():
            acc_ref[...] += part()

    @pl.when(k == nk - 1)
    def _():
        finish(acc_ref[...] + part())


def matmul(a, w, *, tm, tn, tk, out_dtype=F32, epilogue="none", res=None, mod=None,
           gate_blk=0, rows_per_batch=None, norm_gain=None, norm_mod=None, shift_blk=0,
           scale_blk=0):
    M, K = a.shape
    N = w.shape[1]
    nk = K // tk
    assert M % tm == 0 and N % tn == 0 and K % tk == 0
    in_specs = [pl.BlockSpec((tm, tk), lambda i, j, k: (i, k)),
                pl.BlockSpec((tk, tn), lambda i, j, k: (k, j))]
    args = [a, w]
    out_spec = pl.BlockSpec((tm, tn), lambda i, j, k: (i, j))
    out_specs, out_shape = out_spec, jax.ShapeDtypeStruct((M, N), out_dtype)
    if epilogue.startswith("residual"):
        assert rows_per_batch % tm == 0 and tn == D_MODEL == N
        batch = lambda i: i * tm // rows_per_batch
        chunk = lambda blk: pl.BlockSpec((None, 1, tn), lambda i, j, k: (batch(i), 0, blk))
        in_specs += [out_spec, chunk(gate_blk)]
        args += [res, mod]
        if epilogue != "residual":
            in_specs.append(pl.BlockSpec((1, tn), lambda i, j, k: (0, 0)))
            args.append(norm_gain.reshape(1, tn))
        if epilogue == "residual_norm":
            in_specs += [chunk(shift_blk), chunk(scale_blk)]
            args += [norm_mod, norm_mod]
            out_specs = [out_spec, out_spec]
            out_shape = [out_shape, jax.ShapeDtypeStruct((M, N), BF16)]
    scratch = [pltpu.VMEM((tm, tn), F32)] if nk > 1 and out_dtype != F32 else []
    return pl.pallas_call(
        functools.partial(_mm_kernel, nk=nk, epilogue=epilogue),
        grid=(M // tm, N // tn, nk),
        in_specs=in_specs,
        out_specs=out_specs,
        out_shape=out_shape,
        scratch_shapes=scratch,
        compiler_params=_cparams(("parallel", "parallel", "arbitrary")),
        name="mm_" + epilogue,
    )(*args)


def rope_tables(S):
    inv_freq = ROPE_THETA ** (-jnp.arange(0, HEAD_DIM, 2, dtype=F32) / HEAD_DIM)
    ang = jnp.arange(S, dtype=F32)[:, None] * inv_freq[None, :]
    cos, sin = jnp.cos(ang), jnp.sin(ang)
    return jnp.concatenate([cos, cos], axis=-1), jnp.concatenate([-sin, sin], axis=-1)


CAST, ROPE, ROPE_Q, SCALE_Q = 0, 1, 2, 3


def _proj_heads_kernel(a_ref, w_ref, c_ref, s_ref, *o_refs, kinds):
    if len(o_refs) > len(kinds):
        o_refs, wb_ref = o_refs[:-1], o_refs[-1]

        @pl.when(pl.program_id(0) == 0)
        def _():
            wb_ref[...] = w_ref[...].astype(BF16)
        w_ref = wb_ref
    acc = jnp.dot(a_ref[...], w_ref[...], preferred_element_type=F32)
    c = c_ref[...]
    s = s_ref[...]
    for o_ref, head_kinds in zip(o_refs, kinds):
        for h, kind in enumerate(head_kinds):
            sl = slice(h * LANE, (h + 1) * LANE)
            t = acc[:, sl]
            if kind in (ROPE, ROPE_Q):
                t = t * c + pltpu.roll(t, HEAD_DIM // 2, 1) * s
            if kind in (ROPE_Q, SCALE_Q):
                t = t * QK_SCALE
            o_ref[:, sl] = t.astype(o_ref.dtype)


def proj_heads(a, w, cos2, sin2, kinds, *, tm, col_blk=0):
    M, K = a.shape
    N = len(kinds[0]) * LANE
    nt = cos2.shape[0] // tm
    tab = pl.BlockSpec((tm, LANE), lambda i: (i % nt, 0))
    out_spec = pl.BlockSpec((tm, N), lambda i: (i, 0))
    cast = w.dtype != BF16
    return pl.pallas_call(
        functools.partial(_proj_heads_kernel, kinds=tuple(kinds)),
        grid=(M // tm,),
        in_specs=[pl.BlockSpec((tm, K), lambda i: (i, 0)),
                  pl.BlockSpec((K, N), lambda i: (0, col_blk), pipeline_mode=pl.Buffered(1)),
                  tab, tab],
        out_specs=[out_spec] * len(kinds),
        out_shape=[jax.ShapeDtypeStruct((M, N), BF16)] * len(kinds),
        scratch_shapes=[pltpu.VMEM((K, N), BF16)] if cast else [],
        compiler_params=_cparams(("arbitrary",)),
        name="proj_heads",
    )(a, w, cos2, sin2)


def _ones_column(n):
    return jnp.where(lax.broadcasted_iota(jnp.int32, (n, LANE), 1) == 0, 1.0, 0.0).astype(BF16)


def _softmax_pv(s, m, v, ones):
    p = jnp.exp2(s - m).astype(BF16)
    res = jnp.dot(p, jnp.concatenate([v, ones], axis=1), preferred_element_type=F32)
    return res[:, LANE:LANE + 1], res[:, :LANE]


def _banded_kernel(q_ref, k_ref, v_ref, o_ref, *lse_refs, R, hb, nwin, window, tq, qsub, nkeys):
    i = pl.program_id(3)
    L = k_ref.shape[0]
    rows = R * qsub
    rel = (lax.broadcasted_iota(jnp.int32, (rows, nkeys), 0) & (qsub - 1)) \
        - lax.broadcasted_iota(jnp.int32, (rows, nkeys), 1)
    ones = _ones_column(nkeys)
    units = [(h, j) for j in range(tq // qsub) for h in range(hb)]
    lane = lax.broadcasted_iota(jnp.int32, (rows, LANE), 1)
    lse_tile = jnp.zeros((rows, LANE), F32)

    def scores(h, j):
        q0 = i * tq + j * qsub
        kstart = pl.multiple_of(jnp.clip(q0 - nwin * LANE, 0, L - nkeys), LANE)
        qj = q_ref[j * qsub:(j + 1) * qsub, h * R * LANE:(h + 1) * R * LANE]
        if R > 1:
            q = jnp.concatenate([qj[:, r * LANE:(r + 1) * LANE] for r in range(R)], axis=0)
        else:
            q = qj
        s = lax.dot_general(q, k_ref[pl.ds(kstart, nkeys), h * LANE:(h + 1) * LANE], NT_DIMS,
                            preferred_element_type=F32)
        return q0, kstart, s

    ahead = 2
    pending = [scores(*u) for u in units[:ahead]]
    for n, (h, j) in enumerate(units):
        if n + ahead < len(units):
            pending.append(scores(*units[n + ahead]))
        q0, kstart, s = pending[n]
        v = v_ref[pl.ds(kstart, nkeys), h * LANE:(h + 1) * LANE]
        diff = rel + (q0 - kstart)
        valid = lax.bitcast_convert_type(diff, jnp.uint32) <= jnp.uint32(window)
        s = jnp.where(valid, s, NEG_BIG)
        m = jnp.max(s, axis=-1, keepdims=True)
        l, o = _softmax_pv(s, m, v, ones)
        o = o / l
        for r in range(R):
            col = (h * R + r) * LANE
            o_ref[j * qsub:(j + 1) * qsub, col:col + LANE] = o[r * qsub:(r + 1) * qsub]
        if lse_refs:
            lse_tile = jnp.where(lane == h, m + jnp.log(l) * LOG2E, lse_tile)
            if h == hb - 1:
                lse_refs[0][j * qsub:(j + 1) * qsub, :] = lse_tile


def banded_attention(q_arr, k_arr, v_arr, *, n_kv, R, nwin, window, want_lse, q_blk=0, k_blk=0,
                     v_blk=0):
    B, dil, L, _ = q_arr.shape
    tq = min(1024 if R > 1 else 512, L)
    qsub = LANE
    nkeys = min(qsub + nwin * LANE, L)
    hb = 4 if R == 1 else 1
    assert L % tq == 0 and n_kv % hb == 0
    assert q_blk % (hb * R) == 0 and k_blk % hb == 0 and v_blk % hb == 0
    width = n_kv * R * LANE
    out_shape = [jax.ShapeDtypeStruct((B, dil, L, width), F32)]
    out_specs = [pl.BlockSpec((None, None, tq, hb * R * LANE), lambda b, r, h, i: (b, r, i, h))]
    if want_lse:
        out_shape.append(jax.ShapeDtypeStruct((B, dil, L, n_kv // hb * LANE), F32))
        out_specs.append(pl.BlockSpec((None, None, tq, LANE), lambda b, r, h, i: (b, r, i, h)))
    kv_spec = lambda blk: pl.BlockSpec((None, None, L, hb * LANE),
                                       lambda b, r, h, i: (b, r, 0, blk // hb + h))
    return pl.pallas_call(
        functools.partial(_banded_kernel, R=R, hb=hb, nwin=nwin, window=window, tq=tq, qsub=qsub,
                          nkeys=nkeys),
        grid=(B, dil, n_kv // hb, L // tq),
        in_specs=[pl.BlockSpec((None, None, tq, hb * R * LANE),
                               lambda b, r, h, i: (b, r, i, q_blk // (hb * R) + h)),
                  kv_spec(k_blk), kv_spec(v_blk)],
        out_specs=out_specs,
        out_shape=out_shape,
        compiler_params=_cparams(("parallel", "parallel", "parallel", "arbitrary")),
        name="banded_attn",
    )(q_arr, k_arr, v_arr)


def _compress_kernel(x_ref, pe_ref, w1_ref, w2_ref, o_ref):
    n = x_ref.shape[0] // CMP_STRIDE
    a = jnp.zeros((n, CMP_HIDDEN), F32)
    b = jnp.zeros((n, CMP_HIDDEN), F32)
    for j in range(CMP_STRIDE):
        xj = x_ref[pl.ds(j, n, stride=CMP_STRIDE), :]
        top = (xj + pe_ref[j:j + 1, :]).astype(BF16)
        bot = (xj + pe_ref[CMP_STRIDE + j:CMP_STRIDE + j + 1, :]).astype(BF16)
        a = a + jnp.dot(top, w1_ref[j * HEAD_DIM:(j + 1) * HEAD_DIM, :],
                        preferred_element_type=F32)
        b = b + jnp.dot(bot, w1_ref[(CMP_STRIDE + j) * HEAD_DIM:(CMP_STRIDE + j + 1) * HEAD_DIM, :],
                        preferred_element_type=F32)
    hid = a + pltpu.roll(b, n - 1, 0)
    o_ref[...] = jnp.dot(jax.nn.gelu(hid).astype(BF16), w2_ref[...], preferred_element_type=F32)


def compress(kv, pe, w1, w2):
    B, S, _ = kv.shape
    G = NSA_KV_GROUPS
    NP = S // CMP_STRIDE
    return pl.pallas_call(
        _compress_kernel,
        grid=(2, B, G),
        in_specs=[pl.BlockSpec((None, S, HEAD_DIM), lambda t, b, g: (b, 0, t * G + g)),
                  pl.BlockSpec((None, CMP_BLOCK, HEAD_DIM), lambda t, b, g: (t, 0, 0)),
                  pl.BlockSpec((None, CMP_BLOCK * HEAD_DIM, CMP_HIDDEN), lambda t, b, g: (t, 0, 0)),
                  pl.BlockSpec((None, CMP_HIDDEN, HEAD_DIM), lambda t, b, g: (t, 0, 0))],
        out_specs=pl.BlockSpec((None, None, None, NP, HEAD_DIM), lambda t, b, g: (t, b, g, 0, 0)),
        out_shape=jax.ShapeDtypeStruct((2, B, G, NP, HEAD_DIM), F32),
        compiler_params=_cparams(("parallel", "parallel", "parallel")),
        name="nsa_compress",
    )(kv, pe, w1, w2)


def _cmp_attn_kernel(q_ref, kc_ref, vc_ref, ov_ref, o_ref, imp_ref, *, tq, qsub):
    i = pl.program_id(2)
    R = NSA_REP
    nsub = tq // qsub
    rows = R * qsub
    n_var = kc_ref.shape[0] // LANE

    def run(nk):
        kc = kc_ref[0:nk, :].astype(BF16)
        vc = vc_ref[0:nk, :].astype(BF16)
        ov = ov_ref[:, 0:nk]
        tloc = lax.broadcasted_iota(jnp.int32, (rows, nk), 0) & (qsub - 1)
        cmp_end = lax.broadcasted_iota(jnp.int32, (rows, nk), 1) * CMP_STRIDE + (CMP_BLOCK - 1)

        def scores(u):
            qu = q_ref[u * qsub:(u + 1) * qsub, :]
            q = jnp.concatenate([qu[:, r * LANE:(r + 1) * LANE] for r in range(R)], axis=0)
            return lax.dot_general(q, kc, NT_DIMS, preferred_element_type=F32)

        ahead = 2
        pending = [scores(u) for u in range(min(ahead, nsub))]
        for u in range(nsub):
            if u + ahead < nsub:
                pending.append(scores(u + ahead))
            valid = cmp_end <= tloc + (i * tq + u * qsub)
            s = jnp.where(valid, pending[u], NEG_BIG)
            m = jnp.max(s, axis=-1, keepdims=True)
            p = jnp.where(valid, jnp.exp2(s - m), 0.0)
            l = jnp.sum(p, axis=-1, keepdims=True)
            p = p / jnp.where(l > 0, l, 1.0)
            o = jnp.dot(p.astype(BF16), vc, preferred_element_type=F32)
            for r in range(R):
                o_ref[u * qsub:(u + 1) * qsub, r * LANE:(r + 1) * LANE] = \
                    o[r * qsub:(r + 1) * qsub]
            psum = p[0:qsub]
            for r in range(1, R):
                psum = psum + p[r * qsub:(r + 1) * qsub]
            p_hi = psum.astype(BF16)
            p_lo = (psum - p_hi.astype(F32)).astype(BF16)
            imp_ref[:, u * qsub:(u + 1) * qsub] = (
                lax.dot_general(ov, p_hi, NT_DIMS, preferred_element_type=F32)
                + lax.dot_general(ov, p_lo, NT_DIMS, preferred_element_type=F32))

    need = ((i + 1) * tq - CMP_BLOCK) // CMP_STRIDE + 1
    var = jnp.clip((need - 1) // LANE, 0, n_var - 1)
    for v in range(n_var):
        pl.when(var == v)(functools.partial(run, (v + 1) * LANE))


def cmp_attention(q, kvc, overlap_t, tq=1024, qsub=128):
    B, S, _ = q.shape
    G = NSA_KV_GROUPS
    NP = kvc.shape[3]
    n_sel = overlap_t.shape[0]
    qw = NSA_REP * LANE
    return pl.pallas_call(
        functools.partial(_cmp_attn_kernel, tq=tq, qsub=qsub),
        grid=(B, G, S // tq),
        in_specs=[pl.BlockSpec((None, tq, qw), lambda b, g, i: (b, i, g)),
                  pl.BlockSpec((None, None, None, NP, LANE), lambda b, g, i: (0, b, g, 0, 0)),
                  pl.BlockSpec((None, None, None, NP, LANE), lambda b, g, i: (1, b, g, 0, 0)),
                  pl.BlockSpec((n_sel, NP), lambda b, g, i: (0, 0))],
        out_specs=[pl.BlockSpec((None, tq, qw), lambda b, g, i: (b, i, g)),
                   pl.BlockSpec((None, None, n_sel, tq), lambda b, g, i: (b, g, 0, i))],
        out_shape=[jax.ShapeDtypeStruct((B, S, NSA_HEADS * LANE), F32),
                   jax.ShapeDtypeStruct((B, G, n_sel, S), F32)],
        compiler_params=_cparams(("parallel", "parallel", "parallel")),
        name="nsa_cmp_attn",
    )(q, kvc, kvc, overlap_t)


def _topk_kernel(imp_ref, o_ref, *, tq):
    i = pl.program_id(2)
    imp = imp_ref[...]
    n_sel = imp.shape[0]
    blk = lax.broadcasted_iota(jnp.int32, imp.shape, 0)
    t = i * tq + lax.broadcasted_iota(jnp.int32, imp.shape, 1)
    cur = t // SEL_BLOCK
    avail = blk <= cur
    forced = jnp.where(blk == 0, 1.0, jnp.where(blk == cur, 1.0, jnp.where(blk == cur - 1, 1.0, 0.0)))
    score = jnp.where(avail, imp + FORCED_BONUS * forced, -jnp.inf)
    picked = jnp.zeros(imp.shape, F32)
    for _ in range(min(SEL_TOPK, n_sel)):
        mx = jnp.max(score, axis=0, keepdims=True)
        first = jnp.min(jnp.where(score == mx, blk, n_sel), axis=0, keepdims=True)
        hit = blk == first
        picked = jnp.where(hit, 1.0, picked)
        score = jnp.where(hit, -jnp.inf, score)
    feat = jnp.where(avail, jnp.where(picked > 0.0, 0.0, SEL_OFF), SEL_OFF)
    o_ref[...] = feat.T.astype(o_ref.dtype)


def topk_select(imp_t, tq=1024):
    B, G, n_sel, S = imp_t.shape
    tq = min(tq, S)
    return pl.pallas_call(
        functools.partial(_topk_kernel, tq=tq),
        grid=(B, G, S // tq),
        in_specs=[pl.BlockSpec((None, None, n_sel, tq), lambda b, g, i: (b, g, 0, i))],
        out_specs=pl.BlockSpec((None, None, tq, n_sel), lambda b, g, i: (b, g, i, 0)),
        out_shape=jax.ShapeDtypeStruct((B, G, S, n_sel), BF16),
        compiler_params=_cparams(("parallel", "parallel", "parallel")),
        name="nsa_topk",
    )(imp_t)


def _sel_attn_kernel(q_ref, mf_ref, k_ref, e_ref, vt_ref, o_ref, *, tq, qsub, tkv):
    i = pl.program_id(2)
    R = NSA_REP
    nsub = tq // qsub
    rows = R * qsub
    q0 = i * tq
    ntile = (q0 + tq - 1) // tkv + 1
    rel = (lax.broadcasted_iota(jnp.int32, (tkv, rows), 1) & (qsub - 1)) \
        - lax.broadcasted_iota(jnp.int32, (tkv, rows), 0)

    qa = []
    for u in range(nsub):
        qu = q_ref[u * qsub:(u + 1) * qsub, :]
        mf = mf_ref[u * qsub:(u + 1) * qsub, :]
        qa.append(jnp.concatenate(
            [jnp.concatenate([qu[:, r * LANE:(r + 1) * LANE], mf], axis=1) for r in range(R)],
            axis=0))

    def body(j, carry, masked):
        ks = pl.multiple_of(j * tkv, tkv)
        ka = jnp.concatenate([k_ref[pl.ds(ks, tkv), :], e_ref[pl.ds(ks, tkv), :]], axis=1)
        vt = vt_ref[:, pl.ds(ks, tkv)]
        out = []
        nkeys = lambda u: (u + 1) * qsub if masked else tkv
        scores = lambda u: lax.dot_general(ka[:nkeys(u)], qa[u], NT_DIMS,
                                           preferred_element_type=F32)
        ahead = 2
        sts = [scores(u) for u in range(min(ahead, nsub))]
        for u in range(nsub):
            m, l, acc = carry[u]
            if u + ahead < nsub:
                sts.append(scores(u + ahead))
            st = sts[u]
            if masked:
                st = jnp.where(rel[:nkeys(u)] >= -u * qsub, st, NEG_BIG)
            m_new = jnp.maximum(m, jnp.max(st, axis=0, keepdims=True))
            a = jnp.exp2(m - m_new)
            res = jnp.dot(vt[:, :nkeys(u)], jnp.exp2(st - m_new).astype(BF16),
                          preferred_element_type=F32)
            out.append((m_new, a * l + res[LANE:LANE + 1, :], a * acc + res[:LANE, :]))
        return tuple(out)

    init = tuple((jnp.full((1, rows), NEG_BIG, F32), jnp.zeros((1, rows), F32),
                  jnp.zeros((LANE, rows), F32)) for _ in range(nsub))
    carry = lax.fori_loop(0, ntile - 1, lambda j, c: body(j, c, False), init)
    final = body(ntile - 1, carry, True)
    for u in range(nsub):
        _, l, acc = final[u]
        ot = acc / l
        for r in range(R):
            o_ref[u * qsub:(u + 1) * qsub, r * LANE:(r + 1) * LANE] = \
                ot[:, r * qsub:(r + 1) * qsub].T


def sel_attention(q_rot, mfeat, kvb, onehot, *, k_blk, v_blk, tq=2048, qsub=128, tkv=2048):
    B, S, _ = q_rot.shape
    G = NSA_KV_GROUPS
    n_sel = onehot.shape[1]
    qw = NSA_REP * LANE
    tkv = min(tkv, S)
    tq = min(tq, S)
    assert tkv == tq and S % tkv == 0
    v = kvb[:, :, v_blk * LANE:(v_blk + G) * LANE].reshape(B, S, G, LANE).transpose(0, 2, 3, 1)
    pad = jnp.zeros((B, G, 16, S), BF16).at[:, :, 0, :].set(1.0)
    vt = jnp.concatenate([v, pad], axis=2)
    return pl.pallas_call(
        functools.partial(_sel_attn_kernel, tq=tq, qsub=qsub, tkv=tkv),
        grid=(B, G, S // tq),
        in_specs=[pl.BlockSpec((None, tq, qw), lambda b, g, i: (b, i, g)),
                  pl.BlockSpec((None, None, tq, n_sel), lambda b, g, i: (b, g, i, 0)),
                  pl.BlockSpec((None, S, LANE), lambda b, g, i: (b, 0, k_blk + g)),
                  pl.BlockSpec((S, n_sel), lambda b, g, i: (0, 0)),
                  pl.BlockSpec((None, None, LANE + 16, S), lambda b, g, i: (b, g, 0, 0))],
        out_specs=pl.BlockSpec((None, tq, qw), lambda b, g, i: (b, i, g)),
        out_shape=jax.ShapeDtypeStruct((B, S, NSA_HEADS * LANE), F32),
        compiler_params=_cparams(("parallel", "parallel", "arbitrary")),
        name="nsa_sel_attn",
    )(q_rot, mfeat, kvb, onehot, vt)


def _nsa_combine_kernel(g_ref, oc_ref, os_ref, ow_ref, o_ref):
    gates = jax.nn.sigmoid(g_ref[...])
    for h in range(NSA_HEADS):
        sl = slice(h * LANE, (h + 1) * LANE)
        acc = gates[:, 3 * h:3 * h + 1] * oc_ref[:, sl]
        acc = acc + gates[:, 3 * h + 1:3 * h + 2] * os_ref[:, sl]
        acc = acc + gates[:, 3 * h + 2:3 * h + 3] * ow_ref[:, sl]
        o_ref[:, sl] = acc.astype(o_ref.dtype)


def nsa_combine(gate_logits, o_cmp, o_sel, o_win, ts=512):
    B, S, W = o_cmp.shape
    spec = pl.BlockSpec((None, ts, W), lambda b, i: (b, i, 0))
    return pl.pallas_call(
        _nsa_combine_kernel,
        grid=(B, S // ts),
        in_specs=[pl.BlockSpec((None, ts, LANE), lambda b, i: (b, i, 0)), spec, spec, spec],
        out_specs=spec,
        out_shape=jax.ShapeDtypeStruct((B, S, W), BF16),
        compiler_params=_cparams(("parallel", "parallel")),
        name="nsa_combine",
    )(gate_logits, o_cmp, o_sel, o_win)


def _dil_combine_kernel(*refs, dils, hb):
    n = len(dils)
    o_refs, lse_refs, out_ref = refs[:n], refs[n:2 * n], refs[2 * n]
    nat_o, nat_l = refs[2 * n + 1:3 * n + 1], refs[3 * n + 1:]
    ts = out_ref.shape[0]
    for h in range(out_ref.shape[1] // LANE):
        sl = slice(h * LANE, (h + 1) * LANE)
        lane = (h // hb) * LANE + h % hb
        for g, dil in enumerate(dils):
            for r in range(dil):
                rows = pl.ds(r, ts // dil, stride=dil) if dil > 1 else slice(None)
                nat_o[g][rows, :] = o_refs[g][r, :, sl]
                nat_l[g][rows, :] = jnp.broadcast_to(lse_refs[g][r, :, lane:lane + 1],
                                                     (ts // dil, LANE))
        lses = [r[...] for r in nat_l]
        mx = functools.reduce(jnp.maximum, lses)
        ws = [jnp.exp2(l - mx) for l in lses]
        den = functools.reduce(lambda a, b: a + b, ws)
        num = functools.reduce(lambda a, b: a + b, [w * r[...] for w, r in zip(ws, nat_o)])
        out_ref[:, sl] = (num / den).astype(out_ref.dtype)


def dil_combine(outs, lses, ts=256):
    dils = tuple(o.shape[1] for o in outs)
    B, W = outs[0].shape[0], outs[0].shape[-1]
    S = outs[0].shape[1] * outs[0].shape[2]
    WL = lses[0].shape[-1]
    hb = (W // LANE) // (WL // LANE)
    spec = lambda w: [pl.BlockSpec((None, d, ts // d, w), lambda b, i: (b, 0, i, 0)) for d in dils]
    return pl.pallas_call(
        functools.partial(_dil_combine_kernel, dils=dils, hb=hb),
        grid=(B, S // ts),
        in_specs=spec(W) + spec(WL),
        out_specs=pl.BlockSpec((None, ts, W), lambda b, i: (b, i, 0)),
        out_shape=jax.ShapeDtypeStruct((B, S, W), BF16),
        scratch_shapes=[pltpu.VMEM((ts, LANE), F32)] * (2 * len(dils)),
        compiler_params=_cparams(("parallel", "parallel")),
        name="dil_combine",
    )(*outs, *lses)


def _softplus(x):
    return jnp.maximum(x, 0.0) + jnp.log1p(jnp.exp(-jnp.abs(x)))


def _gate_band_starts():
    ntile = D_RNN // LANE
    starts = []
    for j in range(ntile):
        n_lo = (j * LANE) // RNN_BLOCK_DIM
        n_hi = (j * LANE + LANE - 1) // RNN_BLOCK_DIM
        lo = (n_lo * RNN_BLOCK_DIM) // LANE
        hi = -(-((n_hi + 1) * RNN_BLOCK_DIM) // LANE)
        assert hi - lo <= 4
        starts.append(min(lo, ntile - 4))
    return starts


def _rglru_kernel(y_ref, xr_ref, cw_ref, cb_ref, wg_ref, bg_ref, lam_ref, o_ref,
                  h_ref, tail_ref, *, ts, starts):
    i = pl.program_id(1)

    @pl.when(i == 0)
    def _():
        h_ref[...] = jnp.zeros_like(h_ref)
        tail_ref[...] = jnp.zeros_like(tail_ref)

    xr = xr_ref[...]
    ext = jnp.concatenate([tail_ref[...], xr], axis=0)
    x = cb_ref[...] + xr * cw_ref[CONV_WIDTH - 1:CONV_WIDTH, :]
    for d in range(1, CONV_WIDTH):
        shifted = pltpu.roll(ext, d, 0)[8:8 + ts]
        x = x + shifted * cw_ref[CONV_WIDTH - 1 - d:CONV_WIDTH - d, :]
    tail_ref[...] = xr[ts - 8:ts]

    xb = x.astype(BF16)
    gl = []
    for g in range(2):
        tiles = [jnp.dot(xb[:, a * LANE:(a + 4) * LANE], wg_ref[g, j],
                         preferred_element_type=F32) for j, a in enumerate(starts)]
        gl.append(jnp.concatenate(tiles, axis=1) + bg_ref[g:g + 1, :])
    r = jax.nn.sigmoid(gl[0])
    ig = jax.nn.sigmoid(gl[1])
    log_a = (-LRU_C) * r * _softplus(-lam_ref[...])
    a = jnp.exp(log_a)
    z = -jnp.tanh(log_a) * (a * a + 1.0)
    b = jnp.where(z > 0.0, z * lax.rsqrt(z), 0.0) * (ig * x)

    row = lax.broadcasted_iota(jnp.int32, a.shape, 0)
    d = 1
    while d < ts:
        if d < 8:
            keep = row >= d
            b = b + a * jnp.where(keep, pltpu.roll(b, d, 0), 0.0)
            a = a * jnp.where(keep, pltpu.roll(a, d, 0), 1.0)
        else:
            b = jnp.concatenate([b[:d], b[d:] + a[d:] * b[:ts - d]], axis=0)
            a = jnp.concatenate([a[:d], a[d:] * a[:ts - d]], axis=0)
        d *= 2
    h = a * h_ref[0:1, :] + b
    h_ref[0:1, :] = h[ts - 1:ts, :]
    o_ref[...] = (h * jax.nn.gelu(y_ref[...])).astype(o_ref.dtype)


def rglru_scan(proj, conv_w, conv_b, wband, b_gate, lam, ts=128):
    B, S, _ = proj.shape
    C = D_RNN
    starts = _gate_band_starts()
    vec = lambda n: pl.BlockSpec((n, C), lambda b, i: (0, 0))
    return pl.pallas_call(
        functools.partial(_rglru_kernel, ts=ts, starts=starts),
        grid=(B, S // ts),
        in_specs=[pl.BlockSpec((None, ts, C), lambda b, i: (b, i, 0)),
                  pl.BlockSpec((None, ts, C), lambda b, i: (b, i, 1)),
                  vec(CONV_WIDTH), vec(1),
                  pl.BlockSpec(wband.shape, lambda b, i: (0, 0, 0, 0)),
                  vec(2), vec(1)],
        out_specs=pl.BlockSpec((None, ts, C), lambda b, i: (b, i, 0)),
        out_shape=jax.ShapeDtypeStruct((B, S, C), BF16),
        scratch_shapes=[pltpu.VMEM((8, C), F32), pltpu.VMEM((8, C), F32)],
        compiler_params=_cparams(("arbitrary", "arbitrary")),
        name="rglru_scan",
    )(proj, proj, conv_w, conv_b.reshape(1, C), wband, b_gate, lam.reshape(1, C))


def _gate_band_weights(w_gate):
    starts = _gate_band_starts()
    dense = jnp.stack([jax.scipy.linalg.block_diag(*[w_gate[g, n] for n in range(RNN_BLOCKS)])
                       for g in range(2)])
    tiles = [dense[:, a * LANE:(a + 4) * LANE, j * LANE:(j + 1) * LANE]
             for j, a in enumerate(starts)]
    return jnp.stack(tiles, axis=1).astype(BF16)


def _row_tile(T):
    return 512 if T % 512 == 0 else T


def _proj_tile(S):
    return 1024 if S % 1024 == 0 else _row_tile(S)


RESIDUAL = dict(epilogue="residual")


def out_proj(a, w, x, mod, gate_blk, tk, tail, tm=None):
    B, S, D = x.shape
    T = B * S
    out = matmul(a.reshape(T, -1), w.astype(BF16), tm=tm or _row_tile(T), tn=D, tk=tk,
                 res=x.reshape(T, D), mod=mod, gate_blk=gate_blk, rows_per_batch=S, **tail)
    if isinstance(out, (list, tuple)):
        return tuple(o.reshape(B, S, D) for o in out)
    return out.reshape(B, S, D)


def nsa_mixer(hn, x, mod, gate_blk, w_in, cmp_pe, cmp_w1, cmp_w2, w_out, cos2, sin2,
              tail=RESIDUAL):
    B, S, D = x.shape
    T = B * S
    G = NSA_KV_GROUPS
    hq = NSA_HEADS * HEAD_DIM
    kvw = NSA_KV_WIDTH
    main = hq + 6 * kvw
    tm = _proj_tile(S)
    hn2 = hn.reshape(T, D)
    w_gate = jnp.pad(w_in[:, main:], ((0, 0), (0, LANE - 3 * NSA_HEADS))).astype(BF16)
    gate_logits = matmul(hn2, w_gate, tm=tm, tn=LANE, tk=D).reshape(B, S, LANE)

    q_cmp, q_rot = proj_heads(hn2, w_in[:, :hq].astype(BF16), cos2, sin2,
                              [(SCALE_Q,) * NSA_HEADS, (ROPE_Q,) * NSA_HEADS], tm=tm)
    g_rope, g_cast = (ROPE,) * G, (CAST,) * G
    (kvb,) = proj_heads(hn2, w_in[:, hq + 2 * kvw:main].astype(BF16), cos2, sin2,
                        [g_rope + g_cast + g_rope + g_cast], tm=tm)
    q_rot = q_rot.reshape(B, 1, S, hq)
    kvb = kvb.reshape(B, 1, S, 4 * kvw)

    npiece = S // CMP_STRIDE
    kv_cmp = matmul(hn2, w_in[:, hq:hq + 2 * kvw].astype(BF16), tm=tm, tn=2 * kvw, tk=D)
    kvc = compress(kv_cmp.reshape(B, S, 2 * kvw), cmp_pe, cmp_w1.astype(BF16),
                   cmp_w2.astype(BF16))

    n_sb = S // SEL_BLOCK
    cmp_start = np.arange(npiece) * CMP_STRIDE
    sel_start = np.arange(n_sb) * SEL_BLOCK
    overlap = ((cmp_start[:, None] < sel_start[None, :] + SEL_BLOCK)
               & (cmp_start[:, None] + CMP_BLOCK > sel_start[None, :]))
    overlap[npiece - 1] = False
    overlap_t = jnp.asarray(overlap.T, BF16)
    onehot = jnp.asarray(np.arange(S)[:, None] // SEL_BLOCK == np.arange(n_sb)[None, :], BF16)

    o_cmp, imp_t = cmp_attention(q_cmp.reshape(B, S, hq), kvc, overlap_t)
    mfeat = topk_select(imp_t)
    kb = kvw // LANE
    o_sel = sel_attention(q_rot.reshape(B, S, hq), mfeat, kvb.reshape(B, S, 4 * kvw),
                          onehot, k_blk=0, v_blk=kb)
    (o_win,) = banded_attention(q_rot, kvb, kvb, n_kv=G, R=NSA_REP, k_blk=2 * kb, v_blk=3 * kb,
                                nwin=(NSA_WINDOW - 1 + LANE - 1) // LANE,
                                window=NSA_WINDOW - 1, want_lse=False)
    o = nsa_combine(gate_logits, o_cmp, o_sel, o_win.reshape(B, S, hq))
    return out_proj(o, w_out, x, mod, gate_blk, hq, tail)


def dilated_mixer(hn, x, mod, gate_blk, w_in, w_out, cos2, sin2, tail=RESIDUAL):
    B, S, D = x.shape
    T = B * S
    H = DIL_HEADS
    hw = H * HEAD_DIM
    tm = _proj_tile(S)
    outs, lses = [], []
    for g, (window, dil) in enumerate(DIL_PATTERNS):
        w = window // dil
        L = S // dil
        regroup = lambda t: t.reshape(t.shape[:-2] + (L, dil, t.shape[-1])).swapaxes(-3, -2)
        hn_g = regroup(hn).reshape(T, D)
        cos_g, sin_g = regroup(cos2).reshape(S, LANE), regroup(sin2).reshape(S, LANE)
        q, k, v = [proj_heads(hn_g, w_in, cos_g, sin_g, [(kind,) * H], tm=tm,
                              col_blk=3 * g + c)[0].reshape(B, dil, L, hw)
                   for c, kind in enumerate((ROPE_Q, ROPE, CAST))]
        o, lse = banded_attention(q, k, v, n_kv=H, R=1, nwin=(w + LANE - 1) // LANE, window=w,
                                  want_lse=True)
        outs.append(o)
        lses.append(lse)
    o = dil_combine(outs, lses)
    return out_proj(o, w_out, x, mod, gate_blk, hw, tail)


def rglru_mixer(hn, x, mod, gate_blk, w_in, conv_w, conv_b, w_gate, b_gate, lam, w_out,
                tail=RESIDUAL):
    B, S, D = x.shape
    T = B * S
    tm = _row_tile(T)
    proj = matmul(hn.reshape(T, D), w_in.astype(BF16), tm=tm, tn=D_RNN, tk=D)
    hy = rglru_scan(proj.reshape(B, S, 2 * D_RNN), conv_w, conv_b, _gate_band_weights(w_gate),
                    b_gate, lam)
    return out_proj(hy, w_out, x, mod, gate_blk, D_RNN, tail)


def mlp(hn, x, mod, gate_blk, w1, w2, tail=RESIDUAL):
    B, S, D = x.shape
    T = B * S
    h = matmul(hn.reshape(T, D), w1.astype(BF16), tm=2 * _row_tile(T) if T % 1024 == 0 else T,
               tn=2048, tk=D, out_dtype=BF16, epilogue="relu2")
    return out_proj(h, w2, x, mod, gate_blk, 1024, tail, tm=1024 if T % 1024 == 0 else None)


def kernel(x, c, l0_w_ada, l0_b_ada, l0_norm1, l0_w_in, l0_cmp_pe, l0_cmp_w1, l0_cmp_w2, l0_w_out, l0_norm2, l0_w_ff1, l0_w_ff2, l1_w_ada, l1_b_ada, l1_norm1, l1_w_in, l1_w_out, l1_norm2, l1_w_ff1, l1_w_ff2, l2_w_ada, l2_b_ada, l2_norm1, l2_w_in, l2_conv_w, l2_conv_b, l2_w_gate, l2_b_gate, l2_lambda, l2_w_out, l2_norm2, l2_w_ff1, l2_w_ff2, l3_w_ada, l3_b_ada, l3_norm1, l3_w_in, l3_cmp_pe, l3_cmp_w1, l3_cmp_w2, l3_w_out, l3_norm2, l3_w_ff1, l3_w_ff2, norm_f):
    B, S, D = x.shape
    cos2, sin2 = rope_tables(S)
    layers = (
        (l0_w_ada, l0_b_ada, l0_norm1, l0_norm2, l0_w_ff1, l0_w_ff2,
         (l0_w_in, l0_cmp_pe, l0_cmp_w1, l0_cmp_w2, l0_w_out)),
        (l1_w_ada, l1_b_ada, l1_norm1, l1_norm2, l1_w_ff1, l1_w_ff2, (l1_w_in, l1_w_out)),
        (l2_w_ada, l2_b_ada, l2_norm1, l2_norm2, l2_w_ff1, l2_w_ff2,
         (l2_w_in, l2_conv_w, l2_conv_b, l2_w_gate, l2_b_gate, l2_lambda, l2_w_out)),
        (l3_w_ada, l3_b_ada, l3_norm1, l3_norm2, l3_w_ff1, l3_w_ff2,
         (l3_w_in, l3_cmp_pe, l3_cmp_w1, l3_cmp_w2, l3_w_out)),
    )
    mods = [adaln(c, l[0], l[1]).reshape(B, 1, 6 * D) for l in layers]
    hn = modulate(x, layers[0][2], mods[0], 0, 1)
    for li in range(DEPTH):
        _, _, _, n2, ff1, ff2, mix = layers[li]
        mod = mods[li]
        tail = dict(epilogue="residual_norm", norm_gain=n2, norm_mod=mod, shift_blk=3, scale_blk=4)
        kind = li % N_MIXERS
        if kind == 0:
            x, hn = nsa_mixer(hn, x, mod, 2, *mix, cos2, sin2, tail=tail)
        elif kind == 1:
            x, hn = dilated_mixer(hn, x, mod, 2, *mix, cos2, sin2, tail=tail)
        else:
            x, hn = rglru_mixer(hn, x, mod, 2, *mix, tail=tail)
        if li + 1 < DEPTH:
            tail = dict(epilogue="residual_norm", norm_gain=layers[li + 1][2],
                        norm_mod=mods[li + 1], shift_blk=0, scale_blk=1)
            x, hn = mlp(hn, x, mod, 5, ff1, ff2, tail=tail)
        else:
            return mlp(hn, x, mod, 5, ff1, ff2,
                       tail=dict(epilogue="residual_final", norm_gain=norm_f))
```

```python
import functools
import math

import jax
import jax.numpy as jnp
import numpy as np
from jax import lax
from jax.experimental import pallas as pl
from jax.experimental.pallas import tpu as pltpu

F32 = jnp.float32
BF16 = jnp.bfloat16

D_MODEL = 2048
DEPTH = 4
N_MIXERS = 3
HEAD_DIM = 128
ROPE_THETA = 10000.0
NORM_EPS = 1e-6

NSA_HEADS = D_MODEL // HEAD_DIM
NSA_KV_GROUPS = 4
NSA_REP = NSA_HEADS // NSA_KV_GROUPS
NSA_KV_WIDTH = NSA_KV_GROUPS * HEAD_DIM
CMP_BLOCK = 32
CMP_STRIDE = 16
CMP_HIDDEN = 4 * HEAD_DIM
SEL_BLOCK = 64
SEL_TOPK = 16
NSA_WINDOW = 512
FORCED_BONUS = 1e9

DIL_HEADS = D_MODEL // HEAD_DIM
DIL_PATTERNS = ((128, 1), (512, 4), (2048, 16))

D_RNN = 2688
RNN_BLOCKS = 16
RNN_BLOCK_DIM = D_RNN // RNN_BLOCKS
CONV_WIDTH = 4
LRU_C = 8.0

LANE = 128
LOG2E = math.log2(math.e)
QK_SCALE = LOG2E / math.sqrt(HEAD_DIM)
NEG_BIG = -1e30
SEL_OFF = -float(2 ** 30)
VMEM_LIMIT = 56 * 1024 * 1024

NT_DIMS = (((1,), (1,)), ((), ()))


def _cparams(sem):
    return pltpu.CompilerParams(dimension_semantics=sem, vmem_limit_bytes=VMEM_LIMIT)


def _adaln_kernel(c_ref, w_ref, b_ref, o_ref):
    w = w_ref[...]
    for b in range(c_ref.shape[0]):
        c = c_ref[b]
        cond = c * jax.nn.sigmoid(c)
        o_ref[b:b + 1, :] = jnp.sum(w * cond, axis=0, keepdims=True) + b_ref[...]


def adaln(c, w_ada, b_ada, tn=1024):
    B, D = c.shape
    N = w_ada.shape[1]
    return pl.pallas_call(
        _adaln_kernel,
        grid=(N // tn,),
        in_specs=[pl.BlockSpec((B, D, 1), lambda j: (0, 0, 0)),
                  pl.BlockSpec((D, tn), lambda j: (0, j)),
                  pl.BlockSpec((1, tn), lambda j: (0, j))],
        out_specs=pl.BlockSpec((B, tn), lambda j: (0, j)),
        out_shape=jax.ShapeDtypeStruct((B, N), F32),
        compiler_params=_cparams(("parallel",)),
        name="adaln",
    )(c.reshape(B, D, 1), w_ada, b_ada.reshape(1, N))


def _modulate_kernel(x_ref, gain_ref, sh_ref, sc_ref, o_ref):
    x = x_ref[...]
    ms = jnp.mean(x * x, axis=-1, keepdims=True)
    y = x * lax.rsqrt(ms + NORM_EPS) * gain_ref[...]
    o_ref[...] = (y * (1.0 + sc_ref[...]) + sh_ref[...]).astype(o_ref.dtype)


def modulate(x, gain, mod, shift_blk, scale_blk, ts=512):
    B, S, D = x.shape
    return pl.pallas_call(
        _modulate_kernel,
        grid=(B, S // ts),
        in_specs=[pl.BlockSpec((None, ts, D), lambda b, i: (b, i, 0)),
                  pl.BlockSpec((1, D), lambda b, i: (0, 0)),
                  pl.BlockSpec((None, 1, D), lambda b, i: (b, 0, shift_blk)),
                  pl.BlockSpec((None, 1, D), lambda b, i: (b, 0, scale_blk))],
        out_specs=pl.BlockSpec((None, ts, D), lambda b, i: (b, i, 0)),
        out_shape=jax.ShapeDtypeStruct((B, S, D), BF16),
        compiler_params=_cparams(("parallel", "parallel")),
        name="modulate",
    )(x, gain.reshape(1, D), mod, mod)


def _mm_kernel(*refs, nk, epilogue):
    refs = list(refs)
    a_ref, w_ref = refs[:2]
    del refs[:2]
    if epilogue.startswith("residual"):
        res_ref, gate_ref = refs[:2]
        del refs[:2]
    if epilogue == "residual_norm":
        gain_ref, sh_ref, sc_ref = refs[:3]
        del refs[:3]
    elif epilogue == "residual_final":
        gain_ref = refs.pop(0)
    o_ref = refs.pop(0)
    hn_ref = refs.pop(0) if epilogue == "residual_norm" else None
    rest = refs

    def finish(acc):
        if epilogue == "relu2":
            r = jnp.maximum(acc, 0.0)
            acc = r * r
        elif epilogue.startswith("residual"):
            acc = res_ref[...] + gate_ref[...] * acc
        if epilogue in ("residual_norm", "residual_final"):
            ms = jnp.mean(acc * acc, axis=-1, keepdims=True)
            y = acc * lax.rsqrt(ms + NORM_EPS) * gain_ref[...]
            if epilogue == "residual_final":
                o_ref[...] = y
                return
            hn_ref[...] = (y * (1.0 + sc_ref[...]) + sh_ref[...]).astype(hn_ref.dtype)
        o_ref[...] = acc.astype(o_ref.dtype)

    def part():
        return jnp.dot(a_ref[...], w_ref[...], preferred_element_type=F32)

    if nk == 1:
        finish(part())
        return
    acc_ref = rest[0] if rest else o_ref
    k = pl.program_id(2)

    @pl.when(k == 0)
    def _():
        acc_ref[...] = part()

    if nk > 2:
        @pl.when(jnp.logical_and(k > 0, k < nk - 1))
        def _():
            acc_ref[...] += part()

    @pl.when(k == nk - 1)
    def _():
        finish(acc_ref[...] + part())


def matmul(a, w, *, tm, tn, tk, out_dtype=F32, epilogue="none", res=None, mod=None,
           gate_blk=0, rows_per_batch=None, norm_gain=None, norm_mod=None, shift_blk=0,
           scale_blk=0):
    M, K = a.shape
    N = w.shape[1]
    nk = K // tk
    assert M % tm == 0 and N % tn == 0 and K % tk == 0
    in_specs = [pl.BlockSpec((tm, tk), lambda i, j, k: (i, k)),
                pl.BlockSpec((tk, tn), lambda i, j, k: (k, j))]
    args = [a, w]
    out_spec = pl.BlockSpec((tm, tn), lambda i, j, k: (i, j))
    out_specs, out_shape = out_spec, jax.ShapeDtypeStruct((M, N), out_dtype)
    if epilogue.startswith("residual"):
        assert rows_per_batch % tm == 0 and tn == D_MODEL == N
        batch = lambda i: i * tm // rows_per_batch
        chunk = lambda blk: pl.BlockSpec((None, 1, tn), lambda i, j, k: (batch(i), 0, blk))
        in_specs += [out_spec, chunk(gate_blk)]
        args += [res, mod]
        if epilogue != "residual":
            in_specs.append(pl.BlockSpec((1, tn), lambda i, j, k: (0, 0)))
            args.append(norm_gain.reshape(1, tn))
        if epilogue == "residual_norm":
            in_specs += [chunk(shift_blk), chunk(scale_blk)]
            args += [norm_mod, norm_mod]
            out_specs = [out_spec, out_spec]
            out_shape = [out_shape, jax.ShapeDtypeStruct((M, N), BF16)]
    scratch = [pltpu.VMEM((tm, tn), F32)] if nk > 1 and out_dtype != F32 else []
    return pl.pallas_call(
        functools.partial(_mm_kernel, nk=nk, epilogue=epilogue),
        grid=(M // tm, N // tn, nk),
        in_specs=in_specs,
        out_specs=out_specs,
        out_shape=out_shape,
        scratch_shapes=scratch,
        compiler_params=_cparams(("parallel", "parallel", "arbitrary")),
        name="mm_" + epilogue,
    )(*args)


def rope_tables(S):
    inv_freq = ROPE_THETA ** (-jnp.arange(0, HEAD_DIM, 2, dtype=F32) / HEAD_DIM)
    ang = jnp.arange(S, dtype=F32)[:, None] * inv_freq[None, :]
    cos, sin = jnp.cos(ang), jnp.sin(ang)
    return jnp.concatenate([cos, cos], axis=-1), jnp.concatenate([-sin, sin], axis=-1)


CAST, ROPE, ROPE_Q, SCALE_Q = 0, 1, 2, 3


def _proj_heads_kernel(a_ref, w_ref, c_ref, s_ref, *o_refs, kinds):
    if len(o_refs) > len(kinds):
        o_refs, wb_ref = o_refs[:-1], o_refs[-1]

        @pl.when(pl.program_id(0) == 0)
        def _():
            wb_ref[...] = w_ref[...].astype(BF16)
        w_ref = wb_ref
    acc = jnp.dot(a_ref[...], w_ref[...], preferred_element_type=F32)
    c = c_ref[...]
    s = s_ref[...]
    for o_ref, head_kinds in zip(o_refs, kinds):
        for h, kind in enumerate(head_kinds):
            sl = slice(h * LANE, (h + 1) * LANE)
            t = acc[:, sl]
            if kind in (ROPE, ROPE_Q):
                t = t * c + pltpu.roll(t, HEAD_DIM // 2, 1) * s
            if kind in (ROPE_Q, SCALE_Q):
                t = t * QK_SCALE
            o_ref[:, sl] = t.astype(o_ref.dtype)


def proj_heads(a, w, cos2, sin2, kinds, *, tm, col_blk=0):
    M, K = a.shape
    N = len(kinds[0]) * LANE
    nt = cos2.shape[0] // tm
    tab = pl.BlockSpec((tm, LANE), lambda i: (i % nt, 0))
    out_spec = pl.BlockSpec((tm, N), lambda i: (i, 0))
    cast = w.dtype != BF16
    return pl.pallas_call(
        functools.partial(_proj_heads_kernel, kinds=tuple(kinds)),
        grid=(M // tm,),
        in_specs=[pl.BlockSpec((tm, K), lambda i: (i, 0)),
                  pl.BlockSpec((K, N), lambda i: (0, col_blk), pipeline_mode=pl.Buffered(1)),
                  tab, tab],
        out_specs=[out_spec] * len(kinds),
        out_shape=[jax.ShapeDtypeStruct((M, N), BF16)] * len(kinds),
        scratch_shapes=[pltpu.VMEM((K, N), BF16)] if cast else [],
        compiler_params=_cparams(("arbitrary",)),
        name="proj_heads",
    )(a, w, cos2, sin2)


def _ones_column(n):
    return jnp.where(lax.broadcasted_iota(jnp.int32, (n, LANE), 1) == 0, 1.0, 0.0).astype(BF16)


def _softmax_pv(s, m, v, ones):
    p = jnp.exp2(s - m).astype(BF16)
    res = jnp.dot(p, jnp.concatenate([v, ones], axis=1), preferred_element_type=F32)
    return res[:, LANE:LANE + 1], res[:, :LANE]


def _banded_kernel(q_ref, k_ref, v_ref, o_ref, *lse_refs, R, hb, nwin, window, tq, qsub, nkeys):
    i = pl.program_id(3)
    L = k_ref.shape[0]
    rows = R * qsub
    rel = (lax.broadcasted_iota(jnp.int32, (rows, nkeys), 0) & (qsub - 1)) \
        - lax.broadcasted_iota(jnp.int32, (rows, nkeys), 1)
    ones = _ones_column(nkeys)
    units = [(h, j) for j in range(tq // qsub) for h in range(hb)]
    lane = lax.broadcasted_iota(jnp.int32, (rows, LANE), 1)
    lse_tile = jnp.zeros((rows, LANE), F32)

    def scores(h, j):
        q0 = i * tq + j * qsub
        kstart = pl.multiple_of(jnp.clip(q0 - nwin * LANE, 0, L - nkeys), LANE)
        qj = q_ref[j * qsub:(j + 1) * qsub, h * R * LANE:(h + 1) * R * LANE]
        if R > 1:
            q = jnp.concatenate([qj[:, r * LANE:(r + 1) * LANE] for r in range(R)], axis=0)
        else:
            q = qj
        s = lax.dot_general(q, k_ref[pl.ds(kstart, nkeys), h * LANE:(h + 1) * LANE], NT_DIMS,
                            preferred_element_type=F32)
        return q0, kstart, s

    ahead = 2
    pending = [scores(*u) for u in units[:ahead]]
    for n, (h, j) in enumerate(units):
        if n + ahead < len(units):
            pending.append(scores(*units[n + ahead]))
        q0, kstart, s = pending[n]
        v = v_ref[pl.ds(kstart, nkeys), h * LANE:(h + 1) * LANE]
        diff = rel + (q0 - kstart)
        valid = lax.bitcast_convert_type(diff, jnp.uint32) <= jnp.uint32(window)
        s = jnp.where(valid, s, NEG_BIG)
        m = jnp.max(s, axis=-1, keepdims=True)
        l, o = _softmax_pv(s, m, v, ones)
        o = o / l
        for r in range(R):
            col = (h * R + r) * LANE
            o_ref[j * qsub:(j + 1) * qsub, col:col + LANE] = o[r * qsub:(r + 1) * qsub]
        if lse_refs:
            lse_tile = jnp.where(lane == h, m + jnp.log(l) * LOG2E, lse_tile)
            if h == hb - 1:
                lse_refs[0][j * qsub:(j + 1) * qsub, :] = lse_tile


def banded_attention(q_arr, k_arr, v_arr, *, n_kv, R, nwin, window, want_lse, q_blk=0, k_blk=0,
                     v_blk=0):
    B, dil, L, _ = q_arr.shape
    tq = min(1024 if R > 1 else 512, L)
    qsub = LANE
    nkeys = min(qsub + nwin * LANE, L)
    hb = 4 if R == 1 else 1
    assert L % tq == 0 and n_kv % hb == 0
    assert q_blk % (hb * R) == 0 and k_blk % hb == 0 and v_blk % hb == 0
    width = n_kv * R * LANE
    out_shape = [jax.ShapeDtypeStruct((B, dil, L, width), F32)]
    out_specs = [pl.BlockSpec((None, None, tq, hb * R * LANE), lambda b, r, h, i: (b, r, i, h))]
    if want_lse:
        out_shape.append(jax.ShapeDtypeStruct((B, dil, L, n_kv // hb * LANE), F32))
        out_specs.append(pl.BlockSpec((None, None, tq, LANE), lambda b, r, h, i: (b, r, i, h)))
    kv_spec = lambda blk: pl.BlockSpec((None, None, L, hb * LANE),
                                       lambda b, r, h, i: (b, r, 0, blk // hb + h))
    return pl.pallas_call(
        functools.partial(_banded_kernel, R=R, hb=hb, nwin=nwin, window=window, tq=tq, qsub=qsub,
                          nkeys=nkeys),
        grid=(B, dil, n_kv // hb, L // tq),
        in_specs=[pl.BlockSpec((None, None, tq, hb * R * LANE),
                               lambda b, r, h, i: (b, r, i, q_blk // (hb * R) + h)),
                  kv_spec(k_blk), kv_spec(v_blk)],
        out_specs=out_specs,
        out_shape=out_shape,
        compiler_params=_cparams(("parallel", "parallel", "parallel", "arbitrary")),
        name="banded_attn",
    )(q_arr, k_arr, v_arr)


def _compress_kernel(x_ref, pe_ref, w1_ref, w2_ref, o_ref):
    n = x_ref.shape[0] // CMP_STRIDE
    a = jnp.zeros((n, CMP_HIDDEN), F32)
    b = jnp.zeros((n, CMP_HIDDEN), F32)
    for j in range(CMP_STRIDE):
        xj = x_ref[pl.ds(j, n, stride=CMP_STRIDE), :]
        top = (xj + pe_ref[j:j + 1, :]).astype(BF16)
        bot = (xj + pe_ref[CMP_STRIDE + j:CMP_STRIDE + j + 1, :]).astype(BF16)
        a = a + jnp.dot(top, w1_ref[j * HEAD_DIM:(j + 1) * HEAD_DIM, :],
                        preferred_element_type=F32)
        b = b + jnp.dot(bot, w1_ref[(CMP_STRIDE + j) * HEAD_DIM:(CMP_STRIDE + j + 1) * HEAD_DIM, :],
                        preferred_element_type=F32)
    hid = a + pltpu.roll(b, n - 1, 0)
    o_ref[...] = jnp.dot(jax.nn.gelu(hid).astype(BF16), w2_ref[...], preferred_element_type=F32)


def compress(kv, pe, w1, w2):
    B, S, _ = kv.shape
    G = NSA_KV_GROUPS
    NP = S // CMP_STRIDE
    return pl.pallas_call(
        _compress_kernel,
        grid=(2, B, G),
        in_specs=[pl.BlockSpec((None, S, HEAD_DIM), lambda t, b, g: (b, 0, t * G + g)),
                  pl.BlockSpec((None, CMP_BLOCK, HEAD_DIM), lambda t, b, g: (t, 0, 0)),
                  pl.BlockSpec((None, CMP_BLOCK * HEAD_DIM, CMP_HIDDEN), lambda t, b, g: (t, 0, 0)),
                  pl.BlockSpec((None, CMP_HIDDEN, HEAD_DIM), lambda t, b, g: (t, 0, 0))],
        out_specs=pl.BlockSpec((None, None, None, NP, HEAD_DIM), lambda t, b, g: (t, b, g, 0, 0)),
        out_shape=jax.ShapeDtypeStruct((2, B, G, NP, HEAD_DIM), F32),
        compiler_params=_cparams(("parallel", "parallel", "parallel")),
        name="nsa_compress",
    )(kv, pe, w1, w2)


def _cmp_attn_kernel(q_ref, kc_ref, vc_ref, ov_ref, o_ref, imp_ref, *, tq, qsub):
    i = pl.program_id(2)
    R = NSA_REP
    nsub = tq // qsub
    rows = R * qsub
    n_var = kc_ref.shape[0] // LANE

    def run(nk):
        kc = kc_ref[0:nk, :].astype(BF16)
        vc = vc_ref[0:nk, :].astype(BF16)
        ov = ov_ref[:, 0:nk]
        tloc = lax.broadcasted_iota(jnp.int32, (rows, nk), 0) & (qsub - 1)
        cmp_end = lax.broadcasted_iota(jnp.int32, (rows, nk), 1) * CMP_STRIDE + (CMP_BLOCK - 1)

        def scores(u):
            qu = q_ref[u * qsub:(u + 1) * qsub, :]
            q = jnp.concatenate([qu[:, r * LANE:(r + 1) * LANE] for r in range(R)], axis=0)
            return lax.dot_general(q, kc, NT_DIMS, preferred_element_type=F32)

        ahead = 2
        pending = [scores(u) for u in range(min(ahead, nsub))]
        for u in range(nsub):
            if u + ahead < nsub:
                pending.append(scores(u + ahead))
            valid = cmp_end <= tloc + (i * tq + u * qsub)
            s = jnp.where(valid, pending[u], NEG_BIG)
            m = jnp.max(s, axis=-1, keepdims=True)
            p = jnp.where(valid, jnp.exp2(s - m), 0.0)
            l = jnp.sum(p, axis=-1, keepdims=True)
            p = p / jnp.where(l > 0, l, 1.0)
            o = jnp.dot(p.astype(BF16), vc, preferred_element_type=F32)
            for r in range(R):
                o_ref[u * qsub:(u + 1) * qsub, r * LANE:(r + 1) * LANE] = \
                    o[r * qsub:(r + 1) * qsub]
            psum = p[0:qsub]
            for r in range(1, R):
                psum = psum + p[r * qsub:(r + 1) * qsub]
            p_hi = psum.astype(BF16)
            p_lo = (psum - p_hi.astype(F32)).astype(BF16)
            imp_ref[:, u * qsub:(u + 1) * qsub] = (
                lax.dot_general(ov, p_hi, NT_DIMS, preferred_element_type=F32)
                + lax.dot_general(ov, p_lo, NT_DIMS, preferred_element_type=F32))

    need = ((i + 1) * tq - CMP_BLOCK) // CMP_STRIDE + 1
    var = jnp.clip((need - 1) // LANE, 0, n_var - 1)
    for v in range(n_var):
        pl.when(var == v)(functools.partial(run, (v + 1) * LANE))


def cmp_attention(q, kvc, overlap_t, tq=1024, qsub=128):
    B, S, _ = q.shape
    G = NSA_KV_GROUPS
    NP = kvc.shape[3]
    n_sel = overlap_t.shape[0]
    qw = NSA_REP * LANE
    return pl.pallas_call(
        functools.partial(_cmp_attn_kernel, tq=tq, qsub=qsub),
        grid=(B, G, S // tq),
        in_specs=[pl.BlockSpec((None, tq, qw), lambda b, g, i: (b, i, g)),
                  pl.BlockSpec((None, None, None, NP, LANE), lambda b, g, i: (0, b, g, 0, 0)),
                  pl.BlockSpec((None, None, None, NP, LANE), lambda b, g, i: (1, b, g, 0, 0)),
                  pl.BlockSpec((n_sel, NP), lambda b, g, i: (0, 0))],
        out_specs=[pl.BlockSpec((None, tq, qw), lambda b, g, i: (b, i, g)),
                   pl.BlockSpec((None, None, n_sel, tq), lambda b, g, i: (b, g, 0, i))],
        out_shape=[jax.ShapeDtypeStruct((B, S, NSA_HEADS * LANE), F32),
                   jax.ShapeDtypeStruct((B, G, n_sel, S), F32)],
        compiler_params=_cparams(("parallel", "parallel", "parallel")),
        name="nsa_cmp_attn",
    )(q, kvc, kvc, overlap_t)


def _topk_kernel(imp_ref, o_ref, *, tq):
    i = pl.program_id(2)
    imp = imp_ref[...]
    n_sel = imp.shape[0]
    blk = lax.broadcasted_iota(jnp.int32, imp.shape, 0)
    t = i * tq + lax.broadcasted_iota(jnp.int32, imp.shape, 1)
    cur = t // SEL_BLOCK
    avail = blk <= cur
    forced = jnp.where(blk == 0, 1.0, jnp.where(blk == cur, 1.0, jnp.where(blk == cur - 1, 1.0, 0.0)))
    score = jnp.where(avail, imp + FORCED_BONUS * forced, -jnp.inf)
    picked = jnp.zeros(imp.shape, F32)
    for _ in range(min(SEL_TOPK, n_sel)):
        mx = jnp.max(score, axis=0, keepdims=True)
        first = jnp.min(jnp.where(score == mx, blk, n_sel), axis=0, keepdims=True)
        hit = blk == first
        picked = jnp.where(hit, 1.0, picked)
        score = jnp.where(hit, -jnp.inf, score)
    feat = jnp.where(avail, jnp.where(picked > 0.0, 0.0, SEL_OFF), SEL_OFF)
    o_ref[...] = feat.T.astype(o_ref.dtype)


def topk_select(imp_t, tq=1024):
    B, G, n_sel, S = imp_t.shape
    tq = min(tq, S)
    return pl.pallas_call(
        functools.partial(_topk_kernel, tq=tq),
        grid=(B, G, S // tq),
        in_specs=[pl.BlockSpec((None, None, n_sel, tq), lambda b, g, i: (b, g, 0, i))],
        out_specs=pl.BlockSpec((None, None, tq, n_sel), lambda b, g, i: (b, g, i, 0)),
        out_shape=jax.ShapeDtypeStruct((B, G, S, n_sel), BF16),
        compiler_params=_cparams(("parallel", "parallel", "parallel")),
        name="nsa_topk",
    )(imp_t)


def _sel_attn_kernel(q_ref, mf_ref, k_ref, e_ref, vt_ref, o_ref, *, tq, qsub, tkv):
    i = pl.program_id(2)
    R = NSA_REP
    nsub = tq // qsub
    rows = R * qsub
    q0 = i * tq
    ntile = (q0 + tq - 1) // tkv + 1
    rel = (lax.broadcasted_iota(jnp.int32, (tkv, rows), 1) & (qsub - 1)) \
        - lax.broadcasted_iota(jnp.int32, (tkv, rows), 0)

    qa = []
    for u in range(nsub):
        qu = q_ref[u * qsub:(u + 1) * qsub, :]
        mf = mf_ref[u * qsub:(u + 1) * qsub, :]
        qa.append(jnp.concatenate(
            [jnp.concatenate([qu[:, r * LANE:(r + 1) * LANE], mf], axis=1) for r in range(R)],
            axis=0))

    def body(j, carry, masked):
        ks = pl.multiple_of(j * tkv, tkv)
        ka = jnp.concatenate([k_ref[pl.ds(ks, tkv), :], e_ref[pl.ds(ks, tkv), :]], axis=1)
        vt = vt_ref[:, pl.ds(ks, tkv)]
        out = []
        nkeys = lambda u: (u + 1) * qsub if masked else tkv
        scores = lambda u: lax.dot_general(ka[:nkeys(u)], qa[u], NT_DIMS,
                                           preferred_element_type=F32)
        ahead = 2
        sts = [scores(u) for u in range(min(ahead, nsub))]
        for u in range(nsub):
            m, l, acc = carry[u]
            if u + ahead < nsub:
                sts.append(scores(u + ahead))
            st = sts[u]
            if masked:
                st = jnp.where(rel[:nkeys(u)] >= -u * qsub, st, NEG_BIG)
            m_new = jnp.maximum(m, jnp.max(st, axis=0, keepdims=True))
            a = jnp.exp2(m - m_new)
            res = jnp.dot(vt[:, :nkeys(u)], jnp.exp2(st - m_new).astype(BF16),
                          preferred_element_type=F32)
            out.append((m_new, a * l + res[LANE:LANE + 1, :], a * acc + res[:LANE, :]))
        return tuple(out)

    init = tuple((jnp.full((1, rows), NEG_BIG, F32), jnp.zeros((1, rows), F32),
                  jnp.zeros((LANE, rows), F32)) for _ in range(nsub))
    carry = lax.fori_loop(0, ntile - 1, lambda j, c: body(j, c, False), init)
    final = body(ntile - 1, carry, True)
    for u in range(nsub):
        _, l, acc = final[u]
        ot = acc / l
        for r in range(R):
            o_ref[u * qsub:(u + 1) * qsub, r * LANE:(r + 1) * LANE] = \
                ot[:, r * qsub:(r + 1) * qsub].T


def sel_attention(q_rot, mfeat, kvb, onehot, *, k_blk, v_blk, tq=2048, qsub=128, tkv=2048):
    B, S, _ = q_rot.shape
    G = NSA_KV_GROUPS
    n_sel = onehot.shape[1]
    qw = NSA_REP * LANE
    tkv = min(tkv, S)
    tq = min(tq, S)
    assert tkv == tq and S % tkv == 0
    v = kvb[:, :, v_blk * LANE:(v_blk + G) * LANE].reshape(B, S, G, LANE).transpose(0, 2, 3, 1)
    pad = jnp.zeros((B, G, 16, S), BF16).at[:, :, 0, :].set(1.0)
    vt = jnp.concatenate([v, pad], axis=2)
    return pl.pallas_call(
        functools.partial(_sel_attn_kernel, tq=tq, qsub=qsub, tkv=tkv),
        grid=(B, G, S // tq),
        in_specs=[pl.BlockSpec((None, tq, qw), lambda b, g, i: (b, i, g)),
                  pl.BlockSpec((None, None, tq, n_sel), lambda b, g, i: (b, g, i, 0)),
                  pl.BlockSpec((None, S, LANE), lambda b, g, i: (b, 0, k_blk + g)),
                  pl.BlockSpec((S, n_sel), lambda b, g, i: (0, 0)),
                  pl.BlockSpec((None, None, LANE + 16, S), lambda b, g, i: (b, g, 0, 0))],
        out_specs=pl.BlockSpec((None, tq, qw), lambda b, g, i: (b, i, g)),
        out_shape=jax.ShapeDtypeStruct((B, S, NSA_HEADS * LANE), F32),
        compiler_params=_cparams(("parallel", "parallel", "arbitrary")),
        name="nsa_sel_attn",
    )(q_rot, mfeat, kvb, onehot, vt)


def _nsa_combine_kernel(g_ref, e_ref, oc_ref, os_ref, ow_ref, o_ref):
    gates = jax.nn.sigmoid(g_ref[...])
    hi = gates.astype(BF16)
    lo = (gates - hi.astype(F32)).astype(BF16)
    parts = jnp.concatenate([hi, lo], axis=1)
    acc = None
    for br, ref in enumerate((oc_ref, os_ref, ow_ref)):
        term = jnp.dot(parts, e_ref[br], preferred_element_type=F32) * ref[...]
        acc = term if acc is None else acc + term
    o_ref[...] = acc.astype(o_ref.dtype)


def nsa_combine(gate_logits, o_cmp, o_sel, o_win, ts=512):
    B, S, W = o_cmp.shape
    col = np.arange(LANE)[:, None]
    head = np.arange(W)[None, :] // LANE
    expand = np.stack([np.tile(col == 3 * head + br, (2, 1)) for br in range(3)])
    spec = pl.BlockSpec((None, ts, W), lambda b, i: (b, i, 0))
    return pl.pallas_call(
        _nsa_combine_kernel,
        grid=(B, S // ts),
        in_specs=[pl.BlockSpec((None, ts, LANE), lambda b, i: (b, i, 0)),
                  pl.BlockSpec((3, 2 * LANE, W), lambda b, i: (0, 0, 0)), spec, spec, spec],
        out_specs=spec,
        out_shape=jax.ShapeDtypeStruct((B, S, W), BF16),
        compiler_params=_cparams(("parallel", "parallel")),
        name="nsa_combine",
    )(gate_logits, jnp.asarray(expand, BF16), o_cmp, o_sel, o_win)


def _dil_combine_kernel(*refs, dils, hb):
    n = len(dils)
    o_refs, lse_refs, out_ref = refs[:n], refs[n:2 * n], refs[2 * n]
    nat_o, nat_l = refs[2 * n + 1:3 * n + 1], refs[3 * n + 1:]
    ts = out_ref.shape[0]
    for h in range(out_ref.shape[1] // LANE):
        sl = slice(h * LANE, (h + 1) * LANE)
        lane = (h // hb) * LANE + h % hb
        for g, dil in enumerate(dils):
            for r in range(dil):
                rows = pl.ds(r, ts // dil, stride=dil) if dil > 1 else slice(None)
                nat_o[g][rows, :] = o_refs[g][r, :, sl]
                nat_l[g][rows, :] = jnp.broadcast_to(lse_refs[g][r, :, lane:lane + 1],
                                                     (ts // dil, LANE))
        lses = [r[...] for r in nat_l]
        mx = functools.reduce(jnp.maximum, lses)
        ws = [jnp.exp2(l - mx) for l in lses]
        den = functools.reduce(lambda a, b: a + b, ws)
        num = functools.reduce(lambda a, b: a + b, [w * r[...] for w, r in zip(ws, nat_o)])
        out_ref[:, sl] = (num / den).astype(out_ref.dtype)


def dil_combine(outs, lses, ts=256):
    dils = tuple(o.shape[1] for o in outs)
    B, W = outs[0].shape[0], outs[0].shape[-1]
    S = outs[0].shape[1] * outs[0].shape[2]
    WL = lses[0].shape[-1]
    hb = (W // LANE) // (WL // LANE)
    spec = lambda w: [pl.BlockSpec((None, d, ts // d, w), lambda b, i: (b, 0, i, 0)) for d in dils]
    return pl.pallas_call(
        functools.partial(_dil_combine_kernel, dils=dils, hb=hb),
        grid=(B, S // ts),
        in_specs=spec(W) + spec(WL),
        out_specs=pl.BlockSpec((None, ts, W), lambda b, i: (b, i, 0)),
        out_shape=jax.ShapeDtypeStruct((B, S, W), BF16),
        scratch_shapes=[pltpu.VMEM((ts, LANE), F32)] * (2 * len(dils)),
        compiler_params=_cparams(("parallel", "parallel")),
        name="dil_combine",
    )(*outs, *lses)


def _softplus(x):
    return jnp.maximum(x, 0.0) + jnp.log1p(jnp.exp(-jnp.abs(x)))


def _gate_band_starts():
    ntile = D_RNN // LANE
    starts = []
    for j in range(ntile):
        n_lo = (j * LANE) // RNN_BLOCK_DIM
        n_hi = (j * LANE + LANE - 1) // RNN_BLOCK_DIM
        lo = (n_lo * RNN_BLOCK_DIM) // LANE
        hi = -(-((n_hi + 1) * RNN_BLOCK_DIM) // LANE)
        assert hi - lo <= 4
        starts.append(min(lo, ntile - 4))
    return starts


def _rglru_kernel(y_ref, xr_ref, cw_ref, cb_ref, wg_ref, bg_ref, lam_ref, o_ref,
                  h_ref, tail_ref, *, ts, starts):
    i = pl.program_id(1)

    @pl.when(i == 0)
    def _():
        h_ref[...] = jnp.zeros_like(h_ref)
        tail_ref[...] = jnp.zeros_like(tail_ref)

    xr = xr_ref[...]
    ext = jnp.concatenate([tail_ref[...], xr], axis=0)
    x = cb_ref[...] + xr * cw_ref[CONV_WIDTH - 1:CONV_WIDTH, :]
    for d in range(1, CONV_WIDTH):
        shifted = pltpu.roll(ext, d, 0)[8:8 + ts]
        x = x + shifted * cw_ref[CONV_WIDTH - 1 - d:CONV_WIDTH - d, :]
    tail_ref[...] = xr[ts - 8:ts]

    xb = x.astype(BF16)
    gl = []
    for g in range(2):
        tiles = [jnp.dot(xb[:, a * LANE:(a + 4) * LANE], wg_ref[g, j],
                         preferred_element_type=F32) for j, a in enumerate(starts)]
        gl.append(jnp.concatenate(tiles, axis=1) + bg_ref[g:g + 1, :])
    r = jax.nn.sigmoid(gl[0])
    ig = jax.nn.sigmoid(gl[1])
    log_a = (-LRU_C) * r * _softplus(-lam_ref[...])
    a = jnp.exp(log_a)
    z = -jnp.tanh(log_a) * (a * a + 1.0)
    b = jnp.where(z > 0.0, z * lax.rsqrt(z), 0.0) * (ig * x)

    row = lax.broadcasted_iota(jnp.int32, a.shape, 0)
    d = 1
    while d < ts:
        if d < 8:
            keep = row >= d
            b = b + a * jnp.where(keep, pltpu.roll(b, d, 0), 0.0)
            a = a * jnp.where(keep, pltpu.roll(a, d, 0), 1.0)
        else:
            b = jnp.concatenate([b[:d], b[d:] + a[d:] * b[:ts - d]], axis=0)
            a = jnp.concatenate([a[:d], a[d:] * a[:ts - d]], axis=0)
        d *= 2
    h = a * h_ref[0:1, :] + b
    h_ref[0:1, :] = h[ts - 1:ts, :]
    o_ref[...] = (h * jax.nn.gelu(y_ref[...])).astype(o_ref.dtype)


def rglru_scan(proj, conv_w, conv_b, wband, b_gate, lam, ts=128):
    B, S, _ = proj.shape
    C = D_RNN
    starts = _gate_band_starts()
    vec = lambda n: pl.BlockSpec((n, C), lambda b, i: (0, 0))
    return pl.pallas_call(
        functools.partial(_rglru_kernel, ts=ts, starts=starts),
        grid=(B, S // ts),
        in_specs=[pl.BlockSpec((None, ts, C), lambda b, i: (b, i, 0)),
                  pl.BlockSpec((None, ts, C), lambda b, i: (b, i, 1)),
                  vec(CONV_WIDTH), vec(1),
                  pl.BlockSpec(wband.shape, lambda b, i: (0, 0, 0, 0)),
                  vec(2), vec(1)],
        out_specs=pl.BlockSpec((None, ts, C), lambda b, i: (b, i, 0)),
        out_shape=jax.ShapeDtypeStruct((B, S, C), BF16),
        scratch_shapes=[pltpu.VMEM((8, C), F32), pltpu.VMEM((8, C), F32)],
        compiler_params=_cparams(("arbitrary", "arbitrary")),
        name="rglru_scan",
    )(proj, proj, conv_w, conv_b.reshape(1, C), wband, b_gate, lam.reshape(1, C))


def _gate_band_weights(w_gate):
    starts = _gate_band_starts()
    dense = jnp.stack([jax.scipy.linalg.block_diag(*[w_gate[g, n] for n in range(RNN_BLOCKS)])
                       for g in range(2)])
    tiles = [dense[:, a * LANE:(a + 4) * LANE, j * LANE:(j + 1) * LANE]
             for j, a in enumerate(starts)]
    return jnp.stack(tiles, axis=1).astype(BF16)


def _row_tile(T):
    return 512 if T % 512 == 0 else T


def _proj_tile(S):
    return 1024 if S % 1024 == 0 else _row_tile(S)


RESIDUAL = dict(epilogue="residual")


def out_proj(a, w, x, mod, gate_blk, tk, tail, tm=None):
    B, S, D = x.shape
    T = B * S
    out = matmul(a.reshape(T, -1), w.astype(BF16), tm=tm or _row_tile(T), tn=D, tk=tk,
                 res=x.reshape(T, D), mod=mod, gate_blk=gate_blk, rows_per_batch=S, **tail)
    if isinstance(out, (list, tuple)):
        return tuple(o.reshape(B, S, D) for o in out)
    return out.reshape(B, S, D)


def nsa_mixer(hn, x, mod, gate_blk, w_in, cmp_pe, cmp_w1, cmp_w2, w_out, cos2, sin2,
              tail=RESIDUAL):
    B, S, D = x.shape
    T = B * S
    G = NSA_KV_GROUPS
    hq = NSA_HEADS * HEAD_DIM
    kvw = NSA_KV_WIDTH
    main = hq + 6 * kvw
    tm = _proj_tile(S)
    hn2 = hn.reshape(T, D)
    w_gate = jnp.pad(w_in[:, main:], ((0, 0), (0, LANE - 3 * NSA_HEADS))).astype(BF16)
    gate_logits = matmul(hn2, w_gate, tm=tm, tn=LANE, tk=D).reshape(B, S, LANE)

    q_cmp, q_rot = proj_heads(hn2, w_in[:, :hq].astype(BF16), cos2, sin2,
                              [(SCALE_Q,) * NSA_HEADS, (ROPE_Q,) * NSA_HEADS], tm=tm)
    g_rope, g_cast = (ROPE,) * G, (CAST,) * G
    (kvb,) = proj_heads(hn2, w_in[:, hq + 2 * kvw:main].astype(BF16), cos2, sin2,
                        [g_rope + g_cast + g_rope + g_cast], tm=tm)
    q_rot = q_rot.reshape(B, 1, S, hq)
    kvb = kvb.reshape(B, 1, S, 4 * kvw)

    npiece = S // CMP_STRIDE
    kv_cmp = matmul(hn2, w_in[:, hq:hq + 2 * kvw].astype(BF16), tm=tm, tn=2 * kvw, tk=D)
    kvc = compress(kv_cmp.reshape(B, S, 2 * kvw), cmp_pe, cmp_w1.astype(BF16),
                   cmp_w2.astype(BF16))

    n_sb = S // SEL_BLOCK
    cmp_start = np.arange(npiece) * CMP_STRIDE
    sel_start = np.arange(n_sb) * SEL_BLOCK
    overlap = ((cmp_start[:, None] < sel_start[None, :] + SEL_BLOCK)
               & (cmp_start[:, None] + CMP_BLOCK > sel_start[None, :]))
    overlap[npiece - 1] = False
    overlap_t = jnp.asarray(overlap.T, BF16)
    onehot = jnp.asarray(np.arange(S)[:, None] // SEL_BLOCK == np.arange(n_sb)[None, :], BF16)

    o_cmp, imp_t = cmp_attention(q_cmp.reshape(B, S, hq), kvc, overlap_t)
    mfeat = topk_select(imp_t)
    kb = kvw // LANE
    o_sel = sel_attention(q_rot.reshape(B, S, hq), mfeat, kvb.reshape(B, S, 4 * kvw),
                          onehot, k_blk=0, v_blk=kb)
    (o_win,) = banded_attention(q_rot, kvb, kvb, n_kv=G, R=NSA_REP, k_blk=2 * kb, v_blk=3 * kb,
                                nwin=(NSA_WINDOW - 1 + LANE - 1) // LANE,
                                window=NSA_WINDOW - 1, want_lse=False)
    o = nsa_combine(gate_logits, o_cmp, o_sel, o_win.reshape(B, S, hq))
    return out_proj(o, w_out, x, mod, gate_blk, hq, tail)


def dilated_mixer(hn, x, mod, gate_blk, w_in, w_out, cos2, sin2, tail=RESIDUAL):
    B, S, D = x.shape
    T = B * S
    H = DIL_HEADS
    hw = H * HEAD_DIM
    tm = _proj_tile(S)
    outs, lses = [], []
    for g, (window, dil) in enumerate(DIL_PATTERNS):
        w = window // dil
        L = S // dil
        regroup = lambda t: t.reshape(t.shape[:-2] + (L, dil, t.shape[-1])).swapaxes(-3, -2)
        hn_g = regroup(hn).reshape(T, D)
        cos_g, sin_g = regroup(cos2).reshape(S, LANE), regroup(sin2).reshape(S, LANE)
        q, k, v = [proj_heads(hn_g, w_in, cos_g, sin_g, [(kind,) * H], tm=tm,
                              col_blk=3 * g + c)[0].reshape(B, dil, L, hw)
                   for c, kind in enumerate((ROPE_Q, ROPE, CAST))]
        o, lse = banded_attention(q, k, v, n_kv=H, R=1, nwin=(w + LANE - 1) // LANE, window=w,
                                  want_lse=True)
        outs.append(o)
        lses.append(lse)
    o = dil_combine(outs, lses)
    return out_proj(o, w_out, x, mod, gate_blk, hw, tail)


def rglru_mixer(hn, x, mod, gate_blk, w_in, conv_w, conv_b, w_gate, b_gate, lam, w_out,
                tail=RESIDUAL):
    B, S, D = x.shape
    T = B * S
    tm = _row_tile(T)
    proj = matmul(hn.reshape(T, D), w_in.astype(BF16), tm=tm, tn=D_RNN, tk=D)
    hy = rglru_scan(proj.reshape(B, S, 2 * D_RNN), conv_w, conv_b, _gate_band_weights(w_gate),
                    b_gate, lam)
    return out_proj(hy, w_out, x, mod, gate_blk, D_RNN, tail)


def mlp(hn, x, mod, gate_blk, w1, w2, tail=RESIDUAL):
    B, S, D = x.shape
    T = B * S
    h = matmul(hn.reshape(T, D), w1.astype(BF16), tm=2 * _row_tile(T) if T % 1024 == 0 else T,
               tn=2048, tk=D, out_dtype=BF16, epilogue="relu2")
    return out_proj(h, w2, x, mod, gate_blk, 1024, tail, tm=1024 if T % 1024 == 0 else None)


def kernel(x, c, l0_w_ada, l0_b_ada, l0_norm1, l0_w_in, l0_cmp_pe, l0_cmp_w1, l0_cmp_w2, l0_w_out, l0_norm2, l0_w_ff1, l0_w_ff2, l1_w_ada, l1_b_ada, l1_norm1, l1_w_in, l1_w_out, l1_norm2, l1_w_ff1, l1_w_ff2, l2_w_ada, l2_b_ada, l2_norm1, l2_w_in, l2_conv_w, l2_conv_b, l2_w_gate, l2_b_gate, l2_lambda, l2_w_out, l2_norm2, l2_w_ff1, l2_w_ff2, l3_w_ada, l3_b_ada, l3_norm1, l3_w_in, l3_cmp_pe, l3_cmp_w1, l3_cmp_w2, l3_w_out, l3_norm2, l3_w_ff1, l3_w_ff2, norm_f):
    B, S, D = x.shape
    cos2, sin2 = rope_tables(S)
    layers = (
        (l0_w_ada, l0_b_ada, l0_norm1, l0_norm2, l0_w_ff1, l0_w_ff2,
         (l0_w_in, l0_cmp_pe, l0_cmp_w1, l0_cmp_w2, l0_w_out)),
        (l1_w_ada, l1_b_ada, l1_norm1, l1_norm2, l1_w_ff1, l1_w_ff2, (l1_w_in, l1_w_out)),
        (l2_w_ada, l2_b_ada, l2_norm1, l2_norm2, l2_w_ff1, l2_w_ff2,
         (l2_w_in, l2_conv_w, l2_conv_b, l2_w_gate, l2_b_gate, l2_lambda, l2_w_out)),
        (l3_w_ada, l3_b_ada, l3_norm1, l3_norm2, l3_w_ff1, l3_w_ff2,
         (l3_w_in, l3_cmp_pe, l3_cmp_w1, l3_cmp_w2, l3_w_out)),
    )
    mods = [adaln(c, l[0], l[1]).reshape(B, 1, 6 * D) for l in layers]
    hn = modulate(x, layers[0][2], mods[0], 0, 1)
    for li in range(DEPTH):
        _, _, _, n2, ff1, ff2, mix = layers[li]
        mod = mods[li]
        tail = dict(epilogue="residual_norm", norm_gain=n2, norm_mod=mod, shift_blk=3, scale_blk=4)
        kind = li % N_MIXERS
        if kind == 0:
            x, hn = nsa_mixer(hn, x, mod, 2, *mix, cos2, sin2, tail=tail)
        elif kind == 1:
            x, hn = dilated_mixer(hn, x, mod, 2, *mix, cos2, sin2, tail=tail)
        else:
            x, hn = rglru_mixer(hn, x, mod, 2, *mix, tail=tail)
        if li + 1 < DEPTH:
            tail = dict(epilogue="residual_norm", norm_gain=layers[li + 1][2],
                        norm_mod=mods[li + 1], shift_blk=0, scale_blk=1)
            x, hn = mlp(hn, x, mod, 5, ff1, ff2, tail=tail)
        else:
            return mlp(hn, x, mod, 5, ff1, ff2,
                       tail=dict(epilogue="residual_final", norm_gain=norm_f))
```
